```python
import jax, jax.numpy as jnp
from jax import lax
import numpy as np

D_MODEL = 1024
BATCH = 32
SEQ = 256
DEPTH = 2
DEC_BATCH = 4
DEC_SEQ = 1024
PAST_LEN = 256

GRID_W = 64
MLSTM_H = 4
MLSTM_DH = 256
MLSTM_W = MLSTM_H * MLSTM_DH
CHUNK = 128
FOURIER_G = 4
FOURIER_GW = 256
FOURIER_W = FOURIER_G * FOURIER_GW
CONV_W = 1024
N_BRANCH = 3
D_FF = -(-8 * D_MODEL // (3 * 256)) * 256
ALPHA = (2 * DEPTH) ** 0.25
BETA = (8 * DEPTH) ** -0.25
LN_EPS = 1e-5
POS_BASE = 10000.0
IN_SPLITS = (MLSTM_W,) * 4 + (MLSTM_H,) * 4 + (FOURIER_W,) + (CONV_W,) * 3 + (D_MODEL,) * N_BRANCH
N_IN = int(sum(IN_SPLITS))
IN_OFFSETS = tuple(int(o) for o in np.cumsum(IN_SPLITS)[:-1])
F_FWD_OFF = 4 * MLSTM_W + MLSTM_H
F_BWD_OFF = 4 * MLSTM_W + 3 * MLSTM_H

kernel_name = "hybrid_mlstm_fourier_conv_diffusion_step"


def layer_norm(x, g, b):
    xf = x.astype(jnp.float32)
    mu = xf.mean(-1, keepdims=True)
    var = jnp.square(xf - mu).mean(-1, keepdims=True)
    return ((xf - mu) * lax.rsqrt(var + LN_EPS) * g + b).astype(x.dtype)


def grid_pos_embed(rows, dtype):
    quarter = D_MODEL // 4
    freqs = POS_BASE ** (-jnp.arange(quarter, dtype=jnp.float32) / quarter)
    r = jnp.repeat(jnp.arange(rows, dtype=jnp.float32), GRID_W)[:, None] * freqs
    col = jnp.tile(jnp.arange(GRID_W, dtype=jnp.float32), rows)[:, None] * freqs
    return jnp.concatenate([jnp.sin(r), jnp.cos(r), jnp.sin(col), jnp.cos(col)], axis=-1).astype(dtype)


def mlstm_scan(q, k, v, i_pre, logf, C0, n0, m0):
    B, H, T, DH = q.shape
    nc = T // CHUNK
    def split(a):
        return jnp.moveaxis(a.reshape(B, H, nc, CHUNK, *a.shape[3:]), 2, 0)
    mask = jnp.tril(jnp.ones((CHUNK, CHUNK), dtype=bool))

    def step(carry, inp):
        C, n, m = carry
        qc, kc, vc, ic, fc = inp
        b = jnp.cumsum(fc, axis=-1)
        Dm = jnp.where(mask, b[..., :, None] - b[..., None, :] + ic[..., None, :], -jnp.inf)
        inter = b + m[..., None]
        m_t = jnp.maximum(inter, Dm.max(-1))
        s = jnp.einsum('bhtd,bhsd->bhts', qc, kc) * jnp.exp(Dm - m_t[..., None])
        a = jnp.exp(inter - m_t)
        num = a[..., None] * jnp.einsum('bhtd,bhde->bhte', qc, C) + jnp.einsum('bhts,bhse->bhte', s, vc)
        den = a * jnp.einsum('bhtd,bhd->bht', qc, n) + s.sum(-1)
        h = num / jnp.maximum(jnp.abs(den), jnp.exp(-m_t))[..., None]
        g = b[..., -1]
        dec = g[..., None] - b + ic
        m_new = jnp.maximum(g + m, dec.max(-1))
        a_c = jnp.exp(g + m - m_new)
        wk = jnp.exp(dec - m_new[..., None])
        C_new = a_c[..., None, None] * C + jnp.einsum('bhs,bhsd,bhse->bhde', wk, kc, vc)
        n_new = a_c[..., None] * n + jnp.einsum('bhs,bhsd->bhd', wk, kc)
        return (C_new, n_new, m_new), h

    carry0 = (C0.astype(jnp.float32), n0.astype(jnp.float32), m0.astype(jnp.float32))
    (C, n, m), hs = lax.scan(step, carry0, (split(q), split(k), split(v), split(i_pre), split(logf)))
    h = jnp.moveaxis(hs, 0, 2).reshape(B, H, T, DH)
    return h, (C, n, m)


def mlstm_bidir(q, k, v, i_f, lf_f, i_b, lf_b, C0, n0, m0):
    h_f, s_f = mlstm_scan(q, k, v, i_f, lf_f, C0[:, 0], n0[:, 0], m0[:, 0])
    rev = lambda a: jnp.flip(a, axis=2)
    h_b, s_b = mlstm_scan(rev(q), rev(k), rev(v), rev(i_b), rev(lf_b), C0[:, 1], n0[:, 1], m0[:, 1])
    states = (jnp.stack([s_f[0], s_b[0]], axis=1), jnp.stack([s_f[1], s_b[1]], axis=1),
              jnp.stack([s_f[2], s_b[2]], axis=1))
    return h_f + rev(h_b), states


def conv3(u, w, b):
    up = jnp.pad(u, ((0, 0), (1, 1), (0, 0)))
    return up[:, :-2] * w[0] + up[:, 1:-1] * w[1] + up[:, 2:] * w[2] + b


def token_mixer(u, l, P, init_state, rows):
    B, T, _ = u.shape
    f32 = jnp.float32
    z = u @ P['w_in'][l] + P['b_in'][l]
    (q, k, v, o, i_f, f_f, i_b, f_b, xfo, cb, cc, cx, gm, gf, gc) = jnp.split(z, IN_OFFSETS, axis=-1)
    heads = lambda a: a.reshape(B, T, MLSTM_H, MLSTM_DH).transpose(0, 2, 1, 3).astype(f32)
    gate = lambda a: a.transpose(0, 2, 1).astype(f32)
    h, new_state = mlstm_bidir(heads(q), heads(k) * (MLSTM_DH ** -0.5), heads(v),
                               gate(i_f), jax.nn.log_sigmoid(gate(f_f)),
                               gate(i_b), jax.nn.log_sigmoid(gate(f_b)), *init_state)
    mu = h.mean(-1, keepdims=True)
    var = jnp.square(h - mu).mean(-1, keepdims=True)
    h = ((h - mu) * lax.rsqrt(var + LN_EPS)).transpose(0, 2, 1, 3).reshape(B, T, MLSTM_W)
    h_m = (jax.nn.sigmoid(o.astype(f32)) * h * P['mlstm_norm_g'][l]).astype(u.dtype)
    xg = xfo.reshape(B, T, FOURIER_G, FOURIER_GW).astype(f32)
    h_f = jnp.fft.fftn(xg, axes=(1, 3), norm='ortho').real.reshape(B, T, FOURIER_W).astype(u.dtype)
    uc = cc * cx
    if rows is None:
        yc = conv3(uc, P['conv_w'][l], P['conv_b'][l])
    else:
        yc = conv3(uc.reshape(B * rows, GRID_W, CONV_W), P['conv_w'][l], P['conv_b'][l]).reshape(B, T, CONV_W)
    h_c = cb * yc
    merged = (jax.nn.sigmoid(gm) * (h_m @ P['w_br_mlstm'][l])
              + jax.nn.sigmoid(gf) * (h_f @ P['w_br_fourier'][l])
              + jax.nn.sigmoid(gc) * (h_c @ P['w_br_conv'][l]))
    return merged @ P['w_out'][l], new_state


def swiglu(u, w_gate_up, w_down):
    a, b = jnp.split(u @ w_gate_up, 2, axis=-1)
    return (jax.nn.silu(a) * b) @ w_down


def ada_mod(cond, l, P):
    m = jax.nn.silu(cond) @ P['w_ada'][l] + P['b_ada'][l]
    return [a[:, None, :] for a in jnp.split(m, 6, axis=-1)]


def trunk_layer(x, mods, l, P, init_state, rows):
    sh1, sc1, g1, sh2, sc2, g2 = mods
    y, st = token_mixer(x * (1 + sc1) + sh1, l, P, init_state, rows)
    x = layer_norm(ALPHA * x + g1 * y, P['ln_g'][l, 0], P['ln_b'][l, 0])
    y = swiglu(x * (1 + sc2) + sh2, P['w_gate_up'][l], P['w_down'][l])
    x = layer_norm(ALPHA * x + g2 * y, P['ln_g'][l, 1], P['ln_b'][l, 1])
    return x, st


def setup_inputs(seed: int = 0) -> dict:
    key = jax.random.key(seed)
    ks = jax.random.split(key, 24)
    f32 = jnp.float32
    nrm = lambda k, shape, s: s * jax.random.normal(k, shape, f32)
    fgate_bias = jnp.linspace(3.0, 6.0, MLSTM_H, dtype=f32)
    b_in = nrm(ks[10], (DEPTH, N_IN), 0.02)
    b_in = b_in.at[:, F_FWD_OFF:F_FWD_OFF + MLSTM_H].add(fgate_bias)
    b_in = b_in.at[:, F_BWD_OFF:F_BWD_OFF + MLSTM_H].add(fgate_bias)
    return {
        'x_prompt': nrm(ks[0], (BATCH, SEQ, D_MODEL), 1.0),
        'x_sample': nrm(ks[1], (DEC_BATCH, DEC_SEQ, D_MODEL), 1.0),
        'state_C': nrm(ks[2], (DEC_BATCH, DEPTH, 2, MLSTM_H, MLSTM_DH, MLSTM_DH), 0.05),
        'state_n': nrm(ks[3], (DEC_BATCH, DEPTH, 2, MLSTM_H, MLSTM_DH), 0.05),
        'state_m': nrm(ks[4], (DEC_BATCH, DEPTH, 2, MLSTM_H), 0.5),
        'c': nrm(ks[5], (DEC_BATCH, D_MODEL), 1.0),
        'c_ctx': nrm(ks[6], (D_MODEL,), 1.0),
        'w_ada': nrm(ks[7], (DEPTH, D_MODEL, 6 * D_MODEL), 0.5 * D_MODEL ** -0.5),
        'b_ada': nrm(ks[8], (DEPTH, 6 * D_MODEL), 0.02),
        'w_in': nrm(ks[9], (DEPTH, D_MODEL, N_IN), D_MODEL ** -0.5),
        'b_in': b_in,
        'mlstm_norm_g': 1.0 + nrm(ks[11], (DEPTH, MLSTM_W), 0.02),
        'conv_w': nrm(ks[12], (DEPTH, 3, CONV_W), 3 ** -0.5),
        'conv_b': nrm(ks[13], (DEPTH, CONV_W), 0.02),
        'w_br_mlstm': nrm(ks[14], (DEPTH, MLSTM_W, D_MODEL), BETA * MLSTM_W ** -0.5),
        'w_br_fourier': nrm(ks[15], (DEPTH, FOURIER_W, D_MODEL), BETA * FOURIER_W ** -0.5),
        'w_br_conv': nrm(ks[16], (DEPTH, CONV_W, D_MODEL), BETA * CONV_W ** -0.5),
        'w_out': nrm(ks[17], (DEPTH, D_MODEL, D_MODEL), BETA * D_MODEL ** -0.5),
        'ln_g': 1.0 + nrm(ks[18], (DEPTH, 2, D_MODEL), 0.02),
        'ln_b': nrm(ks[19], (DEPTH, 2, D_MODEL), 0.02),
        'w_gate_up': nrm(ks[20], (DEPTH, D_MODEL, 2 * D_FF), D_MODEL ** -0.5),
        'w_down': nrm(ks[21], (DEPTH, D_FF, D_MODEL), BETA * D_FF ** -0.5),
    }


def reference(x_prompt, x_sample, state_C, state_n, state_m, c, c_ctx, w_ada, b_ada, w_in, b_in,
              mlstm_norm_g, conv_w, conv_b, w_br_mlstm, w_br_fourier, w_br_conv, w_out, ln_g, ln_b,
              w_gate_up, w_down):
    P = {'w_ada': w_ada, 'b_ada': b_ada, 'w_in': w_in, 'b_in': b_in, 'mlstm_norm_g': mlstm_norm_g,
         'conv_w': conv_w, 'conv_b': conv_b, 'w_br_mlstm': w_br_mlstm, 'w_br_fourier': w_br_fourier,
         'w_br_conv': w_br_conv, 'w_out': w_out, 'ln_g': ln_g, 'ln_b': ln_b,
         'w_gate_up': w_gate_up, 'w_down': w_down}
    f32 = jnp.float32
    Bp = x_prompt.shape[0]
    zero_state = (jnp.zeros((Bp, 2, MLSTM_H, MLSTM_DH, MLSTM_DH), f32),
                  jnp.zeros((Bp, 2, MLSTM_H, MLSTM_DH), f32),
                  jnp.zeros((Bp, 2, MLSTM_H), f32))
    xp = x_prompt
    Cs, ns, ms = [], [], []
    for l in range(DEPTH):
        xp, (Cl, nl, ml) = trunk_layer(xp, ada_mod(c_ctx[None, :], l, P), l, P, zero_state, None)
        Cs.append(Cl)
        ns.append(nl)
        ms.append(ml)
    y_prompt = xp
    new_state_C = jnp.stack(Cs, axis=1)
    new_state_n = jnp.stack(ns, axis=1)
    new_state_m = jnp.stack(ms, axis=1)
    rows = x_sample.shape[1] // GRID_W
    xs = x_sample + grid_pos_embed(rows, x_sample.dtype)
    for l in range(DEPTH):
        xs, _ = trunk_layer(xs, ada_mod(c, l, P), l, P,
                            (state_C[:, l], state_n[:, l], state_m[:, l]), rows)
    y_sample = xs
    return (y_prompt, y_sample, new_state_C, new_state_n, new_state_m)
```

```python
import functools

import numpy as np
import jax
import jax.numpy as jnp
from jax import lax
from jax.experimental import pallas as pl
from jax.experimental.pallas import tpu as pltpu

D_MODEL = 1024
DEPTH = 2
GRID_W = 64
MLSTM_H = 4
MLSTM_DH = 256
MLSTM_W = MLSTM_H * MLSTM_DH
CHUNK = 128
FOURIER_G = 4
FOURIER_GW = 256
FOURIER_W = FOURIER_G * FOURIER_GW
CONV_W = 1024
D_FF = -(-8 * D_MODEL // (3 * 256)) * 256
ALPHA = (2 * DEPTH) ** 0.25
LN_EPS = 1e-5
POS_BASE = 10000.0

OFF_GATES = 4 * MLSTM_W
OFF_FOURIER = OFF_GATES + 4 * MLSTM_H
OFF_CONV = OFF_FOURIER + FOURIER_W
OFF_MERGE = OFF_CONV + 3 * CONV_W
N_IN = OFF_MERGE + 3 * D_MODEL

LANES = 128
VMEM_LIMIT = 56 * 1024 * 1024

BF16 = jnp.bfloat16
F32 = jnp.float32


def _cparams(n_axes):
    return pltpu.CompilerParams(dimension_semantics=("arbitrary",) * n_axes,
                                vmem_limit_bytes=VMEM_LIMIT)


def _resident(shape):
    zeros = (0,) * len(shape)
    return pl.BlockSpec(shape, lambda *_: zeros, pipeline_mode=pl.Buffered(1))


def _sigmoid(x):
    return 1.0 / (1.0 + jnp.exp(-x))


def _log_sigmoid(x):
    return jnp.minimum(x, 0.0) - jnp.log1p(jnp.exp(-jnp.abs(x)))


def _layer_norm(x, g, b):
    mu = jnp.mean(x, axis=-1, keepdims=True)
    xc = x - mu
    var = jnp.mean(xc * xc, axis=-1, keepdims=True)
    return xc * lax.rsqrt(var + LN_EPS) * g + b


def _dot(a, b):
    return jnp.dot(a, b, preferred_element_type=F32)


def _ada_kernel(cond_ref, w_ref, b_ref, out_ref):
    cond = cond_ref[...]
    act = (cond * _sigmoid(cond)).astype(BF16)
    out_ref[...] = _dot(act, w_ref[...].astype(BF16)) + b_ref[...]


def _ada_mods(cond, w_ada, b_ada):
    tn = 1536
    nb = 6 * D_MODEL // tn
    return pl.pallas_call(
        _ada_kernel,
        out_shape=jax.ShapeDtypeStruct((DEPTH, 8, 6 * D_MODEL), F32),
        grid=(DEPTH, nb),
        in_specs=[pl.BlockSpec((8, D_MODEL), lambda l, j: (0, 0)),
                  pl.BlockSpec((None, D_MODEL, tn), lambda l, j: (l, 0, j)),
                  pl.BlockSpec((None, 1, tn), lambda l, j: (l, 0, j))],
        out_specs=pl.BlockSpec((None, 8, tn), lambda l, j: (l, 0, j)),
        compiler_params=_cparams(2),
        name="ada_mods",
    )(cond, w_ada, b_ada.reshape(DEPTH, 1, 6 * D_MODEL))


def _add_pos_kernel(x_ref, pos_ref, out_ref):
    out_ref[...] = x_ref[...] + pos_ref[...]


def _add_pos(x, pos):
    b, t, d = x.shape
    return pl.pallas_call(
        _add_pos_kernel,
        out_shape=jax.ShapeDtypeStruct(x.shape, x.dtype),
        grid=(b,),
        in_specs=[pl.BlockSpec((None, t, d), lambda i: (i, 0, 0)),
                  pl.BlockSpec((t, d), lambda i: (0, 0))],
        out_specs=pl.BlockSpec((None, t, d), lambda i: (i, 0, 0)),
        compiler_params=_cparams(1),
        name="add_pos",
    )(x, pos)


def _grid_pos_embed(rows, dtype):
    quarter = D_MODEL // 4
    freqs = POS_BASE ** (-jnp.arange(quarter, dtype=F32) / quarter)
    r = jnp.repeat(jnp.arange(rows, dtype=F32), GRID_W)[:, None] * freqs
    col = jnp.tile(jnp.arange(GRID_W, dtype=F32), rows)[:, None] * freqs
    return jnp.concatenate([jnp.sin(r), jnp.cos(r), jnp.sin(col), jnp.cos(col)], axis=-1).astype(dtype)


def _inproj_kernel(x_ref, sh_ref, sc_ref, w_ref, b_ref, wg_ref, bg_ref,
                   qkv_ref, o_ref, f_ref, g_ref):
    u = (x_ref[...] * (1.0 + sc_ref[...]) + sh_ref[...]).astype(BF16)
    d = D_MODEL
    for j in range(3):
        qkv_ref[:, j * d:(j + 1) * d] = (
            _dot(u, w_ref[:, j * d:(j + 1) * d]) + b_ref[:, j * d:(j + 1) * d]).astype(BF16)
    o_ref[...] = _dot(u, w_ref[:, 3 * d:4 * d]) + b_ref[:, 3 * d:4 * d]
    f_ref[...] = _dot(u, w_ref[:, 4 * d:5 * d]) + b_ref[:, 4 * d:5 * d]
    g_ref[...] = _dot(u, wg_ref[...]) + bg_ref[...]


def _inproj(x2d, mods, seq_len, w_a, b_a, w_g, b_g):
    n = x2d.shape[0]
    tm = 512
    per_mod = max(seq_len // tm, 1) if mods.shape[0] > 1 else n
    mrow = (lambda i: i // per_mod) if mods.shape[0] > 1 else (lambda i: 0)
    d = D_MODEL
    return pl.pallas_call(
        _inproj_kernel,
        out_shape=(jax.ShapeDtypeStruct((n, 3 * d), BF16),
                   jax.ShapeDtypeStruct((n, d), F32),
                   jax.ShapeDtypeStruct((n, d), F32),
                   jax.ShapeDtypeStruct((n, LANES), F32)),
        grid=(n // tm,),
        in_specs=[pl.BlockSpec((tm, d), lambda i: (i, 0)),
                  pl.BlockSpec((None, 1, d), lambda i: (mrow(i), 0, 0)),
                  pl.BlockSpec((None, 1, d), lambda i: (mrow(i), 0, 1)),
                  _resident(w_a.shape), _resident(b_a.shape),
                  _resident(w_g.shape), _resident(b_g.shape)],
        out_specs=(pl.BlockSpec((tm, 3 * d), lambda i: (i, 0)),
                   pl.BlockSpec((tm, d), lambda i: (i, 0)),
                   pl.BlockSpec((tm, d), lambda i: (i, 0)),
                   pl.BlockSpec((tm, LANES), lambda i: (i, 0))),
        compiler_params=_cparams(1),
        name="inproj",
    )(x2d, mods, mods, w_a, b_a, w_g, b_g)


def _split_dot_ones_rhs(x, ones_mat):
    x0 = x.astype(BF16)
    r1 = x - x0.astype(F32)
    x1 = r1.astype(BF16)
    x2 = (r1 - x1.astype(F32)).astype(BF16)
    return _dot(x0, ones_mat) + _dot(x1, ones_mat) + _dot(x2, ones_mat)


def _split_dot_ones_lhs(ones_mat, x):
    x0 = x.astype(BF16)
    r1 = x - x0.astype(F32)
    x1 = r1.astype(BF16)
    x2 = (r1 - x1.astype(F32)).astype(BF16)
    return _dot(ones_mat, x0) + _dot(ones_mat, x1) + _dot(ones_mat, x2)


def _mlstm_kernel(*refs, seq_len, has_init, emit_state):
    it = iter(refs)
    q_ref, k_ref, v_ref, o_ref, g_ref, ng_ref = (next(it) for _ in range(6))
    if has_init:
        c0_ref, n0_ref, m0_ref = (next(it) for _ in range(3))
    hm_ref = next(it)
    if emit_state:
        cn_ref, nn_ref, mn_ref = (next(it) for _ in range(3))
    (gt_s, prow_s, srow_s, pcol_s, scol_s, hacc_s, c_s, n_s, m_s) = it

    nc = seq_len // CHUNK
    H, DH = MLSTM_H, MLSTM_DH
    ri = lax.broadcasted_iota(jnp.int32, (CHUNK, CHUNK), 0)
    ci = lax.broadcasted_iota(jnp.int32, (CHUNK, CHUNK), 1)
    lower = ri >= ci
    upper = ri <= ci
    tri_l = jnp.where(lower, 1.0, 0.0).astype(BF16)
    tri_u = jnp.where(upper, 1.0, 0.0).astype(BF16)

    for c in range(nc):
        rows = slice(c * CHUNK, (c + 1) * CHUNK)
        g = g_ref[rows, :]
        lf = _log_sigmoid(g)
        g_t = g.T
        lf_t = lf.T
        gt_s[c] = g_t
        prow_s[c] = _split_dot_ones_rhs(lf_t, tri_u)
        srow_s[c] = _split_dot_ones_rhs(lf_t, tri_l)
        pcol_s[rows, :] = _split_dot_ones_lhs(tri_l, lf)
        scol_s[rows, :] = _split_dot_ones_lhs(tri_u, lf)

    def chunk_step(c, h, backward, last_store):
        r0 = pl.multiple_of(c * CHUNK, CHUNK)
        rows = pl.ds(r0, CHUNK)
        hc = slice(h * DH, (h + 1) * DH)
        gi = (2 * H if backward else 0) + h
        gf = gi + H
        qc = q_ref[rows, hc]
        kc = k_ref[rows, hc] * (MLSTM_DH ** -0.5)
        vc = v_ref[rows, hc]
        g_t = gt_s[c]
        cum_t = srow_s[c] if backward else prow_s[c]
        irow = g_t[gi:gi + 1, :]
        brow = cum_t[gf:gf + 1, :]
        icol = g_ref[rows, :][:, gi:gi + 1]
        bcol = (scol_s if backward else pcol_s)[rows, :][:, gf:gf + 1]
        m_prev = m_s[...]
        c_prev = c_s[...]
        n_prev = n_s[...]

        dm = jnp.where(upper if backward else lower, bcol - brow + irow, -jnp.inf)
        inter = bcol + m_prev
        m_t = jnp.maximum(inter, jnp.max(dm, axis=1, keepdims=True))
        qk = lax.dot_general(qc, kc, (((1,), (1,)), ((), ())), preferred_element_type=F32)
        s = qk * jnp.exp(dm - m_t)
        a = jnp.exp(inter - m_t)
        num = a * _dot(qc, c_prev.astype(BF16)) + _dot(s.astype(BF16), vc)
        qf = qc.astype(F32)
        den = a * jnp.sum(qf * n_prev, axis=1, keepdims=True) + jnp.sum(s, axis=1, keepdims=True)
        hval = num / jnp.maximum(jnp.abs(den), jnp.exp(-m_t))

        if backward:
            g_tot = brow[:, 0:1]
        else:
            g_tot = brow[:, CHUNK - 1:CHUNK]
        dec_row = g_tot - brow + irow
        dec_col = g_tot - bcol + icol
        m_new = jnp.maximum(g_tot + m_prev, jnp.max(dec_row, axis=1, keepdims=True))
        a_c = jnp.exp(g_tot + m_prev - m_new)
        kw = kc.astype(F32) * jnp.exp(dec_col - m_new)
        kv = lax.dot_general(kw.astype(BF16), vc, (((0,), (0,)), ((), ())), preferred_element_type=F32)
        c_s[...] = a_c * c_prev + kv
        n_s[...] = a_c * n_prev + jnp.sum(kw, axis=0, keepdims=True)
        m_s[...] = m_new

        if not backward:
            hacc_s[rows, :] = hval
        else:
            ht = hacc_s[rows, :] + hval
            mu = jnp.mean(ht, axis=1, keepdims=True)
            hcen = ht - mu
            var = jnp.mean(hcen * hcen, axis=1, keepdims=True)
            hn = hcen * lax.rsqrt(var + LN_EPS)
            hm_ref[rows, hc] = (_sigmoid(o_ref[rows, hc]) * hn * ng_ref[:, hc]).astype(BF16)

    for h in range(H):
        for d in range(2):
            if has_init:
                c_s[...] = c0_ref[d, h]
                n_s[...] = n0_ref[h, d:d + 1, :]
                m_s[...] = jnp.full((1, 1), m0_ref[pl.program_id(0), d, h], F32)
            else:
                c_s[...] = jnp.zeros_like(c_s)
                n_s[...] = jnp.zeros_like(n_s)
                m_s[...] = jnp.zeros_like(m_s)

            def body(j, carry, h=h, d=d):
                c = (nc - 1 - j) if d == 1 else j
                chunk_step(c, h, d == 1, None)
                return carry

            lax.fori_loop(0, nc, body, 0)
            if emit_state:
                cn_ref[d, h] = c_s[...]
                nn_ref[h, d:d + 1, :] = n_s[...]
                mn_ref[h, d:d + 1, :] = jnp.broadcast_to(m_s[...], (1, LANES))


def _mlstm(qkv, o, gates, norm_g, batch, seq_len, init_state):
    n = batch * seq_len
    w = MLSTM_W
    H, DH = MLSTM_H, MLSTM_DH
    nc = seq_len // CHUNK
    has_init = init_state is not None
    emit_state = not has_init
    in_specs = [pl.BlockSpec((seq_len, w), lambda b: (b, 0)),
                pl.BlockSpec((seq_len, w), lambda b: (b, 1)),
                pl.BlockSpec((seq_len, w), lambda b: (b, 2)),
                pl.BlockSpec((seq_len, w), lambda b: (b, 0)),
                pl.BlockSpec((seq_len, LANES), lambda b: (b, 0)),
                pl.BlockSpec((1, w), lambda b: (0, 0))]
    args = [qkv, qkv, qkv, o, gates, norm_g]
    if has_init:
        c0, n0, m0 = init_state
        in_specs += [pl.BlockSpec((None, 2, H, DH, DH), lambda b: (b, 0, 0, 0, 0)),
                     pl.BlockSpec((None, H, 2, DH), lambda b: (b, 0, 0, 0)),
                     pl.BlockSpec(memory_space=pltpu.SMEM)]
        args += [c0, n0, m0]
    out_shape = [jax.ShapeDtypeStruct((n, w), BF16)]
    out_specs = [pl.BlockSpec((seq_len, w), lambda b: (b, 0))]
    if emit_state:
        out_shape += [jax.ShapeDtypeStruct((batch, 2, H, DH, DH), F32),
                      jax.ShapeDtypeStruct((batch, H, 2, DH), F32),
                      jax.ShapeDtypeStruct((batch, H, 2, LANES), F32)]
        out_specs += [pl.BlockSpec((None, 2, H, DH, DH), lambda b: (b, 0, 0, 0, 0)),
                      pl.BlockSpec((None, H, 2, DH), lambda b: (b, 0, 0, 0)),
                      pl.BlockSpec((None, H, 2, LANES), lambda b: (b, 0, 0, 0))]
    scratch = [pltpu.VMEM((nc, CHUNK, CHUNK), F32),
               pltpu.VMEM((nc, CHUNK, CHUNK), F32),
               pltpu.VMEM((nc, CHUNK, CHUNK), F32),
               pltpu.VMEM((seq_len, LANES), F32),
               pltpu.VMEM((seq_len, LANES), F32),
               pltpu.VMEM((seq_len, DH), F32),
               pltpu.VMEM((DH, DH), F32),
               pltpu.VMEM((1, DH), F32),
               pltpu.VMEM((1, 1), F32)]
    res = pl.pallas_call(
        functools.partial(_mlstm_kernel, seq_len=seq_len, has_init=has_init, emit_state=emit_state),
        out_shape=tuple(out_shape),
        grid=(batch,),
        in_specs=in_specs,
        out_specs=tuple(out_specs),
        scratch_shapes=scratch,
        compiler_params=_cparams(1),
        name="mlstm_init" if has_init else "mlstm_zero",
    )(*args)
    return res


def _dft_tables(seq_len):
    gw = FOURIER_GW
    kc = np.arange(gw)
    ang_c = 2.0 * np.pi * ((kc[:, None] * kc[None, :]) % gw) / gw
    w_chan = np.concatenate([np.cos(ang_c), np.sin(ang_c)], axis=1)
    kt = np.arange(seq_len)
    ang_t = 2.0 * np.pi * ((kt[:, None] * kt[None, :]) % seq_len) / seq_len
    scale = 1.0 / np.sqrt(seq_len * gw)
    return (jnp.asarray(w_chan, dtype=F32).astype(BF16),
            jnp.asarray(np.cos(ang_t) * scale, dtype=F32).astype(BF16),
            jnp.asarray(-np.sin(ang_t) * scale, dtype=F32).astype(BF16))


def _fourier_kernel(x_ref, wc_ref, ct_ref, st_ref, out_ref):
    gw = FOURIER_GW
    for g in range(FOURIER_G):
        x = x_ref[:, g * gw:(g + 1) * gw].astype(BF16)
        a = _dot(x, wc_ref[...])
        a_c = a[:, :gw].astype(BF16)
        a_s = a[:, gw:].astype(BF16)
        y = _dot(ct_ref[...], a_c) + _dot(st_ref[...], a_s)
        out_ref[:, g * gw:(g + 1) * gw] = y.astype(BF16)


def _fourier(xf, batch, seq_len):
    w_chan, c_t, s_t = _dft_tables(seq_len)
    n = batch * seq_len
    return pl.pallas_call(
        _fourier_kernel,
        out_shape=jax.ShapeDtypeStruct((n, FOURIER_W), BF16),
        grid=(batch,),
        in_specs=[pl.BlockSpec((seq_len, FOURIER_W), lambda b: (b, 0)),
                  _resident(w_chan.shape), _resident(c_t.shape), _resident(s_t.shape)],
        out_specs=pl.BlockSpec((seq_len, FOURIER_W), lambda b: (b, 0)),
        compiler_params=_cparams(1),
        name="fourier",
    )(xf, w_chan, c_t, s_t)


def _merge_kernel(x_ref, sh_ref, sc_ref, gt_ref, hm_ref, hf_ref, wl_ref, bl_ref, cw_ref, cb_ref,
                  wm_ref, wf_ref, wc_ref, wo_ref, lg_ref, lb_ref, out_ref, *, row_len):
    d = D_MODEL
    x = x_ref[...]
    tm = x.shape[0]
    u = (x * (1.0 + sc_ref[...]) + sh_ref[...]).astype(BF16)

    def proj(j):
        return _dot(u, wl_ref[:, j * d:(j + 1) * d]) + bl_ref[:, j * d:(j + 1) * d]

    uc = proj(1) * proj(2)
    pos = lax.broadcasted_iota(jnp.int32, (tm, 1), 0) % row_len
    prev = jnp.where(pos == 0, 0.0, pltpu.roll(uc, 1, 0))
    nxt = jnp.where(pos == row_len - 1, 0.0, pltpu.roll(uc, tm - 1, 0))
    yc = prev * cw_ref[0:1, :] + uc * cw_ref[1:2, :] + nxt * cw_ref[2:3, :] + cb_ref[...]
    h_c = (proj(0) * yc).astype(BF16)
    merged = (_sigmoid(proj(3)) * _dot(hm_ref[...], wm_ref[...])
              + _sigmoid(proj(4)) * _dot(hf_ref[...], wf_ref[...])
              + _sigmoid(proj(5)) * _dot(h_c, wc_ref[...]))
    y = _dot(merged.astype(BF16), wo_ref[...])
    out_ref[...] = _layer_norm(ALPHA * x + gt_ref[...] * y, lg_ref[...], lb_ref[...])


def _merge(x2d, mods, seq_len, row_len, h_m, h_f, w_l, b_l, conv_w, conv_b, w_m, w_f, w_c, w_o, ln_g, ln_b):
    n = x2d.shape[0]
    d = D_MODEL
    tm = 256
    per_mod = seq_len // tm
    mrow = (lambda i: i // per_mod) if mods.shape[0] > 1 else (lambda i: 0)
    tile = lambda i: (i, 0)
    return pl.pallas_call(
        functools.partial(_merge_kernel, row_len=row_len),
        out_shape=jax.ShapeDtypeStruct((n, d), F32),
        grid=(n // tm,),
        in_specs=[pl.BlockSpec((tm, d), tile),
                  pl.BlockSpec((None, 1, d), lambda i: (mrow(i), 0, 0)),
                  pl.BlockSpec((None, 1, d), lambda i: (mrow(i), 0, 1)),
                  pl.BlockSpec((None, 1, d), lambda i: (mrow(i), 0, 2)),
                  pl.BlockSpec((tm, d), tile), pl.BlockSpec((tm, d), tile),
                  _resident(w_l.shape), _resident(b_l.shape),
                  _resident(conv_w.shape), _resident(conv_b.shape),
                  _resident(w_m.shape), _resident(w_f.shape), _resident(w_c.shape), _resident(w_o.shape),
                  _resident(ln_g.shape), _resident(ln_b.shape)],
        out_specs=pl.BlockSpec((tm, d), tile),
        compiler_params=_cparams(1),
        name="merge",
    )(x2d, mods, mods, mods, h_m, h_f, w_l, b_l, conv_w, conv_b, w_m, w_f, w_c, w_o, ln_g, ln_b)


def _ffn_kernel(x_ref, sh_ref, sc_ref, gt_ref, wgu_ref, wd_ref, lg_ref, lb_ref, out_ref):
    x = x_ref[...]
    u = (x * (1.0 + sc_ref[...]) + sh_ref[...]).astype(BF16)
    a = _dot(u, wgu_ref[:, :D_FF])
    b = _dot(u, wgu_ref[:, D_FF:])
    hidden = (a * _sigmoid(a) * b).astype(BF16)
    y = _dot(hidden, wd_ref[...])
    out_ref[...] = _layer_norm(ALPHA * x + gt_ref[...] * y, lg_ref[...], lb_ref[...])


def _ffn(x2d, mods, seq_len, w_gu, w_d, ln_g, ln_b):
    n = x2d.shape[0]
    d = D_MODEL
    tm = 256
    per_mod = seq_len // tm
    mrow = (lambda i: i // per_mod) if mods.shape[0] > 1 else (lambda i: 0)
    tile = lambda i: (i, 0)
    return pl.pallas_call(
        _ffn_kernel,
        out_shape=jax.ShapeDtypeStruct((n, d), F32),
        grid=(n // tm,),
        in_specs=[pl.BlockSpec((tm, d), tile),
                  pl.BlockSpec((None, 1, d), lambda i: (mrow(i), 0, 3)),
                  pl.BlockSpec((None, 1, d), lambda i: (mrow(i), 0, 4)),
                  pl.BlockSpec((None, 1, d), lambda i: (mrow(i), 0, 5)),
                  _resident(w_gu.shape), _resident(w_d.shape),
                  _resident(ln_g.shape), _resident(ln_b.shape)],
        out_specs=pl.BlockSpec((tm, d), tile),
        compiler_params=_cparams(1),
        name="ffn",
    )(x2d, mods, mods, mods, w_gu, w_d, ln_g, ln_b)


def _layer_weights(l, w_in, b_in, mlstm_norm_g, conv_w, conv_b, w_br_mlstm, w_br_fourier, w_br_conv,
                   w_out, ln_g, ln_b, w_gate_up, w_down):
    wi, bi = w_in[l], b_in[l]
    w_a = jnp.concatenate([wi[:, :OFF_GATES], wi[:, OFF_FOURIER:OFF_CONV]], axis=1).astype(BF16)
    b_a = jnp.concatenate([bi[:OFF_GATES], bi[OFF_FOURIER:OFF_CONV]])[None, :]
    pad = LANES - 4 * MLSTM_H
    w_g = jnp.pad(wi[:, OFF_GATES:OFF_FOURIER], ((0, 0), (0, pad))).astype(BF16)
    b_g = jnp.pad(bi[OFF_GATES:OFF_FOURIER], (0, pad))[None, :]
    return dict(
        w_a=w_a, b_a=b_a, w_g=w_g, b_g=b_g,
        w_l=wi[:, OFF_CONV:].astype(BF16), b_l=bi[OFF_CONV:][None, :],
        norm_g=mlstm_norm_g[l][None, :], conv_w=conv_w[l], conv_b=conv_b[l][None, :],
        w_m=w_br_mlstm[l].astype(BF16), w_f=w_br_fourier[l].astype(BF16),
        w_c=w_br_conv[l].astype(BF16), w_o=w_out[l].astype(BF16),
        ln_g1=ln_g[l, 0][None, :], ln_b1=ln_b[l, 0][None, :],
        ln_g2=ln_g[l, 1][None, :], ln_b2=ln_b[l, 1][None, :],
        w_gu=w_gate_up[l].astype(BF16), w_d=w_down[l].astype(BF16))


def _trunk_layer(x2d, mods, batch, seq_len, row_len, wts, init_state):
    qkv, o, xf, gates = _inproj(x2d, mods, seq_len, wts['w_a'], wts['b_a'], wts['w_g'], wts['b_g'])
    res = _mlstm(qkv, o, gates, wts['norm_g'], batch, seq_len, init_state)
    h_m = res[0]
    h_f = _fourier(xf, batch, seq_len)
    x1 = _merge(x2d, mods, seq_len, row_len, h_m, h_f, wts['w_l'], wts['b_l'], wts['conv_w'], wts['conv_b'],
                wts['w_m'], wts['w_f'], wts['w_c'], wts['w_o'], wts['ln_g1'], wts['ln_b1'])
    x2 = _ffn(x1, mods, seq_len, wts['w_gu'], wts['w_d'], wts['ln_g2'], wts['ln_b2'])
    return x2, res[1:]


def kernel(x_prompt, x_sample, state_C, state_n, state_m, c, c_ctx, w_ada, b_ada, w_in, b_in,
           mlstm_norm_g, conv_w, conv_b, w_br_mlstm, w_br_fourier, w_br_conv, w_out, ln_g, ln_b,
           w_gate_up, w_down):
    bp, tp, d = x_prompt.shape
    bs, ts, _ = x_sample.shape
    cond = jnp.concatenate([c_ctx[None, :], c, jnp.zeros((8 - 1 - bs, d), F32)], axis=0)
    mods = _ada_mods(cond, w_ada, b_ada)

    xp = x_prompt.reshape(bp * tp, d)
    xs = _add_pos(x_sample, _grid_pos_embed(ts // GRID_W, x_sample.dtype)).reshape(bs * ts, d)
    cs, ns, ms = [], [], []
    for l in range(DEPTH):
        wts = _layer_weights(l, w_in, b_in, mlstm_norm_g, conv_w, conv_b, w_br_mlstm, w_br_fourier,
                             w_br_conv, w_out, ln_g, ln_b, w_gate_up, w_down)
        mods_ctx = mods[l, 0:1].reshape(1, 1, 6 * d)
        mods_smp = mods[l, 1:1 + bs].reshape(bs, 1, 6 * d)
        xp, (c_new, n_new, m_new) = _trunk_layer(xp, mods_ctx, bp, tp, tp, wts, None)
        cs.append(c_new)
        ns.append(jnp.swapaxes(n_new, 1, 2))
        ms.append(jnp.swapaxes(m_new[..., 0], 1, 2))
        init = (state_C[:, l], jnp.swapaxes(state_n[:, l], 1, 2), state_m[:, l])
        xs, _ = _trunk_layer(xs, mods_smp, bs, ts, GRID_W, wts, init)
    return (xp.reshape(bp, tp, d), xs.reshape(bs, ts, d),
            jnp.stack(cs, axis=1), jnp.stack(ns, axis=1), jnp.stack(ms, axis=1))
```

```python
import functools

import numpy as np
import jax
import jax.numpy as jnp
from jax import lax
from jax.experimental import pallas as pl
from jax.experimental.pallas import tpu as pltpu

D_MODEL = 1024
DEPTH = 2
GRID_W = 64
MLSTM_H = 4
MLSTM_DH = 256
MLSTM_W = MLSTM_H * MLSTM_DH
CHUNK = 128
FOURIER_G = 4
FOURIER_GW = 256
FOURIER_W = FOURIER_G * FOURIER_GW
CONV_W = 1024
D_FF = -(-8 * D_MODEL // (3 * 256)) * 256
ALPHA = (2 * DEPTH) ** 0.25
LN_EPS = 1e-5
POS_BASE = 10000.0

OFF_GATES = 4 * MLSTM_W
OFF_FOURIER = OFF_GATES + 4 * MLSTM_H
OFF_CONV = OFF_FOURIER + FOURIER_W
OFF_MERGE = OFF_CONV + 3 * CONV_W
N_IN = OFF_MERGE + 3 * D_MODEL

LANES = 128
VMEM_LIMIT = 56 * 1024 * 1024

BF16 = jnp.bfloat16
F32 = jnp.float32


def _cparams(n_axes):
    return pltpu.CompilerParams(dimension_semantics=("arbitrary",) * n_axes,
                                vmem_limit_bytes=VMEM_LIMIT)


def _resident(shape):
    zeros = (0,) * len(shape)
    return pl.BlockSpec(shape, lambda *_: zeros, pipeline_mode=pl.Buffered(1))


def _sigmoid(x):
    return 1.0 / (1.0 + jnp.exp(-x))


def _log_sigmoid(x):
    return jnp.minimum(x, 0.0) - jnp.log1p(jnp.exp(-jnp.abs(x)))


def _layer_norm(x, g, b):
    mu = jnp.mean(x, axis=-1, keepdims=True)
    xc = x - mu
    var = jnp.mean(xc * xc, axis=-1, keepdims=True)
    return xc * lax.rsqrt(var + LN_EPS) * g + b


def _dot(a, b):
    return jnp.dot(a, b, preferred_element_type=F32)


def _ada_kernel(cond_ref, w_ref, b_ref, out_ref):
    cond = cond_ref[...]
    act = (cond * _sigmoid(cond)).astype(BF16)
    out_ref[...] = _dot(act, w_ref[...].astype(BF16)) + b_ref[...]


def _ada_mods(cond, w_ada, b_ada):
    tn = 1536
    nb = 6 * D_MODEL // tn
    return pl.pallas_call(
        _ada_kernel,
        out_shape=jax.ShapeDtypeStruct((DEPTH, 8, 6 * D_MODEL), F32),
        grid=(DEPTH, nb),
        in_specs=[pl.BlockSpec((8, D_MODEL), lambda l, j: (0, 0)),
                  pl.BlockSpec((None, D_MODEL, tn), lambda l, j: (l, 0, j)),
                  pl.BlockSpec((None, 1, tn), lambda l, j: (l, 0, j))],
        out_specs=pl.BlockSpec((None, 8, tn), lambda l, j: (l, 0, j)),
        compiler_params=_cparams(2),
        name="ada_mods",
    )(cond, w_ada, b_ada.reshape(DEPTH, 1, 6 * D_MODEL))


def _add_pos_kernel(x_ref, pos_ref, out_ref):
    out_ref[...] = x_ref[...] + pos_ref[...]


def _add_pos(x, pos):
    b, t, d = x.shape
    return pl.pallas_call(
        _add_pos_kernel,
        out_shape=jax.ShapeDtypeStruct(x.shape, x.dtype),
        grid=(b,),
        in_specs=[pl.BlockSpec((None, t, d), lambda i: (i, 0, 0)),
                  pl.BlockSpec((t, d), lambda i: (0, 0))],
        out_specs=pl.BlockSpec((None, t, d), lambda i: (i, 0, 0)),
        compiler_params=_cparams(1),
        name="add_pos",
    )(x, pos)


def _grid_pos_embed(rows, dtype):
    quarter = D_MODEL // 4
    freqs = POS_BASE ** (-jnp.arange(quarter, dtype=F32) / quarter)
    r = jnp.repeat(jnp.arange(rows, dtype=F32), GRID_W)[:, None] * freqs
    col = jnp.tile(jnp.arange(GRID_W, dtype=F32), rows)[:, None] * freqs
    return jnp.concatenate([jnp.sin(r), jnp.cos(r), jnp.sin(col), jnp.cos(col)], axis=-1).astype(dtype)


def _inproj_kernel(x_ref, sh_ref, sc_ref, w_ref, b_ref, wvt_ref, bvt_ref, wg_ref, bg_ref,
                   qk_ref, vt_ref, o_ref, f_ref, g_ref):
    u = (x_ref[...] * (1.0 + sc_ref[...]) + sh_ref[...]).astype(BF16)
    d = D_MODEL
    for j in range(2):
        qk_ref[:, j * d:(j + 1) * d] = (
            _dot(u, w_ref[:, j * d:(j + 1) * d]) + b_ref[:, j * d:(j + 1) * d]).astype(BF16)
    v_t = lax.dot_general(wvt_ref[...], u, (((1,), (1,)), ((), ())), preferred_element_type=F32)
    for c in range(vt_ref.shape[0]):
        vt_ref[c] = (v_t[:, c * CHUNK:(c + 1) * CHUNK] + bvt_ref[...]).astype(BF16)
    o_ref[...] = _dot(u, w_ref[:, 2 * d:3 * d]) + b_ref[:, 2 * d:3 * d]
    f_ref[...] = _dot(u, w_ref[:, 3 * d:4 * d]) + b_ref[:, 3 * d:4 * d]
    g_ref[...] = _dot(u, wg_ref[...]) + bg_ref[...]


def _inproj(x2d, mods, seq_len, w_a, b_a, w_vt, b_vt, w_g, b_g):
    n = x2d.shape[0]
    tm = 512
    per_mod = max(seq_len // tm, 1) if mods.shape[0] > 1 else n
    mrow = (lambda i: i // per_mod) if mods.shape[0] > 1 else (lambda i: 0)
    d = D_MODEL
    return pl.pallas_call(
        _inproj_kernel,
        out_shape=(jax.ShapeDtypeStruct((n, 2 * d), BF16),
                   jax.ShapeDtypeStruct((n // CHUNK, d, CHUNK), BF16),
                   jax.ShapeDtypeStruct((n, d), F32),
                   jax.ShapeDtypeStruct((n, d), F32),
                   jax.ShapeDtypeStruct((n, LANES), F32)),
        grid=(n // tm,),
        in_specs=[pl.BlockSpec((tm, d), lambda i: (i, 0)),
                  pl.BlockSpec((None, 1, d), lambda i: (mrow(i), 0, 0)),
                  pl.BlockSpec((None, 1, d), lambda i: (mrow(i), 0, 1)),
                  _resident(w_a.shape), _resident(b_a.shape),
                  _resident(w_vt.shape), _resident(b_vt.shape),
                  _resident(w_g.shape), _resident(b_g.shape)],
        out_specs=(pl.BlockSpec((tm, 2 * d), lambda i: (i, 0)),
                   pl.BlockSpec((tm // CHUNK, d, CHUNK), lambda i: (i, 0, 0)),
                   pl.BlockSpec((tm, d), lambda i: (i, 0)),
                   pl.BlockSpec((tm, d), lambda i: (i, 0)),
                   pl.BlockSpec((tm, LANES), lambda i: (i, 0))),
        compiler_params=_cparams(1),
        name="inproj",
    )(x2d, mods, mods, w_a, b_a, w_vt, b_vt, w_g, b_g)


def _dot_ones_rhs(x, ones_mat):
    x0 = x.astype(BF16)
    r1 = x - x0.astype(F32)
    x1 = r1.astype(BF16)
    x2 = (r1 - x1.astype(F32)).astype(BF16)
    return _dot(x0, ones_mat) + _dot(x1, ones_mat) + _dot(x2, ones_mat)


def _mlstm_kernel(*refs, seq_len, has_init, emit_state):
    it = iter(refs)
    qk_ref, vt_ref, o_ref, g_ref, ng_ref = (next(it) for _ in range(5))
    if has_init:
        c0_ref, n0_ref, m0_ref = (next(it) for _ in range(3))
    hm_ref = next(it)
    if emit_state:
        cn_ref, nn_ref, mn_ref = (next(it) for _ in range(3))
    lft_s, prow_s, srow_s, wt_s, hf_s, hb_s, ct_s, n_s, m_s = it

    nc = seq_len // CHUNK
    H, DH, W = MLSTM_H, MLSTM_DH, MLSTM_W
    ri = lax.broadcasted_iota(jnp.int32, (CHUNK, CHUNK), 0)
    ci = lax.broadcasted_iota(jnp.int32, (CHUNK, CHUNK), 1)
    lower = ri >= ci
    upper = ri <= ci
    tri_l = jnp.where(lower, 1.0, 0.0).astype(BF16)
    tri_u = jnp.where(upper, 1.0, 0.0).astype(BF16)

    for c in range(nc):
        rows = slice(c * CHUNK, (c + 1) * CHUNK)
        lft_s[rows, :] = _log_sigmoid(g_ref[rows, :]).T
    lft = lft_s[...]
    prow_s[...] = _dot_ones_rhs(lft, tri_u)
    srow_s[...] = _dot_ones_rhs(lft, tri_l)
    fwd_cols = lax.broadcasted_iota(jnp.int32, (CHUNK, LANES), 1) < 2 * H
    for c in range(nc):
        rows = slice(c * CHUNK, (c + 1) * CHUNK)
        cum = jnp.where(fwd_cols, prow_s[rows, :].T, srow_s[rows, :].T)
        wt_s[rows, :] = pltpu.roll(g_ref[rows, :], H, 1) - cum

    if has_init:
        for d in range(2):
            for h in range(H):
                ct_s[d, h] = c0_ref[d, h].T
                n_s[d, h] = n0_ref[h, d:d + 1, :]
                m_s[d, h] = jnp.full((1, LANES), m0_ref[pl.program_id(0), d, h], F32)
    else:
        m_s[...] = jnp.zeros_like(m_s)

    def chain(c, h, backward, use_state, update_state):
        static = isinstance(c, int)
        r0 = c * CHUNK if static else pl.multiple_of(c * CHUNK, CHUNK)
        rows = pl.ds(r0, CHUNK)
        d = 1 if backward else 0
        hc = slice(h * DH, (h + 1) * DH)
        gf = (3 if backward else 1) * H + h
        qc = qk_ref[rows, hc]
        kc = qk_ref[rows, W + h * DH:W + (h + 1) * DH] * (MLSTM_DH ** -0.5)
        vt = vt_ref[c, hc, :]
        grow = pl.ds(r0 + gf, 1)
        brow = (srow_s if backward else prow_s)[grow, :]
        g_tot = prow_s[grow, :] + srow_s[grow, :] - lft_s[grow, :]
        wb = jnp.broadcast_to(wt_s[rows, gf:gf + 1], (CHUNK, CHUNK))
        m_prev = m_s[d, h]

        dt = jnp.where(lower if backward else upper, brow + wb, -jnp.inf)
        inter = brow + m_prev
        m_t = jnp.maximum(inter, jnp.max(dt, axis=0, keepdims=True))
        p = jnp.exp(dt - m_t)
        nt_dims = (((1,), (1,)), ((), ()))
        if use_state:
            n_prev = n_s[d, h]
            lhs = jnp.concatenate([kc, jnp.broadcast_to(n_prev.astype(BF16), (16, DH)),
                                   ct_s[d, h].astype(BF16)], axis=0)
            r = lax.dot_general(lhs, qc, nt_dims, preferred_element_type=F32)
            qk_t, nq, num_c = r[:CHUNK], r[CHUNK:CHUNK + 1], r[CHUNK + 16:]
        else:
            qk_t = lax.dot_general(kc, qc, nt_dims, preferred_element_type=F32)
        st = qk_t * p
        num = _dot(vt, st.astype(BF16))
        den = jnp.sum(st, axis=0, keepdims=True)
        if use_state:
            a = jnp.exp(inter - m_t)
            num = a * num_c + num
            den = a * nq + den
        hval = num * (1.0 / jnp.maximum(jnp.abs(den), jnp.exp(-m_t)))
        (hb_s if backward else hf_s)[c, h] = hval

        if update_state:
            m_new = jnp.maximum(g_tot + m_prev, jnp.max(wb, axis=0, keepdims=True) + g_tot)
            wk = jnp.exp(wb + (g_tot - m_new))
            kw = kc.astype(F32) * jnp.concatenate([wk, wk], axis=1)
            kv = _dot(vt, kw.astype(BF16))
            nsum = jnp.sum(kw, axis=0, keepdims=True)
            if use_state:
                a_c = jnp.exp(g_tot + m_prev - m_new)
                a_c2 = jnp.concatenate([a_c, a_c], axis=1)
                ct_s[d, h] = a_c2 * ct_s[d, h] + kv
                n_s[d, h] = a_c2 * n_prev + nsum
            else:
                ct_s[d, h] = kv
                n_s[d, h] = nsum
            m_s[d, h] = m_new

    def step(j, use_state, update_state):
        for h in range(H):
            chain(j, h, False, use_state, update_state)
            chain(nc - 1 - j, h, True, use_state, update_state)

    first = 0
    if not has_init:
        step(0, False, True)
        first = 1
    last = nc if emit_state else nc - 1
    if last - first <= 2:
        for j in range(first, last):
            step(j, True, True)
    else:
        def body(j, carry):
            step(j, True, True)
            return carry
        lax.fori_loop(first, last, body, 0)
    if not emit_state:
        step(nc - 1, True, False)

    def finalize(c):
        static = isinstance(c, int)
        rows = pl.ds(c * CHUNK if static else pl.multiple_of(c * CHUNK, CHUNK), CHUNK)
        for h in range(H):
            hc = slice(h * DH, (h + 1) * DH)
            ht = hf_s[c, h] + hb_s[c, h]
            mu = jnp.mean(ht, axis=0, keepdims=True)
            cen = ht - mu
            var = jnp.mean(cen * cen, axis=0, keepdims=True)
            hn = (cen * lax.rsqrt(var + LN_EPS)).T
            hm_ref[rows, hc] = (_sigmoid(o_ref[rows, hc]) * hn * ng_ref[:, hc]).astype(BF16)

    if nc <= 2:
        for c in range(nc):
            finalize(c)
    else:
        def fbody(c, carry):
            finalize(c)
            return carry
        lax.fori_loop(0, nc, fbody, 0)

    if emit_state:
        for d in range(2):
            for h in range(H):
                cn_ref[d, h] = ct_s[d, h].T
                nn_ref[h, d:d + 1, :] = n_s[d, h]
                mn_ref[h, d:d + 1, :] = m_s[d, h]


def _mlstm(qk, vt, o, gates, norm_g, batch, seq_len, init_state):
    n = batch * seq_len
    w = MLSTM_W
    H, DH = MLSTM_H, MLSTM_DH
    nc = seq_len // CHUNK
    has_init = init_state is not None
    emit_state = not has_init
    in_specs = [pl.BlockSpec((seq_len, 2 * w), lambda b: (b, 0)),
                pl.BlockSpec((nc, w, CHUNK), lambda b: (b, 0, 0)),
                pl.BlockSpec((seq_len, w), lambda b: (b, 0)),
                pl.BlockSpec((seq_len, LANES), lambda b: (b, 0)),
                pl.BlockSpec((1, w), lambda b: (0, 0))]
    args = [qk, vt, o, gates, norm_g]
    if has_init:
        c0, n0, m0 = init_state
        in_specs += [pl.BlockSpec((None, 2, H, DH, DH), lambda b: (b, 0, 0, 0, 0)),
                     pl.BlockSpec((None, H, 2, DH), lambda b: (b, 0, 0, 0)),
                     pl.BlockSpec(memory_space=pltpu.SMEM)]
        args += [c0, n0, m0]
    out_shape = [jax.ShapeDtypeStruct((n, w), BF16)]
    out_specs = [pl.BlockSpec((seq_len, w), lambda b: (b, 0))]
    if emit_state:
        out_shape += [jax.ShapeDtypeStruct((batch, 2, H, DH, DH), F32),
                      jax.ShapeDtypeStruct((batch, H, 2, DH), F32),
                      jax.ShapeDtypeStruct((batch, H, 2, LANES), F32)]
        out_specs += [pl.BlockSpec((None, 2, H, DH, DH), lambda b: (b, 0, 0, 0, 0)),
                      pl.BlockSpec((None, H, 2, DH), lambda b: (b, 0, 0, 0)),
                      pl.BlockSpec((None, H, 2, LANES), lambda b: (b, 0, 0, 0))]
    scratch = [pltpu.VMEM((seq_len, CHUNK), F32),
               pltpu.VMEM((seq_len, CHUNK), F32),
               pltpu.VMEM((seq_len, CHUNK), F32),
               pltpu.VMEM((seq_len, LANES), F32),
               pltpu.VMEM((nc, H, DH, CHUNK), F32),
               pltpu.VMEM((nc, H, DH, CHUNK), F32),
               pltpu.VMEM((2, H, DH, DH), F32),
               pltpu.VMEM((2, H, 1, DH), F32),
               pltpu.VMEM((2, H, 1, LANES), F32)]
    return pl.pallas_call(
        functools.partial(_mlstm_kernel, seq_len=seq_len, has_init=has_init, emit_state=emit_state),
        out_shape=tuple(out_shape),
        grid=(batch,),
        in_specs=in_specs,
        out_specs=tuple(out_specs),
        scratch_shapes=scratch,
        compiler_params=_cparams(1),
        name="mlstm_init" if has_init else "mlstm_zero",
    )(*args)


def _dft_tables(seq_len):
    gw = FOURIER_GW
    kc = np.arange(gw)
    ang_c = 2.0 * np.pi * ((kc[:, None] * kc[None, :]) % gw) / gw
    w_chan = np.concatenate([np.cos(ang_c), np.sin(ang_c)], axis=1)
    kt = np.arange(seq_len)
    ang_t = 2.0 * np.pi * ((kt[:, None] * kt[None, :]) % seq_len) / seq_len
    scale = 1.0 / np.sqrt(seq_len * gw)
    return (jnp.asarray(w_chan, dtype=F32).astype(BF16),
            jnp.asarray(np.cos(ang_t) * scale, dtype=F32).astype(BF16),
            jnp.asarray(-np.sin(ang_t) * scale, dtype=F32).astype(BF16))


def _fourier_kernel(x_ref, wc_ref, ct_ref, st_ref, out_ref):
    gw = FOURIER_GW
    for g in range(FOURIER_G):
        x = x_ref[:, g * gw:(g + 1) * gw].astype(BF16)
        a = _dot(x, wc_ref[...])
        a_c = a[:, :gw].astype(BF16)
        a_s = a[:, gw:].astype(BF16)
        y = _dot(ct_ref[...], a_c) + _dot(st_ref[...], a_s)
        out_ref[:, g * gw:(g + 1) * gw] = y.astype(BF16)


def _fourier(xf, batch, seq_len):
    w_chan, c_t, s_t = _dft_tables(seq_len)
    n = batch * seq_len
    return pl.pallas_call(
        _fourier_kernel,
        out_shape=jax.ShapeDtypeStruct((n, FOURIER_W), BF16),
        grid=(batch,),
        in_specs=[pl.BlockSpec((seq_len, FOURIER_W), lambda b: (b, 0)),
                  _resident(w_chan.shape), _resident(c_t.shape), _resident(s_t.shape)],
        out_specs=pl.BlockSpec((seq_len, FOURIER_W), lambda b: (b, 0)),
        compiler_params=_cparams(1),
        name="fourier",
    )(xf, w_chan, c_t, s_t)


def _merge_kernel(x_ref, sh_ref, sc_ref, gt_ref, hm_ref, hf_ref, wl_ref, bl_ref, cw_ref, cb_ref,
                  wm_ref, wf_ref, wc_ref, wo_ref, lg_ref, lb_ref, out_ref, *, row_len):
    d = D_MODEL
    x = x_ref[...]
    tm = x.shape[0]
    u = (x * (1.0 + sc_ref[...]) + sh_ref[...]).astype(BF16)

    def proj(j):
        return _dot(u, wl_ref[:, j * d:(j + 1) * d]) + bl_ref[:, j * d:(j + 1) * d]

    uc = proj(1) * proj(2)
    pos = lax.broadcasted_iota(jnp.int32, (tm, 1), 0) % row_len
    prev = jnp.where(pos == 0, 0.0, pltpu.roll(uc, 1, 0))
    nxt = jnp.where(pos == row_len - 1, 0.0, pltpu.roll(uc, tm - 1, 0))
    yc = prev * cw_ref[0:1, :] + uc * cw_ref[1:2, :] + nxt * cw_ref[2:3, :] + cb_ref[...]
    h_c = (proj(0) * yc).astype(BF16)
    merged = (_sigmoid(proj(3)) * _dot(hm_ref[...], wm_ref[...])
              + _sigmoid(proj(4)) * _dot(hf_ref[...], wf_ref[...])
              + _sigmoid(proj(5)) * _dot(h_c, wc_ref[...]))
    y = _dot(merged.astype(BF16), wo_ref[...])
    out_ref[...] = _layer_norm(ALPHA * x + gt_ref[...] * y, lg_ref[...], lb_ref[...])


def _merge(x2d, mods, seq_len, row_len, h_m, h_f, w_l, b_l, conv_w, conv_b, w_m, w_f, w_c, w_o, ln_g, ln_b):
    n = x2d.shape[0]
    d = D_MODEL
    tm = 256
    per_mod = seq_len // tm
    mrow = (lambda i: i // per_mod) if mods.shape[0] > 1 else (lambda i: 0)
    tile = lambda i: (i, 0)
    return pl.pallas_call(
        functools.partial(_merge_kernel, row_len=row_len),
        out_shape=jax.ShapeDtypeStruct((n, d), F32),
        grid=(n // tm,),
        in_specs=[pl.BlockSpec((tm, d), tile),
                  pl.BlockSpec((None, 1, d), lambda i: (mrow(i), 0, 0)),
                  pl.BlockSpec((None, 1, d), lambda i: (mrow(i), 0, 1)),
                  pl.BlockSpec((None, 1, d), lambda i: (mrow(i), 0, 2)),
                  pl.BlockSpec((tm, d), tile), pl.BlockSpec((tm, d), tile),
                  _resident(w_l.shape), _resident(b_l.shape),
                  _resident(conv_w.shape), _resident(conv_b.shape),
                  _resident(w_m.shape), _resident(w_f.shape), _resident(w_c.shape), _resident(w_o.shape),
                  _resident(ln_g.shape), _resident(ln_b.shape)],
        out_specs=pl.BlockSpec((tm, d), tile),
        compiler_params=_cparams(1),
        name="merge",
    )(x2d, mods, mods, mods, h_m, h_f, w_l, b_l, conv_w, conv_b, w_m, w_f, w_c, w_o, ln_g, ln_b)


def _ffn_kernel(x_ref, sh_ref, sc_ref, gt_ref, wgu_ref, wd_ref, lg_ref, lb_ref, out_ref):
    x = x_ref[...]
    u = (x * (1.0 + sc_ref[...]) + sh_ref[...]).astype(BF16)
    a = _dot(u, wgu_ref[:, :D_FF])
    b = _dot(u, wgu_ref[:, D_FF:])
    hidden = (a * _sigmoid(a) * b).astype(BF16)
    y = _dot(hidden, wd_ref[...])
    out_ref[...] = _layer_norm(ALPHA * x + gt_ref[...] * y, lg_ref[...], lb_ref[...])


def _ffn(x2d, mods, seq_len, w_gu, w_d, ln_g, ln_b):
    n = x2d.shape[0]
    d = D_MODEL
    tm = 256
    per_mod = seq_len // tm
    mrow = (lambda i: i // per_mod) if mods.shape[0] > 1 else (lambda i: 0)
    tile = lambda i: (i, 0)
    return pl.pallas_call(
        _ffn_kernel,
        out_shape=jax.ShapeDtypeStruct((n, d), F32),
        grid=(n // tm,),
        in_specs=[pl.BlockSpec((tm, d), tile),
                  pl.BlockSpec((None, 1, d), lambda i: (mrow(i), 0, 3)),
                  pl.BlockSpec((None, 1, d), lambda i: (mrow(i), 0, 4)),
                  pl.BlockSpec((None, 1, d), lambda i: (mrow(i), 0, 5)),
                  _resident(w_gu.shape), _resident(w_d.shape),
                  _resident(ln_g.shape), _resident(ln_b.shape)],
        out_specs=pl.BlockSpec((tm, d), tile),
        compiler_params=_cparams(1),
        name="ffn",
    )(x2d, mods, mods, mods, w_gu, w_d, ln_g, ln_b)


def _layer_weights(l, w_in, b_in, mlstm_norm_g, conv_w, conv_b, w_br_mlstm, w_br_fourier, w_br_conv,
                   w_out, ln_g, ln_b, w_gate_up, w_down):
    wi, bi = w_in[l], b_in[l]
    w2 = 2 * MLSTM_W
    w_a = jnp.concatenate([wi[:, :w2], wi[:, w2 + MLSTM_W:OFF_GATES], wi[:, OFF_FOURIER:OFF_CONV]],
                          axis=1).astype(BF16)
    b_a = jnp.concatenate([bi[:w2], bi[w2 + MLSTM_W:OFF_GATES], bi[OFF_FOURIER:OFF_CONV]])[None, :]
    w_vt = wi[:, w2:w2 + MLSTM_W].T.astype(BF16)
    b_vt = jnp.broadcast_to(bi[w2:w2 + MLSTM_W][:, None], (MLSTM_W, CHUNK))
    pad = LANES - 4 * MLSTM_H
    w_g = jnp.pad(wi[:, OFF_GATES:OFF_FOURIER], ((0, 0), (0, pad))).astype(BF16)
    b_g = jnp.pad(bi[OFF_GATES:OFF_FOURIER], (0, pad))[None, :]
    return dict(
        w_a=w_a, b_a=b_a, w_vt=w_vt, b_vt=b_vt, w_g=w_g, b_g=b_g,
        w_l=wi[:, OFF_CONV:].astype(BF16), b_l=bi[OFF_CONV:][None, :],
        norm_g=mlstm_norm_g[l][None, :], conv_w=conv_w[l], conv_b=conv_b[l][None, :],
        w_m=w_br_mlstm[l].astype(BF16), w_f=w_br_fourier[l].astype(BF16),
        w_c=w_br_conv[l].astype(BF16), w_o=w_out[l].astype(BF16),
        ln_g1=ln_g[l, 0][None, :], ln_b1=ln_b[l, 0][None, :],
        ln_g2=ln_g[l, 1][None, :], ln_b2=ln_b[l, 1][None, :],
        w_gu=w_gate_up[l].astype(BF16), w_d=w_down[l].astype(BF16))


def _trunk_layer(x2d, mods, batch, seq_len, row_len, wts, init_state):
    qk, vt, o, xf, gates = _inproj(x2d, mods, seq_len, wts['w_a'], wts['b_a'], wts['w_vt'], wts['b_vt'],
                                   wts['w_g'], wts['b_g'])
    res = _mlstm(qk, vt, o, gates, wts['norm_g'], batch, seq_len, init_state)
    h_m = res[0]
    h_f = _fourier(xf, batch, seq_len)
    x1 = _merge(x2d, mods, seq_len, row_len, h_m, h_f, wts['w_l'], wts['b_l'], wts['conv_w'], wts['conv_b'],
                wts['w_m'], wts['w_f'], wts['w_c'], wts['w_o'], wts['ln_g1'], wts['ln_b1'])
    x2 = _ffn(x1, mods, seq_len, wts['w_gu'], wts['w_d'], wts['ln_g2'], wts['ln_b2'])
    return x2, res[1:]


def kernel(x_prompt, x_sample, state_C, state_n, state_m, c, c_ctx, w_ada, b_ada, w_in, b_in,
           mlstm_norm_g, conv_w, conv_b, w_br_mlstm, w_br_fourier, w_br_conv, w_out, ln_g, ln_b,
           w_gate_up, w_down):
    bp, tp, d = x_prompt.shape
    bs, ts, _ = x_sample.shape
    cond = jnp.concatenate([c_ctx[None, :], c, jnp.zeros((8 - 1 - bs, d), F32)], axis=0)
    mods = _ada_mods(cond, w_ada, b_ada)

    xp = x_prompt.reshape(bp * tp, d)
    xs = _add_pos(x_sample, _grid_pos_embed(ts // GRID_W, x_sample.dtype)).reshape(bs * ts, d)
    cs, ns, ms = [], [], []
    for l in range(DEPTH):
        wts = _layer_weights(l, w_in, b_in, mlstm_norm_g, conv_w, conv_b, w_br_mlstm, w_br_fourier,
                             w_br_conv, w_out, ln_g, ln_b, w_gate_up, w_down)
        mods_ctx = mods[l, 0:1].reshape(1, 1, 6 * d)
        mods_smp = mods[l, 1:1 + bs].reshape(bs, 1, 6 * d)
        xp, (c_new, n_new, m_new) = _trunk_layer(xp, mods_ctx, bp, tp, tp, wts, None)
        cs.append(c_new)
        ns.append(jnp.swapaxes(n_new, 1, 2))
        ms.append(jnp.swapaxes(m_new[..., 0], 1, 2))
        init = (state_C[:, l], jnp.swapaxes(state_n[:, l], 1, 2), state_m[:, l])
        xs, _ = _trunk_layer(xs, mods_smp, bs, ts, GRID_W, wts, init)
    return (xp.reshape(bp, tp, d), xs.reshape(bs, ts, d),
            jnp.stack(cs, axis=1), jnp.stack(ns, axis=1), jnp.stack(ms, axis=1))
```

```python
import functools

import numpy as np
import jax
import jax.numpy as jnp
from jax import lax
from jax.experimental import pallas as pl
from jax.experimental.pallas import tpu as pltpu

D_MODEL = 1024
DEPTH = 2
GRID_W = 64
MLSTM_H = 4
MLSTM_DH = 256
MLSTM_W = MLSTM_H * MLSTM_DH
CHUNK = 128
FOURIER_G = 4
FOURIER_GW = 256
FOURIER_W = FOURIER_G * FOURIER_GW
CONV_W = 1024
D_FF = -(-8 * D_MODEL // (3 * 256)) * 256
ALPHA = (2 * DEPTH) ** 0.25
LN_EPS = 1e-5
POS_BASE = 10000.0

OFF_GATES = 4 * MLSTM_W
OFF_FOURIER = OFF_GATES + 4 * MLSTM_H
OFF_CONV = OFF_FOURIER + FOURIER_W
OFF_MERGE = OFF_CONV + 3 * CONV_W
N_IN = OFF_MERGE + 3 * D_MODEL

LANES = 128
WIN_PHASE = OFF_FOURIER % LANES
N_IN_PAD = -(-N_IN // LANES) * LANES
VMEM_LIMIT = 56 * 1024 * 1024

BF16 = jnp.bfloat16
F32 = jnp.float32


def _cparams(n_axes):
    return pltpu.CompilerParams(dimension_semantics=("arbitrary",) * n_axes,
                                vmem_limit_bytes=VMEM_LIMIT)


def _resident(shape):
    zeros = (0,) * len(shape)
    return pl.BlockSpec(shape, lambda *_: zeros, pipeline_mode=pl.Buffered(1))


def _sigmoid(x):
    return 1.0 / (1.0 + jnp.exp(-x))


def _log_sigmoid(x):
    return jnp.minimum(x, 0.0) - jnp.log1p(jnp.exp(-jnp.abs(x)))


def _layer_norm(x, g, b):
    mu = jnp.mean(x, axis=-1, keepdims=True)
    xc = x - mu
    var = jnp.mean(xc * xc, axis=-1, keepdims=True)
    return xc * lax.rsqrt(var + LN_EPS) * g + b


def _dot(a, b):
    return jnp.dot(a, b, preferred_element_type=F32)


def _ada_kernel(cond_ref, w_ref, b_ref, out_ref):
    cond = cond_ref[...]
    act = (cond * _sigmoid(cond)).astype(BF16)
    out_ref[...] = _dot(act, w_ref[...].astype(BF16)) + b_ref[...]


def _ada_mods(cond, w_ada, b_ada):
    tn = 1536
    nb = 6 * D_MODEL // tn
    return pl.pallas_call(
        _ada_kernel,
        out_shape=jax.ShapeDtypeStruct((DEPTH, 8, 6 * D_MODEL), F32),
        grid=(DEPTH, nb),
        in_specs=[pl.BlockSpec((8, D_MODEL), lambda l, j: (0, 0)),
                  pl.BlockSpec((None, D_MODEL, tn), lambda l, j: (l, 0, j)),
                  pl.BlockSpec((None, 1, tn), lambda l, j: (l, 0, j))],
        out_specs=pl.BlockSpec((None, 8, tn), lambda l, j: (l, 0, j)),
        compiler_params=_cparams(2),
        name="ada_mods",
    )(cond, w_ada, b_ada.reshape(DEPTH, 1, 6 * D_MODEL))


def _add_pos_kernel(x_ref, pos_ref, out_ref):
    out_ref[...] = x_ref[...] + pos_ref[...]


def _add_pos(x, pos):
    b, t, d = x.shape
    return pl.pallas_call(
        _add_pos_kernel,
        out_shape=jax.ShapeDtypeStruct(x.shape, x.dtype),
        grid=(b,),
        in_specs=[pl.BlockSpec((None, t, d), lambda i: (i, 0, 0)),
                  pl.BlockSpec((t, d), lambda i: (0, 0))],
        out_specs=pl.BlockSpec((None, t, d), lambda i: (i, 0, 0)),
        compiler_params=_cparams(1),
        name="add_pos",
    )(x, pos)


def _grid_pos_embed(rows, dtype):
    quarter = D_MODEL // 4
    freqs = POS_BASE ** (-jnp.arange(quarter, dtype=F32) / quarter)
    r = jnp.repeat(jnp.arange(rows, dtype=F32), GRID_W)[:, None] * freqs
    col = jnp.tile(jnp.arange(GRID_W, dtype=F32), rows)[:, None] * freqs
    return jnp.concatenate([jnp.sin(r), jnp.cos(r), jnp.sin(col), jnp.cos(col)], axis=-1).astype(dtype)


def _inproj_kernel(x_ref, sh_ref, sc_ref, wqk_ref, wv_ref, wo_ref, wf_ref, wft_ref,
                   bqk_ref, bv_ref, bo_ref, bf_ref, bft_ref, qk_ref, vt_ref, o_ref, gf_ref):
    u = (x_ref[...] * (1.0 + sc_ref[...]) + sh_ref[...]).astype(BF16)
    d = D_MODEL
    for j in range(2):
        qk_ref[:, j * d:(j + 1) * d] = (
            _dot(u, wqk_ref[:, j * d:(j + 1) * d]) + bqk_ref[:, j * d:(j + 1) * d]).astype(BF16)
    v_t = (_dot(u, wv_ref[...]) + bv_ref[...]).T
    for c in range(vt_ref.shape[0]):
        vt_ref[c] = v_t[:, c * CHUNK:(c + 1) * CHUNK].astype(BF16)
    o_ref[...] = _dot(u, wo_ref[...]) + bo_ref[...]
    gf_ref[:, :d] = _dot(u, wf_ref[...]) + bf_ref[...]
    gf_ref[:, d:] = _dot(u, wft_ref[...]) + bft_ref[...]


def _layer_window(layer, rows, width, k):
    return pl.BlockSpec((None, rows, width), lambda *_: (layer, 0, k), pipeline_mode=pl.Buffered(1))


def _inproj(x2d, mods, seq_len, layer, w_in_bf, b_in_p):
    n = x2d.shape[0]
    tm = 512
    per_mod = max(seq_len // tm, 1) if mods.shape[0] > 1 else n
    mrow = (lambda i: i // per_mod) if mods.shape[0] > 1 else (lambda i: 0)
    d = D_MODEL
    wspec = functools.partial(_layer_window, layer)
    wins = [(2 * d, 0), (d, 2), (d, 3), (d, OFF_GATES // d), (LANES, (OFF_GATES + d) // LANES)]
    return pl.pallas_call(
        _inproj_kernel,
        out_shape=(jax.ShapeDtypeStruct((n, 2 * d), BF16),
                   jax.ShapeDtypeStruct((n // CHUNK, d, CHUNK), BF16),
                   jax.ShapeDtypeStruct((n, d), F32),
                   jax.ShapeDtypeStruct((n, d + LANES), F32)),
        grid=(n // tm,),
        in_specs=[pl.BlockSpec((tm, d), lambda i: (i, 0)),
                  pl.BlockSpec((None, 1, d), lambda i: (mrow(i), 0, 0)),
                  pl.BlockSpec((None, 1, d), lambda i: (mrow(i), 0, 1))]
                 + [wspec(d, w, k) for w, k in wins] + [wspec(1, w, k) for w, k in wins],
        out_specs=(pl.BlockSpec((tm, 2 * d), lambda i: (i, 0)),
                   pl.BlockSpec((tm // CHUNK, d, CHUNK), lambda i: (i, 0, 0)),
                   pl.BlockSpec((tm, d), lambda i: (i, 0)),
                   pl.BlockSpec((tm, d + LANES), lambda i: (i, 0))),
        compiler_params=_cparams(1),
        name="inproj",
    )(x2d, mods, mods, *([w_in_bf] * 5), *([b_in_p] * 5))


def _dot_ones_rhs(x, ones_mat):
    x0 = x.astype(BF16)
    r1 = x - x0.astype(F32)
    x1 = r1.astype(BF16)
    x2 = (r1 - x1.astype(F32)).astype(BF16)
    return _dot(x0, ones_mat) + _dot(x1, ones_mat) + _dot(x2, ones_mat)


def _mlstm_kernel(*refs, seq_len, layer, has_init, emit_state, n_prev):
    it = iter(refs)
    qk_ref, vt_ref, o_ref, g_ref, ng_ref = (next(it) for _ in range(5))
    if has_init:
        c0_ref, n0_ref, m0_ref = (next(it) for _ in range(3))
    for _ in range(n_prev):
        next(it)
    hm_ref = next(it)
    if emit_state:
        cn_ref, nn_ref, mn_ref = (next(it) for _ in range(3))
    lft_s, prow_s, srow_s, wt_s, hf_s, hb_s, ct_s, n_s, m_s = it

    nc = seq_len // CHUNK
    H, DH, W = MLSTM_H, MLSTM_DH, MLSTM_W
    ri = lax.broadcasted_iota(jnp.int32, (CHUNK, CHUNK), 0)
    ci = lax.broadcasted_iota(jnp.int32, (CHUNK, CHUNK), 1)
    lower = ri >= ci
    upper = ri <= ci
    tri_l = jnp.where(lower, 1.0, 0.0).astype(BF16)
    tri_u = jnp.where(upper, 1.0, 0.0).astype(BF16)

    for c in range(nc):
        rows = slice(c * CHUNK, (c + 1) * CHUNK)
        lft_s[rows, :] = _log_sigmoid(g_ref[rows, :]).T
    lft = lft_s[...]
    prow_s[...] = _dot_ones_rhs(lft, tri_u)
    srow_s[...] = _dot_ones_rhs(lft, tri_l)
    fwd_cols = lax.broadcasted_iota(jnp.int32, (CHUNK, LANES), 1) < 2 * H
    for c in range(nc):
        rows = slice(c * CHUNK, (c + 1) * CHUNK)
        cum = jnp.where(fwd_cols, prow_s[rows, :].T, srow_s[rows, :].T)
        wt_s[rows, :] = pltpu.roll(g_ref[rows, :], H, 1) - cum

    if has_init:
        for d in range(2):
            for h in range(H):
                ct_s[d, h] = c0_ref[d, h].T
                n_s[d, h] = n0_ref[d, h:h + 1, :]
                m_s[d, h] = jnp.full((1, LANES), m0_ref[pl.program_id(0), layer, d, h], F32)
    else:
        m_s[...] = jnp.zeros_like(m_s)

    def chain(c, h, backward, use_state, update_state):
        static = isinstance(c, int)
        r0 = c * CHUNK if static else pl.multiple_of(c * CHUNK, CHUNK)
        rows = pl.ds(r0, CHUNK)
        d = 1 if backward else 0
        hc = slice(h * DH, (h + 1) * DH)
        gf = (3 if backward else 1) * H + h
        qc = qk_ref[rows, hc]
        kc = qk_ref[rows, W + h * DH:W + (h + 1) * DH] * (MLSTM_DH ** -0.5)
        vt = vt_ref[c, hc, :]
        grow = pl.ds(r0 + gf, 1)
        brow = (srow_s if backward else prow_s)[grow, :]
        g_tot = prow_s[grow, :] + srow_s[grow, :] - lft_s[grow, :]
        wb = jnp.broadcast_to(wt_s[rows, gf:gf + 1], (CHUNK, CHUNK))
        m_prev = m_s[d, h]

        dt = jnp.where(lower if backward else upper, brow + wb, -jnp.inf)
        inter = brow + m_prev
        m_t = jnp.maximum(inter, jnp.max(dt, axis=0, keepdims=True))
        p = jnp.exp(dt - m_t)
        nt_dims = (((1,), (1,)), ((), ()))
        if use_state:
            n_prev = n_s[d, h]
            lhs = jnp.concatenate([kc, jnp.broadcast_to(n_prev.astype(BF16), (16, DH)),
                                   ct_s[d, h].astype(BF16)], axis=0)
            r = lax.dot_general(lhs, qc, nt_dims, preferred_element_type=F32)
            qk_t, nq, num_c = r[:CHUNK], r[CHUNK:CHUNK + 1], r[CHUNK + 16:]
        else:
            qk_t = lax.dot_general(kc, qc, nt_dims, preferred_element_type=F32)
        st = qk_t * p
        num = _dot(vt, st.astype(BF16))
        den = jnp.sum(st, axis=0, keepdims=True)
        if use_state:
            a = jnp.exp(inter - m_t)
            num = a * num_c + num
            den = a * nq + den
        hval = num * (1.0 / jnp.maximum(jnp.abs(den), jnp.exp(-m_t)))
        (hb_s if backward else hf_s)[c, h] = hval

        if update_state:
            m_new = jnp.maximum(g_tot + m_prev, jnp.max(wb, axis=0, keepdims=True) + g_tot)
            wk = jnp.exp(wb + (g_tot - m_new))
            kw = kc.astype(F32) * jnp.concatenate([wk, wk], axis=1)
            kv = _dot(vt, kw.astype(BF16))
            nsum = jnp.sum(kw, axis=0, keepdims=True)
            if use_state:
                a_c = jnp.exp(g_tot + m_prev - m_new)
                a_c2 = jnp.concatenate([a_c, a_c], axis=1)
                ct_s[d, h] = a_c2 * ct_s[d, h] + kv
                n_s[d, h] = a_c2 * n_prev + nsum
            else:
                ct_s[d, h] = kv
                n_s[d, h] = nsum
            m_s[d, h] = m_new

    def step(j, use_state, update_state):
        for h in range(H):
            chain(j, h, False, use_state, update_state)
            chain(nc - 1 - j, h, True, use_state, update_state)

    first = 0
    if not has_init:
        step(0, False, True)
        first = 1
    last = nc if emit_state else nc - 1
    if last - first <= 2:
        for j in range(first, last):
            step(j, True, True)
    else:
        def body(j, carry):
            step(j, True, True)
            return carry
        lax.fori_loop(first, last, body, 0)
    if not emit_state:
        step(nc - 1, True, False)

    def finalize(c):
        static = isinstance(c, int)
        rows = pl.ds(c * CHUNK if static else pl.multiple_of(c * CHUNK, CHUNK), CHUNK)
        for h in range(H):
            hc = slice(h * DH, (h + 1) * DH)
            ht = hf_s[c, h] + hb_s[c, h]
            mu = jnp.mean(ht, axis=0, keepdims=True)
            cen = ht - mu
            var = jnp.mean(cen * cen, axis=0, keepdims=True)
            hn = (cen * lax.rsqrt(var + LN_EPS)).T
            hm_ref[rows, hc] = (_sigmoid(o_ref[rows, hc]) * hn * ng_ref[:, hc]).astype(BF16)

    if nc <= 2:
        for c in range(nc):
            finalize(c)
    else:
        def fbody(c, carry):
            finalize(c)
            return carry
        lax.fori_loop(0, nc, fbody, 0)

    if emit_state:
        for d in range(2):
            for h in range(H):
                cn_ref[d, h] = ct_s[d, h].T
                nn_ref[d, h:h + 1, :] = n_s[d, h]
                mn_ref[d, h:h + 1, :] = m_s[d, h]


def _mlstm(qk, vt, o, gates, norm_g, batch, seq_len, layer, init_state, prev_state):
    n = batch * seq_len
    w = MLSTM_W
    H, DH = MLSTM_H, MLSTM_DH
    nc = seq_len // CHUNK
    has_init = init_state is not None
    emit_state = not has_init
    in_specs = [pl.BlockSpec((seq_len, 2 * w), lambda b: (b, 0)),
                pl.BlockSpec((nc, w, CHUNK), lambda b: (b, 0, 0)),
                pl.BlockSpec((seq_len, w), lambda b: (b, 0)),
                pl.BlockSpec((seq_len, LANES), lambda b: (b, 0)),
                pl.BlockSpec((None, 1, w), lambda b: (layer, 0, 0))]
    args = [qk, vt, o, gates, norm_g]
    state_specs = [pl.BlockSpec((None, None, 2, H, DH, DH), lambda b: (b, layer, 0, 0, 0, 0)),
                   pl.BlockSpec((None, None, 2, H, DH), lambda b: (b, layer, 0, 0, 0))]
    aliases = {}
    if has_init:
        in_specs += state_specs + [pl.BlockSpec(memory_space=pltpu.SMEM)]
        args += list(init_state)
    out_shape = [jax.ShapeDtypeStruct((n, w), BF16)]
    out_specs = [pl.BlockSpec((seq_len, w), lambda b: (b, 0))]
    if emit_state:
        out_shape += [jax.ShapeDtypeStruct((batch, DEPTH, 2, H, DH, DH), F32),
                      jax.ShapeDtypeStruct((batch, DEPTH, 2, H, DH), F32),
                      jax.ShapeDtypeStruct((batch, DEPTH, 2, H, LANES), F32)]
        out_specs += state_specs + [pl.BlockSpec((None, None, 2, H, LANES), lambda b: (b, layer, 0, 0, 0))]
        if prev_state is not None:
            aliases = {len(args) + k: 1 + k for k in range(3)}
            in_specs += [pl.BlockSpec(memory_space=pl.ANY)] * 3
            args += list(prev_state)
    scratch = [pltpu.VMEM((seq_len, CHUNK), F32),
               pltpu.VMEM((seq_len, CHUNK), F32),
               pltpu.VMEM((seq_len, CHUNK), F32),
               pltpu.VMEM((seq_len, LANES), F32),
               pltpu.VMEM((nc, H, DH, CHUNK), F32),
               pltpu.VMEM((nc, H, DH, CHUNK), F32),
               pltpu.VMEM((2, H, DH, DH), F32),
               pltpu.VMEM((2, H, 1, DH), F32),
               pltpu.VMEM((2, H, 1, LANES), F32)]
    return pl.pallas_call(
        functools.partial(_mlstm_kernel, seq_len=seq_len, layer=layer, has_init=has_init,
                          emit_state=emit_state, n_prev=len(aliases)),
        out_shape=tuple(out_shape),
        grid=(batch,),
        in_specs=in_specs,
        out_specs=tuple(out_specs),
        scratch_shapes=scratch,
        input_output_aliases=aliases,
        compiler_params=_cparams(1),
        name="mlstm_init" if has_init else "mlstm_zero",
    )(*args)


def _dft_tables(seq_len):
    gw = FOURIER_GW
    kc = np.arange(gw)
    ang_c = 2.0 * np.pi * ((kc[:, None] * kc[None, :]) % gw) / gw
    w_chan = np.concatenate([np.cos(ang_c), np.sin(ang_c)], axis=1)
    w_chan = np.pad(w_chan, ((WIN_PHASE, LANES - WIN_PHASE), (0, 0)))
    kt = np.arange(seq_len)
    ang_t = 2.0 * np.pi * ((kt[:, None] * kt[None, :]) % seq_len) / seq_len
    scale = 1.0 / np.sqrt(seq_len * gw)
    return (jnp.asarray(w_chan, dtype=F32).astype(BF16),
            jnp.asarray(np.cos(ang_t) * scale, dtype=F32).astype(BF16),
            jnp.asarray(-np.sin(ang_t) * scale, dtype=F32).astype(BF16))


def _fourier_kernel(x_ref, wc_ref, ct_ref, st_ref, out_ref):
    gw = FOURIER_GW
    for g in range(FOURIER_G):
        x = x_ref[:, g * gw:(g + 1) * gw + LANES].astype(BF16)
        a = _dot(x, wc_ref[...])
        a_c = a[:, :gw].astype(BF16)
        a_s = a[:, gw:].astype(BF16)
        y = _dot(ct_ref[...], a_c) + _dot(st_ref[...], a_s)
        out_ref[:, g * gw:(g + 1) * gw] = y.astype(BF16)


def _fourier(xf, batch, seq_len):
    w_chan, c_t, s_t = _dft_tables(seq_len)
    n = batch * seq_len
    return pl.pallas_call(
        _fourier_kernel,
        out_shape=jax.ShapeDtypeStruct((n, FOURIER_W), BF16),
        grid=(batch,),
        in_specs=[pl.BlockSpec((seq_len, FOURIER_W + LANES), lambda b: (b, 0)),
                  _resident(w_chan.shape), _resident(c_t.shape), _resident(s_t.shape)],
        out_specs=pl.BlockSpec((seq_len, FOURIER_W), lambda b: (b, 0)),
        compiler_params=_cparams(1),
        name="fourier",
    )(xf, w_chan, c_t, s_t)


def _merge_kernel(*refs, row_len):
    (x_ref, sh_ref, sc_ref, gt_ref, hm_ref, hf_ref), rest = refs[:6], refs[6:]
    w_refs, b_refs = rest[:7], rest[7:14]
    cw_ref, cb_ref, wm_ref, wf_ref, wc_ref, wo_ref, lg_ref, lb_ref, out_ref = rest[14:]
    x = x_ref[...]
    tm = x.shape[0]
    u = (x * (1.0 + sc_ref[...]) + sh_ref[...]).astype(BF16)

    wins = [_dot(u, w[...]) + b[...] for w, b in zip(w_refs, b_refs)]

    def seg(k):
        return jnp.concatenate([wins[k], wins[k + 1][:, :LANES]], axis=1)

    uc = seg(1) * seg(2)
    pos = lax.broadcasted_iota(jnp.int32, (tm, 1), 0) % row_len
    prev = jnp.where(pos == 0, 0.0, pltpu.roll(uc, 1, 0))
    nxt = jnp.where(pos == row_len - 1, 0.0, pltpu.roll(uc, tm - 1, 0))
    yc = prev * cw_ref[0:1, :] + uc * cw_ref[1:2, :] + nxt * cw_ref[2:3, :] + cb_ref[...]
    h_c = (seg(0) * yc).astype(BF16)
    merged = (_sigmoid(seg(3)) * _dot(hm_ref[...], wm_ref[...])
              + _sigmoid(seg(4)) * _dot(hf_ref[...], wf_ref[...])
              + _sigmoid(seg(5)) * _dot(h_c, wc_ref[...]))
    y = _dot(merged.astype(BF16), wo_ref[...])
    out_ref[...] = _layer_norm(ALPHA * x + gt_ref[...] * y, lg_ref[...], lb_ref[...])


def _merge(x2d, mods, seq_len, row_len, layer, h_m, h_f, w_in_bf, b_in_p, ext):
    n = x2d.shape[0]
    d = D_MODEL
    dw = d + LANES
    tm = 256
    per_mod = seq_len // tm
    mrow = (lambda i: i // per_mod) if mods.shape[0] > 1 else (lambda i: 0)
    tile = lambda i: (i, 0)
    wspec = functools.partial(_layer_window, layer)
    first = (OFF_CONV - WIN_PHASE) // d
    wins = [(d, first + k) for k in range(6)] + [(LANES, (OFF_CONV - WIN_PHASE + 6 * d) // LANES)]
    return pl.pallas_call(
        functools.partial(_merge_kernel, row_len=row_len),
        out_shape=jax.ShapeDtypeStruct((n, d), F32),
        grid=(n // tm,),
        in_specs=[pl.BlockSpec((tm, d), tile),
                  pl.BlockSpec((None, 1, d), lambda i: (mrow(i), 0, 0)),
                  pl.BlockSpec((None, 1, d), lambda i: (mrow(i), 0, 1)),
                  pl.BlockSpec((None, 1, d), lambda i: (mrow(i), 0, 2)),
                  pl.BlockSpec((tm, d), tile), pl.BlockSpec((tm, d), tile)]
                 + [wspec(d, w, k) for w, k in wins] + [wspec(1, w, k) for w, k in wins]
                 + [wspec(3, dw, 0), wspec(1, dw, 0), wspec(d, dw, 0), wspec(d, dw, 0), wspec(dw, dw, 0),
                    wspec(dw, d, 0),
                    pl.BlockSpec((None, 1, d), lambda i: (2 * layer, 0, 0)),
                    pl.BlockSpec((None, 1, d), lambda i: (2 * layer, 0, 0))],
        out_specs=pl.BlockSpec((tm, d), tile),
        compiler_params=_cparams(1),
        name="merge",
    )(x2d, mods, mods, mods, h_m, h_f, *([w_in_bf] * 7), *([b_in_p] * 7),
      ext['conv_w'], ext['conv_b'], ext['w_m'], ext['w_f'], ext['w_c'], ext['w_o'], ext['ln_g'], ext['ln_b'])


def _ffn_kernel(x_ref, sh_ref, sc_ref, gt_ref, wgu_ref, wd_ref, lg_ref, lb_ref, out_ref):
    x = x_ref[...]
    u = (x * (1.0 + sc_ref[...]) + sh_ref[...]).astype(BF16)
    a = _dot(u, wgu_ref[:, :D_FF])
    b = _dot(u, wgu_ref[:, D_FF:])
    hidden = (a * _sigmoid(a) * b).astype(BF16)
    y = _dot(hidden, wd_ref[...])
    out_ref[...] = _layer_norm(ALPHA * x + gt_ref[...] * y, lg_ref[...], lb_ref[...])


def _ffn(x2d, mods, seq_len, layer, w_gu_bf, w_d_bf, ln_g, ln_b):
    n = x2d.shape[0]
    d = D_MODEL
    tm = 256
    per_mod = seq_len // tm
    mrow = (lambda i: i // per_mod) if mods.shape[0] > 1 else (lambda i: 0)
    tile = lambda i: (i, 0)
    wspec = functools.partial(_layer_window, layer)
    return pl.pallas_call(
        _ffn_kernel,
        out_shape=jax.ShapeDtypeStruct((n, d), F32),
        grid=(n // tm,),
        in_specs=[pl.BlockSpec((tm, d), tile),
                  pl.BlockSpec((None, 1, d), lambda i: (mrow(i), 0, 3)),
                  pl.BlockSpec((None, 1, d), lambda i: (mrow(i), 0, 4)),
                  pl.BlockSpec((None, 1, d), lambda i: (mrow(i), 0, 5)),
                  wspec(d, 2 * D_FF, 0), wspec(D_FF, d, 0),
                  pl.BlockSpec((None, 1, d), lambda i: (2 * layer + 1, 0, 0)),
                  pl.BlockSpec((None, 1, d), lambda i: (2 * layer + 1, 0, 0))],
        out_specs=pl.BlockSpec((tm, d), tile),
        compiler_params=_cparams(1),
        name="ffn",
    )(x2d, mods, mods, mods, w_gu_bf, w_d_bf, ln_g, ln_b)


def _prepared_weights(w_in, b_in, mlstm_norm_g, conv_w, conv_b, w_br_mlstm, w_br_fourier, w_br_conv,
                      w_out, ln_g, ln_b, w_gate_up, w_down):
    lo, hi = WIN_PHASE, LANES - WIN_PHASE
    cols = ((0, 0), (0, 0), (lo, hi))
    rows = ((0, 0), (lo, hi), (0, 0))
    ext = dict(
        conv_w=jnp.pad(conv_w, cols), conv_b=jnp.pad(conv_b[:, None, :], cols),
        w_m=jnp.pad(w_br_mlstm, cols).astype(BF16), w_f=jnp.pad(w_br_fourier, cols).astype(BF16),
        w_c=jnp.pad(w_br_conv, ((0, 0), (lo, hi), (lo, hi))).astype(BF16),
        w_o=jnp.pad(w_out, rows).astype(BF16),
        ln_g=ln_g.reshape(2 * DEPTH, 1, D_MODEL), ln_b=ln_b.reshape(2 * DEPTH, 1, D_MODEL))
    pad_in = ((0, 0), (0, 0), (0, N_IN_PAD - N_IN))
    return dict(
        w_in=jnp.pad(w_in, pad_in).astype(BF16), b_in=jnp.pad(b_in[:, None, :], pad_in),
        norm_g=mlstm_norm_g[:, None, :], ext=ext,
        w_gu=w_gate_up.astype(BF16), w_d=w_down.astype(BF16))


def _trunk_layer(x2d, mods, batch, seq_len, row_len, layer, wts, init_state, prev_state):
    qk, vt, o, gf = _inproj(x2d, mods, seq_len, layer, wts['w_in'], wts['b_in'])
    res = _mlstm(qk, vt, o, gf, wts['norm_g'], batch, seq_len, layer, init_state, prev_state)
    h_f = _fourier(gf, batch, seq_len)
    x1 = _merge(x2d, mods, seq_len, row_len, layer, res[0], h_f, wts['w_in'], wts['b_in'], wts['ext'])
    x2 = _ffn(x1, mods, seq_len, layer, wts['w_gu'], wts['w_d'], wts['ext']['ln_g'], wts['ext']['ln_b'])
    return x2, res[1:]


def kernel(x_prompt, x_sample, state_C, state_n, state_m, c, c_ctx, w_ada, b_ada, w_in, b_in,
           mlstm_norm_g, conv_w, conv_b, w_br_mlstm, w_br_fourier, w_br_conv, w_out, ln_g, ln_b,
           w_gate_up, w_down):
    bp, tp, d = x_prompt.shape
    bs, ts, _ = x_sample.shape
    cond = jnp.concatenate([c_ctx[None, :], c, jnp.zeros((8 - 1 - bs, d), F32)], axis=0)
    mods = _ada_mods(cond, w_ada, b_ada)
    wts = _prepared_weights(w_in, b_in, mlstm_norm_g, conv_w, conv_b, w_br_mlstm, w_br_fourier,
                            w_br_conv, w_out, ln_g, ln_b, w_gate_up, w_down)

    xp = x_prompt.reshape(bp * tp, d)
    xs = _add_pos(x_sample, _grid_pos_embed(ts // GRID_W, x_sample.dtype)).reshape(bs * ts, d)
    new_state = None
    for l in range(DEPTH):
        mods_ctx = mods[l, 0:1].reshape(1, 1, 6 * d)
        mods_smp = mods[l, 1:1 + bs].reshape(bs, 1, 6 * d)
        xp, new_state = _trunk_layer(xp, mods_ctx, bp, tp, tp, l, wts, None, new_state)
        xs, _ = _trunk_layer(xs, mods_smp, bs, ts, GRID_W, l, wts, (state_C, state_n, state_m), None)
    c_new, n_new, m_new = new_state
    return (xp.reshape(bp, tp, d), xs.reshape(bs, ts, d), c_new, n_new, m_new[..., 0])
```

```python
import functools

import numpy as np
import jax
import jax.numpy as jnp
from jax import lax
from jax.experimental import pallas as pl
from jax.experimental.pallas import tpu as pltpu

D_MODEL = 1024
DEPTH = 2
GRID_W = 64
MLSTM_H = 4
MLSTM_DH = 256
MLSTM_W = MLSTM_H * MLSTM_DH
CHUNK = 128
FOURIER_G = 4
FOURIER_GW = 256
FOURIER_W = FOURIER_G * FOURIER_GW
CONV_W = 1024
D_FF = -(-8 * D_MODEL // (3 * 256)) * 256
ALPHA = (2 * DEPTH) ** 0.25
LN_EPS = 1e-5
POS_BASE = 10000.0

OFF_GATES = 4 * MLSTM_W
OFF_FOURIER = OFF_GATES + 4 * MLSTM_H
OFF_CONV = OFF_FOURIER + FOURIER_W
OFF_MERGE = OFF_CONV + 3 * CONV_W
N_IN = OFF_MERGE + 3 * D_MODEL

LANES = 128
WIN_PHASE = OFF_FOURIER % LANES
VMEM_LIMIT = 56 * 1024 * 1024

BF16 = jnp.bfloat16
F32 = jnp.float32


def _cparams(n_axes):
    return pltpu.CompilerParams(dimension_semantics=("arbitrary",) * n_axes,
                                vmem_limit_bytes=VMEM_LIMIT)


def _resident(shape):
    zeros = (0,) * len(shape)
    return pl.BlockSpec(shape, lambda *_: zeros, pipeline_mode=pl.Buffered(1))


def _sigmoid(x):
    return 1.0 / (1.0 + jnp.exp(-x))


def _log_sigmoid(x):
    return jnp.minimum(x, 0.0) - jnp.log1p(jnp.exp(-jnp.abs(x)))


def _layer_norm(x, g, b):
    mu = jnp.mean(x, axis=-1, keepdims=True)
    xc = x - mu
    var = jnp.mean(xc * xc, axis=-1, keepdims=True)
    return xc * lax.rsqrt(var + LN_EPS) * g + b


def _dot(a, b):
    return jnp.dot(a, b, preferred_element_type=F32)


def _ada_kernel(cond_ref, w_ref, b_ref, out_ref):
    cond = cond_ref[...]
    act = (cond * _sigmoid(cond)).astype(BF16)
    out_ref[...] = _dot(act, w_ref[...].astype(BF16)) + b_ref[...]


def _ada_mods(cond, w_ada, b_ada):
    tn = 1536
    nb = 6 * D_MODEL // tn
    return pl.pallas_call(
        _ada_kernel,
        out_shape=jax.ShapeDtypeStruct((DEPTH, 8, 6 * D_MODEL), F32),
        grid=(DEPTH, nb),
        in_specs=[pl.BlockSpec((8, D_MODEL), lambda l, j: (0, 0)),
                  pl.BlockSpec((None, D_MODEL, tn), lambda l, j: (l, 0, j)),
                  pl.BlockSpec((None, 1, tn), lambda l, j: (l, 0, j))],
        out_specs=pl.BlockSpec((None, 8, tn), lambda l, j: (l, 0, j)),
        compiler_params=_cparams(2),
        name="ada_mods",
    )(cond, w_ada, b_ada.reshape(DEPTH, 1, 6 * D_MODEL))


def _add_pos_kernel(x_ref, pos_ref, out_ref):
    out_ref[...] = x_ref[...] + pos_ref[...]


def _add_pos(x, pos):
    b, t, d = x.shape
    return pl.pallas_call(
        _add_pos_kernel,
        out_shape=jax.ShapeDtypeStruct(x.shape, x.dtype),
        grid=(b,),
        in_specs=[pl.BlockSpec((None, t, d), lambda i: (i, 0, 0)),
                  pl.BlockSpec((t, d), lambda i: (0, 0))],
        out_specs=pl.BlockSpec((None, t, d), lambda i: (i, 0, 0)),
        compiler_params=_cparams(1),
        name="add_pos",
    )(x, pos)


def _grid_pos_embed(rows, dtype):
    quarter = D_MODEL // 4
    freqs = POS_BASE ** (-jnp.arange(quarter, dtype=F32) / quarter)
    r = jnp.repeat(jnp.arange(rows, dtype=F32), GRID_W)[:, None] * freqs
    col = jnp.tile(jnp.arange(GRID_W, dtype=F32), rows)[:, None] * freqs
    return jnp.concatenate([jnp.sin(r), jnp.cos(r), jnp.sin(col), jnp.cos(col)], axis=-1).astype(dtype)


def _shifted(a, b):
    return jnp.concatenate([a[:, WIN_PHASE:], b[:, :WIN_PHASE]], axis=1)


def _inproj_kernel(x_ref, sh_ref, sc_ref, wqk_ref, wv_ref, wo_ref, wf_ref, wft_ref,
                   bqk_ref, bv_ref, bo_ref, bf_ref, bft_ref, qk_ref, vt_ref, o_ref, f_ref, g_ref,
                   wfa_s, bfa_s):
    @pl.when(pl.program_id(0) == 0)
    def _():
        wfa_s[...] = _shifted(wf_ref[...], wft_ref[...])
        bfa_s[...] = _shifted(bf_ref[...], bft_ref[...])

    u = (x_ref[...] * (1.0 + sc_ref[...]) + sh_ref[...]).astype(BF16)
    d = D_MODEL
    for j in range(2):
        qk_ref[:, j * d:(j + 1) * d] = (
            _dot(u, wqk_ref[:, j * d:(j + 1) * d]) + bqk_ref[:, j * d:(j + 1) * d]).astype(BF16)
    v_t = (_dot(u, wv_ref[...]) + bv_ref[...]).T
    for c in range(vt_ref.shape[0]):
        vt_ref[c] = v_t[:, c * CHUNK:(c + 1) * CHUNK].astype(BF16)
    o_ref[...] = _dot(u, wo_ref[...]) + bo_ref[...]
    f_ref[...] = _dot(u, wfa_s[...]) + bfa_s[...]
    g_ref[...] = _dot(u, wf_ref[:, :LANES]) + bf_ref[:, :LANES]


def _layer_window(layer, rows, width, k):
    return pl.BlockSpec((None, rows, width), lambda *_: (layer, 0, k), pipeline_mode=pl.Buffered(1))


def _inproj(x2d, mods, seq_len, layer, w_in_bf, b_in_p):
    n = x2d.shape[0]
    tm = 512
    per_mod = max(seq_len // tm, 1) if mods.shape[0] > 1 else n
    mrow = (lambda i: i // per_mod) if mods.shape[0] > 1 else (lambda i: 0)
    d = D_MODEL
    wspec = functools.partial(_layer_window, layer)
    wins = [(2 * d, 0), (d, 2), (d, 3), (d, OFF_GATES // d), (LANES, (OFF_GATES + d) // LANES)]
    return pl.pallas_call(
        _inproj_kernel,
        out_shape=(jax.ShapeDtypeStruct((n, 2 * d), BF16),
                   jax.ShapeDtypeStruct((n // CHUNK, d, CHUNK), BF16),
                   jax.ShapeDtypeStruct((n, d), F32),
                   jax.ShapeDtypeStruct((n, d), F32),
                   jax.ShapeDtypeStruct((n, LANES), F32)),
        grid=(n // tm,),
        in_specs=[pl.BlockSpec((tm, d), lambda i: (i, 0)),
                  pl.BlockSpec((None, 1, d), lambda i: (mrow(i), 0, 0)),
                  pl.BlockSpec((None, 1, d), lambda i: (mrow(i), 0, 1))]
                 + [wspec(d, w, k) for w, k in wins] + [wspec(1, w, k) for w, k in wins],
        out_specs=(pl.BlockSpec((tm, 2 * d), lambda i: (i, 0)),
                   pl.BlockSpec((tm // CHUNK, d, CHUNK), lambda i: (i, 0, 0)),
                   pl.BlockSpec((tm, d), lambda i: (i, 0)),
                   pl.BlockSpec((tm, d), lambda i: (i, 0)),
                   pl.BlockSpec((tm, LANES), lambda i: (i, 0))),
        scratch_shapes=[pltpu.VMEM((d, d), BF16), pltpu.VMEM((1, d), F32)],
        compiler_params=_cparams(1),
        name="inproj",
    )(x2d, mods, mods, *([w_in_bf] * 5), *([b_in_p] * 5))


def _dot_ones_rhs(x, ones_mat):
    x0 = x.astype(BF16)
    r1 = x - x0.astype(F32)
    x1 = r1.astype(BF16)
    x2 = (r1 - x1.astype(F32)).astype(BF16)
    return _dot(x0, ones_mat) + _dot(x1, ones_mat) + _dot(x2, ones_mat)


def _mlstm_kernel(*refs, seq_len, layer, has_init, emit_state, n_prev):
    it = iter(refs)
    qk_ref, vt_ref, o_ref, g_ref, ng_ref = (next(it) for _ in range(5))
    if has_init:
        c0_ref, n0_ref, m0_ref = (next(it) for _ in range(3))
    for _ in range(n_prev):
        next(it)
    hm_ref = next(it)
    if emit_state:
        cn_ref, nn_ref, mn_ref = (next(it) for _ in range(3))
    lft_s, prow_s, srow_s, wt_s, hf_s, hb_s, ct_s, n_s, m_s = it

    nc = seq_len // CHUNK
    H, DH, W = MLSTM_H, MLSTM_DH, MLSTM_W
    ri = lax.broadcasted_iota(jnp.int32, (CHUNK, CHUNK), 0)
    ci = lax.broadcasted_iota(jnp.int32, (CHUNK, CHUNK), 1)
    lower = ri >= ci
    upper = ri <= ci
    tri_l = jnp.where(lower, 1.0, 0.0).astype(BF16)
    tri_u = jnp.where(upper, 1.0, 0.0).astype(BF16)

    for c in range(nc):
        rows = slice(c * CHUNK, (c + 1) * CHUNK)
        lft_s[rows, :] = _log_sigmoid(g_ref[rows, :]).T
    lft = lft_s[...]
    prow_s[...] = _dot_ones_rhs(lft, tri_u)
    srow_s[...] = _dot_ones_rhs(lft, tri_l)
    fwd_cols = lax.broadcasted_iota(jnp.int32, (CHUNK, LANES), 1) < 2 * H
    for c in range(nc):
        rows = slice(c * CHUNK, (c + 1) * CHUNK)
        cum = jnp.where(fwd_cols, prow_s[rows, :].T, srow_s[rows, :].T)
        wt_s[rows, :] = pltpu.roll(g_ref[rows, :], H, 1) - cum

    if has_init:
        for d in range(2):
            for h in range(H):
                ct_s[d, h] = c0_ref[d, h].T
                n_s[d, h] = n0_ref[d, h:h + 1, :]
                m_s[d, h] = jnp.full((1, LANES), m0_ref[pl.program_id(0), layer, d, h], F32)
    else:
        m_s[...] = jnp.zeros_like(m_s)

    def chain(c, h, backward, use_state, update_state):
        static = isinstance(c, int)
        r0 = c * CHUNK if static else pl.multiple_of(c * CHUNK, CHUNK)
        rows = pl.ds(r0, CHUNK)
        d = 1 if backward else 0
        hc = slice(h * DH, (h + 1) * DH)
        gf = (3 if backward else 1) * H + h
        qc = qk_ref[rows, hc]
        kc = qk_ref[rows, W + h * DH:W + (h + 1) * DH] * (MLSTM_DH ** -0.5)
        vt = vt_ref[c, hc, :]
        grow = pl.ds(r0 + gf, 1)
        brow = (srow_s if backward else prow_s)[grow, :]
        g_tot = prow_s[grow, :] + srow_s[grow, :] - lft_s[grow, :]
        wb = jnp.broadcast_to(wt_s[rows, gf:gf + 1], (CHUNK, CHUNK))
        m_prev = m_s[d, h]

        dt = jnp.where(lower if backward else upper, brow + wb, -jnp.inf)
        inter = brow + m_prev
        m_t = jnp.maximum(inter, jnp.max(dt, axis=0, keepdims=True))
        p = jnp.exp(dt - m_t)
        nt_dims = (((1,), (1,)), ((), ()))
        if use_state:
            n_prev = n_s[d, h]
            lhs = jnp.concatenate([kc, jnp.broadcast_to(n_prev.astype(BF16), (16, DH)),
                                   ct_s[d, h].astype(BF16)], axis=0)
            r = lax.dot_general(lhs, qc, nt_dims, preferred_element_type=F32)
            qk_t, nq, num_c = r[:CHUNK], r[CHUNK:CHUNK + 1], r[CHUNK + 16:]
        else:
            qk_t = lax.dot_general(kc, qc, nt_dims, preferred_element_type=F32)
        st = qk_t * p
        num = _dot(vt, st.astype(BF16))
        den = jnp.sum(st, axis=0, keepdims=True)
        if use_state:
            a = jnp.exp(inter - m_t)
            num = a * num_c + num
            den = a * nq + den
        hval = num * (1.0 / jnp.maximum(jnp.abs(den), jnp.exp(-m_t)))
        (hb_s if backward else hf_s)[c, h] = hval

        if update_state:
            m_new = jnp.maximum(g_tot + m_prev, jnp.max(wb, axis=0, keepdims=True) + g_tot)
            wk = jnp.exp(wb + (g_tot - m_new))
            kw = kc.astype(F32) * jnp.concatenate([wk, wk], axis=1)
            kv = _dot(vt, kw.astype(BF16))
            nsum = jnp.sum(kw, axis=0, keepdims=True)
            if use_state:
                a_c = jnp.exp(g_tot + m_prev - m_new)
                a_c2 = jnp.concatenate([a_c, a_c], axis=1)
                ct_s[d, h] = a_c2 * ct_s[d, h] + kv
                n_s[d, h] = a_c2 * n_prev + nsum
            else:
                ct_s[d, h] = kv
                n_s[d, h] = nsum
            m_s[d, h] = m_new

    def step(j, use_state, update_state):
        for h in range(H):
            chain(j, h, False, use_state, update_state)
            chain(nc - 1 - j, h, True, use_state, update_state)

    first = 0
    if not has_init:
        step(0, False, True)
        first = 1
    last = nc if emit_state else nc - 1
    if last - first <= 2:
        for j in range(first, last):
            step(j, True, True)
    else:
        def body(j, carry):
            step(j, True, True)
            return carry
        lax.fori_loop(first, last, body, 0)
    if not emit_state:
        step(nc - 1, True, False)

    def finalize(c):
        static = isinstance(c, int)
        rows = pl.ds(c * CHUNK if static else pl.multiple_of(c * CHUNK, CHUNK), CHUNK)
        for h in range(H):
            hc = slice(h * DH, (h + 1) * DH)
            ht = hf_s[c, h] + hb_s[c, h]
            mu = jnp.mean(ht, axis=0, keepdims=True)
            cen = ht - mu
            var = jnp.mean(cen * cen, axis=0, keepdims=True)
            hn = (cen * lax.rsqrt(var + LN_EPS)).T
            hm_ref[rows, hc] = (_sigmoid(o_ref[rows, hc]) * hn * ng_ref[:, hc]).astype(BF16)

    if nc <= 2:
        for c in range(nc):
            finalize(c)
    else:
        def fbody(c, carry):
            finalize(c)
            return carry
        lax.fori_loop(0, nc, fbody, 0)

    if emit_state:
        for d in range(2):
            for h in range(H):
                cn_ref[d, h] = ct_s[d, h].T
                nn_ref[d, h:h + 1, :] = n_s[d, h]
                mn_ref[d, h:h + 1, :] = m_s[d, h]


def _mlstm(qk, vt, o, gates, norm_g, batch, seq_len, layer, init_state, prev_state):
    n = batch * seq_len
    w = MLSTM_W
    H, DH = MLSTM_H, MLSTM_DH
    nc = seq_len // CHUNK
    has_init = init_state is not None
    emit_state = not has_init
    in_specs = [pl.BlockSpec((seq_len, 2 * w), lambda b: (b, 0)),
                pl.BlockSpec((nc, w, CHUNK), lambda b: (b, 0, 0)),
                pl.BlockSpec((seq_len, w), lambda b: (b, 0)),
                pl.BlockSpec((seq_len, LANES), lambda b: (b, 0)),
                pl.BlockSpec((None, 1, w), lambda b: (layer, 0, 0))]
    args = [qk, vt, o, gates, norm_g]
    state_specs = [pl.BlockSpec((None, None, 2, H, DH, DH), lambda b: (b, layer, 0, 0, 0, 0)),
                   pl.BlockSpec((None, None, 2, H, DH), lambda b: (b, layer, 0, 0, 0))]
    aliases = {}
    if has_init:
        in_specs += state_specs + [pl.BlockSpec(memory_space=pltpu.SMEM)]
        args += list(init_state)
    out_shape = [jax.ShapeDtypeStruct((n, w), BF16)]
    out_specs = [pl.BlockSpec((seq_len, w), lambda b: (b, 0))]
    if emit_state:
        out_shape += [jax.ShapeDtypeStruct((batch, DEPTH, 2, H, DH, DH), F32),
                      jax.ShapeDtypeStruct((batch, DEPTH, 2, H, DH), F32),
                      jax.ShapeDtypeStruct((batch, DEPTH, 2, H, LANES), F32)]
        out_specs += state_specs + [pl.BlockSpec((None, None, 2, H, LANES), lambda b: (b, layer, 0, 0, 0))]
        if prev_state is not None:
            aliases = {len(args) + k: 1 + k for k in range(3)}
            in_specs += [pl.BlockSpec(memory_space=pl.ANY)] * 3
            args += list(prev_state)
    scratch = [pltpu.VMEM((seq_len, CHUNK), F32),
               pltpu.VMEM((seq_len, CHUNK), F32),
               pltpu.VMEM((seq_len, CHUNK), F32),
               pltpu.VMEM((seq_len, LANES), F32),
               pltpu.VMEM((nc, H, DH, CHUNK), F32),
               pltpu.VMEM((nc, H, DH, CHUNK), F32),
               pltpu.VMEM((2, H, DH, DH), F32),
               pltpu.VMEM((2, H, 1, DH), F32),
               pltpu.VMEM((2, H, 1, LANES), F32)]
    return pl.pallas_call(
        functools.partial(_mlstm_kernel, seq_len=seq_len, layer=layer, has_init=has_init,
                          emit_state=emit_state, n_prev=len(aliases)),
        out_shape=tuple(out_shape),
        grid=(batch,),
        in_specs=in_specs,
        out_specs=tuple(out_specs),
        scratch_shapes=scratch,
        input_output_aliases=aliases,
        compiler_params=_cparams(1),
        name="mlstm_init" if has_init else "mlstm_zero",
    )(*args)


def _dft_tables(seq_len):
    gw = FOURIER_GW
    kc = np.arange(gw)
    ang_c = 2.0 * np.pi * ((kc[:, None] * kc[None, :]) % gw) / gw
    w_chan = np.concatenate([np.cos(ang_c), np.sin(ang_c)], axis=1)
    kt = np.arange(seq_len)
    ang_t = 2.0 * np.pi * ((kt[:, None] * kt[None, :]) % seq_len) / seq_len
    scale = 1.0 / np.sqrt(seq_len * gw)
    return (jnp.asarray(w_chan, dtype=F32).astype(BF16),
            jnp.asarray(np.cos(ang_t) * scale, dtype=F32).astype(BF16),
            jnp.asarray(-np.sin(ang_t) * scale, dtype=F32).astype(BF16))


def _fourier_kernel(x_ref, wc_ref, ct_ref, st_ref, out_ref):
    gw = FOURIER_GW
    for g in range(FOURIER_G):
        x = x_ref[:, g * gw:(g + 1) * gw].astype(BF16)
        a = _dot(x, wc_ref[...])
        a_c = a[:, :gw].astype(BF16)
        a_s = a[:, gw:].astype(BF16)
        y = _dot(ct_ref[...], a_c) + _dot(st_ref[...], a_s)
        out_ref[:, g * gw:(g + 1) * gw] = y.astype(BF16)


def _fourier(xf, batch, seq_len):
    w_chan, c_t, s_t = _dft_tables(seq_len)
    n = batch * seq_len
    return pl.pallas_call(
        _fourier_kernel,
        out_shape=jax.ShapeDtypeStruct((n, FOURIER_W), BF16),
        grid=(batch,),
        in_specs=[pl.BlockSpec((seq_len, FOURIER_W), lambda b: (b, 0)),
                  _resident(w_chan.shape), _resident(c_t.shape), _resident(s_t.shape)],
        out_specs=pl.BlockSpec((seq_len, FOURIER_W), lambda b: (b, 0)),
        compiler_params=_cparams(1),
        name="fourier",
    )(xf, w_chan, c_t, s_t)


def _merge_kernel(*refs, row_len):
    (x_ref, sh_ref, sc_ref, gt_ref, hm_ref, hf_ref), rest = refs[:6], refs[6:]
    w_refs, b_refs = rest[:7], rest[7:14]
    cw_ref, cb_ref, wm_ref, wf_ref, wc_ref, wo_ref, lg_ref, lb_ref, out_ref, wal_s, bal_s = rest[14:]

    @pl.when(pl.program_id(0) == 0)
    def _():
        for k in range(6):
            wal_s[k] = _shifted(w_refs[k][...], w_refs[k + 1][:, :LANES])
            bal_s[k] = _shifted(b_refs[k][...], b_refs[k + 1][:, :LANES])

    x = x_ref[...]
    tm = x.shape[0]
    u = (x * (1.0 + sc_ref[...]) + sh_ref[...]).astype(BF16)

    def seg(k):
        return _dot(u, wal_s[k]) + bal_s[k]

    uc = seg(1) * seg(2)
    pos = lax.broadcasted_iota(jnp.int32, (tm, 1), 0) % row_len
    prev = jnp.where(pos == 0, 0.0, pltpu.roll(uc, 1, 0))
    nxt = jnp.where(pos == row_len - 1, 0.0, pltpu.roll(uc, tm - 1, 0))
    yc = prev * cw_ref[0:1, :] + uc * cw_ref[1:2, :] + nxt * cw_ref[2:3, :] + cb_ref[...]
    h_c = (seg(0) * yc).astype(BF16)
    merged = (_sigmoid(seg(3)) * _dot(hm_ref[...], wm_ref[...])
              + _sigmoid(seg(4)) * _dot(hf_ref[...], wf_ref[...])
              + _sigmoid(seg(5)) * _dot(h_c, wc_ref[...]))
    y = _dot(merged.astype(BF16), wo_ref[...])
    out_ref[...] = _layer_norm(ALPHA * x + gt_ref[...] * y, lg_ref[...], lb_ref[...])


def _merge(x2d, mods, seq_len, row_len, layer, h_m, h_f, wts):
    n = x2d.shape[0]
    d = D_MODEL
    tm = 256
    per_mod = seq_len // tm
    mrow = (lambda i: i // per_mod) if mods.shape[0] > 1 else (lambda i: 0)
    tile = lambda i: (i, 0)
    wspec = functools.partial(_layer_window, layer)
    first = (OFF_CONV - WIN_PHASE) // d
    wins = [wspec(d, d, first + k) for k in range(6)] + [wspec(d, LANES, 0)]
    bwins = [wspec(1, d, first + k) for k in range(6)] + [wspec(1, LANES, 0)]
    return pl.pallas_call(
        functools.partial(_merge_kernel, row_len=row_len),
        out_shape=jax.ShapeDtypeStruct((n, d), F32),
        grid=(n // tm,),
        in_specs=[pl.BlockSpec((tm, d), tile),
                  pl.BlockSpec((None, 1, d), lambda i: (mrow(i), 0, 0)),
                  pl.BlockSpec((None, 1, d), lambda i: (mrow(i), 0, 1)),
                  pl.BlockSpec((None, 1, d), lambda i: (mrow(i), 0, 2)),
                  pl.BlockSpec((tm, d), tile), pl.BlockSpec((tm, d), tile)]
                 + wins + bwins
                 + [wspec(3, d, 0), wspec(1, d, 0), wspec(d, d, 0), wspec(d, d, 0), wspec(d, d, 0),
                    wspec(d, d, 0),
                    pl.BlockSpec((None, 1, d), lambda i: (2 * layer, 0, 0)),
                    pl.BlockSpec((None, 1, d), lambda i: (2 * layer, 0, 0))],
        out_specs=pl.BlockSpec((tm, d), tile),
        scratch_shapes=[pltpu.VMEM((6, d, d), BF16), pltpu.VMEM((6, 1, d), F32)],
        compiler_params=_cparams(1),
        name="merge",
    )(x2d, mods, mods, mods, h_m, h_f, *([wts['w_in']] * 6), wts['w_tail'], *([wts['b_in']] * 6),
      wts['b_tail'], wts['conv_w'], wts['conv_b'], wts['w_m'], wts['w_f'], wts['w_c'], wts['w_o'],
      wts['ln_g'], wts['ln_b'])


def _ffn_kernel(x_ref, sh_ref, sc_ref, gt_ref, wgu_ref, wd_ref, lg_ref, lb_ref, out_ref):
    x = x_ref[...]
    u = (x * (1.0 + sc_ref[...]) + sh_ref[...]).astype(BF16)
    a = _dot(u, wgu_ref[:, :D_FF])
    b = _dot(u, wgu_ref[:, D_FF:])
    hidden = (a * _sigmoid(a) * b).astype(BF16)
    y = _dot(hidden, wd_ref[...])
    out_ref[...] = _layer_norm(ALPHA * x + gt_ref[...] * y, lg_ref[...], lb_ref[...])


def _ffn(x2d, mods, seq_len, layer, w_gu_bf, w_d_bf, ln_g, ln_b):
    n = x2d.shape[0]
    d = D_MODEL
    tm = 256
    per_mod = seq_len // tm
    mrow = (lambda i: i // per_mod) if mods.shape[0] > 1 else (lambda i: 0)
    tile = lambda i: (i, 0)
    wspec = functools.partial(_layer_window, layer)
    return pl.pallas_call(
        _ffn_kernel,
        out_shape=jax.ShapeDtypeStruct((n, d), F32),
        grid=(n // tm,),
        in_specs=[pl.BlockSpec((tm, d), tile),
                  pl.BlockSpec((None, 1, d), lambda i: (mrow(i), 0, 3)),
                  pl.BlockSpec((None, 1, d), lambda i: (mrow(i), 0, 4)),
                  pl.BlockSpec((None, 1, d), lambda i: (mrow(i), 0, 5)),
                  wspec(d, 2 * D_FF, 0), wspec(D_FF, d, 0),
                  pl.BlockSpec((None, 1, d), lambda i: (2 * layer + 1, 0, 0)),
                  pl.BlockSpec((None, 1, d), lambda i: (2 * layer + 1, 0, 0))],
        out_specs=pl.BlockSpec((tm, d), tile),
        compiler_params=_cparams(1),
        name="ffn",
    )(x2d, mods, mods, mods, w_gu_bf, w_d_bf, ln_g, ln_b)


def _prepared_weights(w_in, b_in, mlstm_norm_g, conv_w, conv_b, w_br_mlstm, w_br_fourier, w_br_conv,
                      w_out, ln_g, ln_b, w_gate_up, w_down):
    tail = N_IN - WIN_PHASE
    pad_tail = ((0, 0), (0, 0), (0, LANES - WIN_PHASE))
    b_in3 = b_in[:, None, :]
    return dict(
        w_in=w_in.astype(BF16), b_in=b_in3,
        w_tail=jnp.pad(w_in[:, :, tail:], pad_tail).astype(BF16), b_tail=jnp.pad(b_in3[:, :, tail:], pad_tail),
        norm_g=mlstm_norm_g[:, None, :], conv_w=conv_w, conv_b=conv_b[:, None, :],
        w_m=w_br_mlstm.astype(BF16), w_f=w_br_fourier.astype(BF16), w_c=w_br_conv.astype(BF16),
        w_o=w_out.astype(BF16),
        ln_g=ln_g.reshape(2 * DEPTH, 1, D_MODEL), ln_b=ln_b.reshape(2 * DEPTH, 1, D_MODEL),
        w_gu=w_gate_up.astype(BF16), w_d=w_down.astype(BF16))


def _trunk_layer(x2d, mods, batch, seq_len, row_len, layer, wts, init_state, prev_state):
    qk, vt, o, xf, gates = _inproj(x2d, mods, seq_len, layer, wts['w_in'], wts['b_in'])
    res = _mlstm(qk, vt, o, gates, wts['norm_g'], batch, seq_len, layer, init_state, prev_state)
    h_f = _fourier(xf, batch, seq_len)
    x1 = _merge(x2d, mods, seq_len, row_len, layer, res[0], h_f, wts)
    x2 = _ffn(x1, mods, seq_len, layer, wts['w_gu'], wts['w_d'], wts['ln_g'], wts['ln_b'])
    return x2, res[1:]


def kernel(x_prompt, x_sample, state_C, state_n, state_m, c, c_ctx, w_ada, b_ada, w_in, b_in,
           mlstm_norm_g, conv_w, conv_b, w_br_mlstm, w_br_fourier, w_br_conv, w_out, ln_g, ln_b,
           w_gate_up, w_down):
    bp, tp, d = x_prompt.shape
    bs, ts, _ = x_sample.shape
    cond = jnp.concatenate([c_ctx[None, :], c, jnp.zeros((8 - 1 - bs, d), F32)], axis=0)
    mods = _ada_mods(cond, w_ada, b_ada)
    wts = _prepared_weights(w_in, b_in, mlstm_norm_g, conv_w, conv_b, w_br_mlstm, w_br_fourier,
                            w_br_conv, w_out, ln_g, ln_b, w_gate_up, w_down)

    xp = x_prompt.reshape(bp * tp, d)
    xs = _add_pos(x_sample, _grid_pos_embed(ts // GRID_W, x_sample.dtype)).reshape(bs * ts, d)
    new_state = None
    for l in range(DEPTH):
        mods_ctx = mods[l, 0:1].reshape(1, 1, 6 * d)
        mods_smp = mods[l, 1:1 + bs].reshape(bs, 1, 6 * d)
        xp, new_state = _trunk_layer(xp, mods_ctx, bp, tp, tp, l, wts, None, new_state)
        xs, _ = _trunk_layer(xs, mods_smp, bs, ts, GRID_W, l, wts, (state_C, state_n, state_m), None)
    c_new, n_new, m_new = new_state
    return (xp.reshape(bp, tp, d), xs.reshape(bs, ts, d), c_new, n_new, m_new[..., 0])
```

```python
import functools

import numpy as np
import jax
import jax.numpy as jnp
from jax import lax
from jax.experimental import pallas as pl
from jax.experimental.pallas import tpu as pltpu

D_MODEL = 1024
DEPTH = 2
GRID_W = 64
MLSTM_H = 4
MLSTM_DH = 256
MLSTM_W = MLSTM_H * MLSTM_DH
CHUNK = 256
GATE_ROWS = 4 * MLSTM_H
FOURIER_G = 4
FOURIER_GW = 256
FOURIER_W = FOURIER_G * FOURIER_GW
CONV_W = 1024
D_FF = -(-8 * D_MODEL // (3 * 256)) * 256
ALPHA = (2 * DEPTH) ** 0.25
LN_EPS = 1e-5
POS_BASE = 10000.0

OFF_GATES = 4 * MLSTM_W
OFF_FOURIER = OFF_GATES + 4 * MLSTM_H
OFF_CONV = OFF_FOURIER + FOURIER_W
OFF_MERGE = OFF_CONV + 3 * CONV_W
N_IN = OFF_MERGE + 3 * D_MODEL

LANES = 128
WIN_PHASE = OFF_FOURIER % LANES
VMEM_LIMIT = 56 * 1024 * 1024

BF16 = jnp.bfloat16
F32 = jnp.float32


def _cparams(n_axes):
    return pltpu.CompilerParams(dimension_semantics=("arbitrary",) * n_axes,
                                vmem_limit_bytes=VMEM_LIMIT)


def _resident(shape):
    zeros = (0,) * len(shape)
    return pl.BlockSpec(shape, lambda *_: zeros, pipeline_mode=pl.Buffered(1))


def _sigmoid(x):
    return 1.0 / (1.0 + jnp.exp(-x))


def _log_sigmoid(x):
    return jnp.minimum(x, 0.0) - jnp.log1p(jnp.exp(-jnp.abs(x)))


def _layer_norm(x, g, b):
    mu = jnp.mean(x, axis=-1, keepdims=True)
    xc = x - mu
    var = jnp.mean(xc * xc, axis=-1, keepdims=True)
    return xc * lax.rsqrt(var + LN_EPS) * g + b


def _dot(a, b):
    return jnp.dot(a, b, preferred_element_type=F32)


def _ada_kernel(cond_ref, w_ref, b_ref, out_ref):
    cond = cond_ref[...]
    act = (cond * _sigmoid(cond)).astype(BF16)
    out_ref[...] = _dot(act, w_ref[...].astype(BF16)) + b_ref[...]


def _ada_mods(cond, w_ada, b_ada):
    tn = 1536
    nb = 6 * D_MODEL // tn
    return pl.pallas_call(
        _ada_kernel,
        out_shape=jax.ShapeDtypeStruct((DEPTH, 8, 6 * D_MODEL), F32),
        grid=(DEPTH, nb),
        in_specs=[pl.BlockSpec((8, D_MODEL), lambda l, j: (0, 0)),
                  pl.BlockSpec((None, D_MODEL, tn), lambda l, j: (l, 0, j)),
                  pl.BlockSpec((None, 1, tn), lambda l, j: (l, 0, j))],
        out_specs=pl.BlockSpec((None, 8, tn), lambda l, j: (l, 0, j)),
        compiler_params=_cparams(2),
        name="ada_mods",
    )(cond, w_ada, b_ada.reshape(DEPTH, 1, 6 * D_MODEL))


def _add_pos_kernel(x_ref, pos_ref, out_ref):
    out_ref[...] = x_ref[...] + pos_ref[...]


def _add_pos(x, pos):
    b, t, d = x.shape
    return pl.pallas_call(
        _add_pos_kernel,
        out_shape=jax.ShapeDtypeStruct(x.shape, x.dtype),
        grid=(b,),
        in_specs=[pl.BlockSpec((None, t, d), lambda i: (i, 0, 0)),
                  pl.BlockSpec((t, d), lambda i: (0, 0))],
        out_specs=pl.BlockSpec((None, t, d), lambda i: (i, 0, 0)),
        compiler_params=_cparams(1),
        name="add_pos",
    )(x, pos)


def _grid_pos_embed(rows, dtype):
    quarter = D_MODEL // 4
    freqs = POS_BASE ** (-jnp.arange(quarter, dtype=F32) / quarter)
    r = jnp.repeat(jnp.arange(rows, dtype=F32), GRID_W)[:, None] * freqs
    col = jnp.tile(jnp.arange(GRID_W, dtype=F32), rows)[:, None] * freqs
    return jnp.concatenate([jnp.sin(r), jnp.cos(r), jnp.sin(col), jnp.cos(col)], axis=-1).astype(dtype)


def _shifted(a, b):
    return jnp.concatenate([a[:, WIN_PHASE:], b[:, :WIN_PHASE]], axis=1)


def _inproj_kernel(x_ref, sh_ref, sc_ref, wqk_ref, wv_ref, wo_ref, wf_ref, wft_ref,
                   bqk_ref, bv_ref, bo_ref, bf_ref, bft_ref, qk_ref, vt_ref, o_ref, f_ref, g_ref,
                   wfa_s, bfa_s):
    @pl.when(pl.program_id(0) == 0)
    def _():
        wfa_s[...] = _shifted(wf_ref[...], wft_ref[...])
        bfa_s[...] = _shifted(bf_ref[...], bft_ref[...])

    u = (x_ref[...] * (1.0 + sc_ref[...]) + sh_ref[...]).astype(BF16)
    d = D_MODEL
    for j in range(2):
        qk_ref[:, j * d:(j + 1) * d] = (
            _dot(u, wqk_ref[:, j * d:(j + 1) * d]) + bqk_ref[:, j * d:(j + 1) * d]).astype(BF16)
    v_t = (_dot(u, wv_ref[...]) + bv_ref[...]).T
    for c in range(vt_ref.shape[0]):
        vt_ref[c] = v_t[:, c * CHUNK:(c + 1) * CHUNK].astype(BF16)
    o_ref[...] = _dot(u, wo_ref[...]) + bo_ref[...]
    f_ref[...] = _dot(u, wfa_s[...]) + bfa_s[...]
    g_ref[...] = _dot(u, wf_ref[:, :LANES]) + bf_ref[:, :LANES]


def _layer_window(layer, rows, width, k):
    return pl.BlockSpec((None, rows, width), lambda *_: (layer, 0, k), pipeline_mode=pl.Buffered(1))


def _inproj(x2d, mods, seq_len, layer, w_in_bf, b_in_p):
    n = x2d.shape[0]
    tm = 512
    per_mod = max(seq_len // tm, 1) if mods.shape[0] > 1 else n
    mrow = (lambda i: i // per_mod) if mods.shape[0] > 1 else (lambda i: 0)
    d = D_MODEL
    wspec = functools.partial(_layer_window, layer)
    wins = [(2 * d, 0), (d, 2), (d, 3), (d, OFF_GATES // d), (LANES, (OFF_GATES + d) // LANES)]
    return pl.pallas_call(
        _inproj_kernel,
        out_shape=(jax.ShapeDtypeStruct((n, 2 * d), BF16),
                   jax.ShapeDtypeStruct((n // CHUNK, d, CHUNK), BF16),
                   jax.ShapeDtypeStruct((n, d), F32),
                   jax.ShapeDtypeStruct((n, d), F32),
                   jax.ShapeDtypeStruct((n, LANES), F32)),
        grid=(n // tm,),
        in_specs=[pl.BlockSpec((tm, d), lambda i: (i, 0)),
                  pl.BlockSpec((None, 1, d), lambda i: (mrow(i), 0, 0)),
                  pl.BlockSpec((None, 1, d), lambda i: (mrow(i), 0, 1))]
                 + [wspec(d, w, k) for w, k in wins] + [wspec(1, w, k) for w, k in wins],
        out_specs=(pl.BlockSpec((tm, 2 * d), lambda i: (i, 0)),
                   pl.BlockSpec((tm // CHUNK, d, CHUNK), lambda i: (i, 0, 0)),
                   pl.BlockSpec((tm, d), lambda i: (i, 0)),
                   pl.BlockSpec((tm, d), lambda i: (i, 0)),
                   pl.BlockSpec((tm, LANES), lambda i: (i, 0))),
        scratch_shapes=[pltpu.VMEM((d, d), BF16), pltpu.VMEM((1, d), F32)],
        compiler_params=_cparams(1),
        name="inproj",
    )(x2d, mods, mods, *([w_in_bf] * 5), *([b_in_p] * 5))


def _dot_ones_rhs(x, ones_mat):
    x0 = x.astype(BF16)
    r1 = x - x0.astype(F32)
    x1 = r1.astype(BF16)
    x2 = (r1 - x1.astype(F32)).astype(BF16)
    return _dot(x0, ones_mat) + _dot(x1, ones_mat) + _dot(x2, ones_mat)


def _mlstm_kernel(*refs, seq_len, layer, has_init, emit_state, n_prev):
    it = iter(refs)
    qk_ref, vt_ref, o_ref, g_ref, ng_ref = (next(it) for _ in range(5))
    if has_init:
        c0_ref, n0_ref, m0_ref = (next(it) for _ in range(3))
    for _ in range(n_prev):
        next(it)
    hm_ref = next(it)
    if emit_state:
        cn_ref, nn_ref, mn_ref = (next(it) for _ in range(3))
    lft_s, prow_s, srow_s, wt_s, hf_s, hb_s, ct_s, n_s, m_s = it

    nc = seq_len // CHUNK
    H, DH, W = MLSTM_H, MLSTM_DH, MLSTM_W
    GR = GATE_ROWS
    assert CHUNK == DH
    ri = lax.broadcasted_iota(jnp.int32, (CHUNK, CHUNK), 0)
    ci = lax.broadcasted_iota(jnp.int32, (CHUNK, CHUNK), 1)
    lower = ri >= ci
    upper = ri <= ci
    tri_l = jnp.where(lower, 1.0, 0.0).astype(BF16)
    tri_u = jnp.where(upper, 1.0, 0.0).astype(BF16)

    for c in range(nc):
        rows = slice(c * CHUNK, (c + 1) * CHUNK)
        lft_s[c * GR:(c + 1) * GR, :] = _log_sigmoid(g_ref[rows, :].T[:GR, :])
    lft = lft_s[...]
    prow_s[...] = _dot_ones_rhs(lft, tri_u)
    srow_s[...] = _dot_ones_rhs(lft, tri_l)
    fwd_cols = lax.broadcasted_iota(jnp.int32, (CHUNK, LANES), 1) < 2 * H
    zpad = jnp.zeros((LANES - GR, CHUNK), F32)
    for c in range(nc):
        rows = slice(c * CHUNK, (c + 1) * CHUNK)
        slab = slice(c * GR, (c + 1) * GR)
        pcol = jnp.concatenate([prow_s[slab, :], zpad], axis=0).T
        scol = jnp.concatenate([srow_s[slab, :], zpad], axis=0).T
        wt_s[rows, :] = pltpu.roll(g_ref[rows, :], H, 1) - jnp.where(fwd_cols, pcol, scol)

    if has_init:
        for d in range(2):
            for h in range(H):
                ct_s[d, h] = c0_ref[d, h].T
                n_s[d, h] = n0_ref[d, h:h + 1, :]
                m_s[d, h] = jnp.full((1, CHUNK), m0_ref[pl.program_id(0), layer, d, h], F32)
    else:
        m_s[...] = jnp.zeros_like(m_s)
    ones_rows = jnp.ones((16, CHUNK), BF16)

    def chain(c, h, backward, use_state, update_state):
        static = isinstance(c, int)
        r0 = c * CHUNK if static else pl.multiple_of(c * CHUNK, CHUNK)
        rows = pl.ds(r0, CHUNK)
        d = 1 if backward else 0
        hc = slice(h * DH, (h + 1) * DH)
        gf = (3 if backward else 1) * H + h
        qc = qk_ref[rows, hc]
        kc = qk_ref[rows, W + h * DH:W + (h + 1) * DH] * (MLSTM_DH ** -0.5)
        vt = vt_ref[c, hc, :]
        grow = pl.ds(c * GR + gf, 1)
        brow = (srow_s if backward else prow_s)[grow, :]
        g_tot = prow_s[grow, :] + srow_s[grow, :] - lft_s[grow, :]
        wb = jnp.broadcast_to(wt_s[rows, gf:gf + 1], (CHUNK, CHUNK))
        m_prev = m_s[d, h]

        dt = jnp.where(lower if backward else upper, brow + wb, -jnp.inf)
        inter = brow + m_prev
        m_t = jnp.maximum(inter, jnp.max(dt, axis=0, keepdims=True))
        p = jnp.exp(dt - m_t)
        nt_dims = (((1,), (1,)), ((), ()))
        if use_state:
            n_prev = n_s[d, h]
            lhs = jnp.concatenate([kc, jnp.broadcast_to(n_prev.astype(BF16), (16, DH)),
                                   ct_s[d, h].astype(BF16)], axis=0)
            r = lax.dot_general(lhs, qc, nt_dims, preferred_element_type=F32)
            qk_t, nq, num_c = r[:CHUNK], r[CHUNK:CHUNK + 1], r[CHUNK + 16:]
        else:
            qk_t = lax.dot_general(kc, qc, nt_dims, preferred_element_type=F32)
        st = qk_t * p
        num = _dot(vt, st.astype(BF16))
        den = jnp.sum(st, axis=0, keepdims=True)
        if use_state:
            a = jnp.exp(inter - m_t)
            num = a * num_c + num
            den = a * nq + den
        hval = num * (1.0 / jnp.maximum(jnp.abs(den), jnp.exp(-m_t)))
        (hb_s if backward else hf_s)[c, h] = hval

        if update_state:
            m_new = jnp.maximum(g_tot + m_prev, jnp.max(wb, axis=0, keepdims=True) + g_tot)
            wk = jnp.exp(wb + (g_tot - m_new))
            kw = kc * wk.astype(BF16)
            r2 = _dot(jnp.concatenate([vt, ones_rows], axis=0), kw)
            kv, nsum = r2[:DH], r2[DH:DH + 1]
            if use_state:
                a_c = jnp.exp(g_tot + m_prev - m_new)
                ct_s[d, h] = a_c * ct_s[d, h] + kv
                n_s[d, h] = a_c * n_prev + nsum
            else:
                ct_s[d, h] = kv
                n_s[d, h] = nsum
            m_s[d, h] = m_new

    def step(j, use_state, update_state):
        for h in range(H):
            chain(j, h, False, use_state, update_state)
            chain(nc - 1 - j, h, True, use_state, update_state)

    first = 0
    if not has_init:
        step(0, False, True)
        first = 1
    last = nc if emit_state else nc - 1
    if last - first <= 2:
        for j in range(first, last):
            step(j, True, True)
    else:
        def body(j, carry):
            step(j, True, True)
            return carry
        lax.fori_loop(first, last, body, 0)
    if not emit_state:
        step(nc - 1, True, False)

    def finalize(c):
        static = isinstance(c, int)
        rows = pl.ds(c * CHUNK if static else pl.multiple_of(c * CHUNK, CHUNK), CHUNK)
        for h in range(H):
            hc = slice(h * DH, (h + 1) * DH)
            ht = hf_s[c, h] + hb_s[c, h]
            mu = jnp.mean(ht, axis=0, keepdims=True)
            cen = ht - mu
            var = jnp.mean(cen * cen, axis=0, keepdims=True)
            hn = (cen * lax.rsqrt(var + LN_EPS)).T
            hm_ref[rows, hc] = (_sigmoid(o_ref[rows, hc]) * hn * ng_ref[:, hc]).astype(BF16)

    if nc <= 2:
        for c in range(nc):
            finalize(c)
    else:
        def fbody(c, carry):
            finalize(c)
            return carry
        lax.fori_loop(0, nc, fbody, 0)

    if emit_state:
        for d in range(2):
            for h in range(H):
                cn_ref[d, h] = ct_s[d, h].T
                nn_ref[d, h:h + 1, :] = n_s[d, h]
                mn_ref[d, h:h + 1, :] = m_s[d, h][:, :LANES]


def _mlstm(qk, vt, o, gates, norm_g, batch, seq_len, layer, init_state, prev_state):
    n = batch * seq_len
    w = MLSTM_W
    H, DH = MLSTM_H, MLSTM_DH
    nc = seq_len // CHUNK
    has_init = init_state is not None
    emit_state = not has_init
    in_specs = [pl.BlockSpec((seq_len, 2 * w), lambda b: (b, 0)),
                pl.BlockSpec((nc, w, CHUNK), lambda b: (b, 0, 0)),
                pl.BlockSpec((seq_len, w), lambda b: (b, 0)),
                pl.BlockSpec((seq_len, LANES), lambda b: (b, 0)),
                pl.BlockSpec((None, 1, w), lambda b: (layer, 0, 0))]
    args = [qk, vt, o, gates, norm_g]
    state_specs = [pl.BlockSpec((None, None, 2, H, DH, DH), lambda b: (b, layer, 0, 0, 0, 0)),
                   pl.BlockSpec((None, None, 2, H, DH), lambda b: (b, layer, 0, 0, 0))]
    aliases = {}
    if has_init:
        in_specs += state_specs + [pl.BlockSpec(memory_space=pltpu.SMEM)]
        args += list(init_state)
    out_shape = [jax.ShapeDtypeStruct((n, w), BF16)]
    out_specs = [pl.BlockSpec((seq_len, w), lambda b: (b, 0))]
    if emit_state:
        out_shape += [jax.ShapeDtypeStruct((batch, DEPTH, 2, H, DH, DH), F32),
                      jax.ShapeDtypeStruct((batch, DEPTH, 2, H, DH), F32),
                      jax.ShapeDtypeStruct((batch, DEPTH, 2, H, LANES), F32)]
        out_specs += state_specs + [pl.BlockSpec((None, None, 2, H, LANES), lambda b: (b, layer, 0, 0, 0))]
        if prev_state is not None:
            aliases = {len(args) + k: 1 + k for k in range(3)}
            in_specs += [pl.BlockSpec(memory_space=pl.ANY)] * 3
            args += list(prev_state)
    scratch = [pltpu.VMEM((nc * GATE_ROWS, CHUNK), F32),
               pltpu.VMEM((nc * GATE_ROWS, CHUNK), F32),
               pltpu.VMEM((nc * GATE_ROWS, CHUNK), F32),
               pltpu.VMEM((seq_len, LANES), F32),
               pltpu.VMEM((nc, H, DH, CHUNK), F32),
               pltpu.VMEM((nc, H, DH, CHUNK), F32),
               pltpu.VMEM((2, H, DH, DH), F32),
               pltpu.VMEM((2, H, 1, DH), F32),
               pltpu.VMEM((2, H, 1, CHUNK), F32)]
    return pl.pallas_call(
        functools.partial(_mlstm_kernel, seq_len=seq_len, layer=layer, has_init=has_init,
                          emit_state=emit_state, n_prev=len(aliases)),
        out_shape=tuple(out_shape),
        grid=(batch,),
        in_specs=in_specs,
        out_specs=tuple(out_specs),
        scratch_shapes=scratch,
        input_output_aliases=aliases,
        compiler_params=_cparams(1),
        name="mlstm_init" if has_init else "mlstm_zero",
    )(*args)


def _dft_tables(seq_len):
    gw = FOURIER_GW
    kc = np.arange(gw)
    ang_c = 2.0 * np.pi * ((kc[:, None] * kc[None, :]) % gw) / gw
    w_chan = np.concatenate([np.cos(ang_c), np.sin(ang_c)], axis=1)
    kt = np.arange(seq_len)
    ang_t = 2.0 * np.pi * ((kt[:, None] * kt[None, :]) % seq_len) / seq_len
    scale = 1.0 / np.sqrt(seq_len * gw)
    return (jnp.asarray(w_chan, dtype=F32).astype(BF16),
            jnp.asarray(np.cos(ang_t) * scale, dtype=F32).astype(BF16),
            jnp.asarray(-np.sin(ang_t) * scale, dtype=F32).astype(BF16))


def _fourier_kernel(x_ref, wc_ref, ct_ref, st_ref, out_ref):
    gw = FOURIER_GW
    for g in range(FOURIER_G):
        x = x_ref[:, g * gw:(g + 1) * gw].astype(BF16)
        a = _dot(x, wc_ref[...])
        a_c = a[:, :gw].astype(BF16)
        a_s = a[:, gw:].astype(BF16)
        y = _dot(ct_ref[...], a_c) + _dot(st_ref[...], a_s)
        out_ref[:, g * gw:(g + 1) * gw] = y.astype(BF16)


def _fourier(xf, batch, seq_len):
    w_chan, c_t, s_t = _dft_tables(seq_len)
    n = batch * seq_len
    return pl.pallas_call(
        _fourier_kernel,
        out_shape=jax.ShapeDtypeStruct((n, FOURIER_W), BF16),
        grid=(batch,),
        in_specs=[pl.BlockSpec((seq_len, FOURIER_W), lambda b: (b, 0)),
                  _resident(w_chan.shape), _resident(c_t.shape), _resident(s_t.shape)],
        out_specs=pl.BlockSpec((seq_len, FOURIER_W), lambda b: (b, 0)),
        compiler_params=_cparams(1),
        name="fourier",
    )(xf, w_chan, c_t, s_t)


def _merge_kernel(*refs, row_len):
    (x_ref, sh_ref, sc_ref, gt_ref, hm_ref, hf_ref), rest = refs[:6], refs[6:]
    w_refs, b_refs = rest[:7], rest[7:14]
    cw_ref, cb_ref, wm_ref, wf_ref, wc_ref, wo_ref, lg_ref, lb_ref, out_ref, wal_s, bal_s = rest[14:]

    @pl.when(pl.program_id(0) == 0)
    def _():
        for k in range(6):
            wal_s[k] = _shifted(w_refs[k][...], w_refs[k + 1][:, :LANES])
            bal_s[k] = _shifted(b_refs[k][...], b_refs[k + 1][:, :LANES])

    x = x_ref[...]
    tm = x.shape[0]
    u = (x * (1.0 + sc_ref[...]) + sh_ref[...]).astype(BF16)

    def seg(k):
        return _dot(u, wal_s[k]) + bal_s[k]

    uc = seg(1) * seg(2)
    pos = lax.broadcasted_iota(jnp.int32, (tm, 1), 0) % row_len
    prev = jnp.where(pos == 0, 0.0, pltpu.roll(uc, 1, 0))
    nxt = jnp.where(pos == row_len - 1, 0.0, pltpu.roll(uc, tm - 1, 0))
    yc = prev * cw_ref[0:1, :] + uc * cw_ref[1:2, :] + nxt * cw_ref[2:3, :] + cb_ref[...]
    h_c = (seg(0) * yc).astype(BF16)
    merged = (_sigmoid(seg(3)) * _dot(hm_ref[...], wm_ref[...])
              + _sigmoid(seg(4)) * _dot(hf_ref[...], wf_ref[...])
              + _sigmoid(seg(5)) * _dot(h_c, wc_ref[...]))
    y = _dot(merged.astype(BF16), wo_ref[...])
    out_ref[...] = _layer_norm(ALPHA * x + gt_ref[...] * y, lg_ref[...], lb_ref[...])


def _merge(x2d, mods, seq_len, row_len, layer, h_m, h_f, wts):
    n = x2d.shape[0]
    d = D_MODEL
    tm = 256
    per_mod = seq_len // tm
    mrow = (lambda i: i // per_mod) if mods.shape[0] > 1 else (lambda i: 0)
    tile = lambda i: (i, 0)
    wspec = functools.partial(_layer_window, layer)
    first = (OFF_CONV - WIN_PHASE) // d
    wins = [wspec(d, d, first + k) for k in range(6)] + [wspec(d, LANES, 0)]
    bwins = [wspec(1, d, first + k) for k in range(6)] + [wspec(1, LANES, 0)]
    return pl.pallas_call(
        functools.partial(_merge_kernel, row_len=row_len),
        out_shape=jax.ShapeDtypeStruct((n, d), F32),
        grid=(n // tm,),
        in_specs=[pl.BlockSpec((tm, d), tile),
                  pl.BlockSpec((None, 1, d), lambda i: (mrow(i), 0, 0)),
                  pl.BlockSpec((None, 1, d), lambda i: (mrow(i), 0, 1)),
                  pl.BlockSpec((None, 1, d), lambda i: (mrow(i), 0, 2)),
                  pl.BlockSpec((tm, d), tile), pl.BlockSpec((tm, d), tile)]
                 + wins + bwins
                 + [wspec(3, d, 0), wspec(1, d, 0), wspec(d, d, 0), wspec(d, d, 0), wspec(d, d, 0),
                    wspec(d, d, 0),
                    pl.BlockSpec((None, 1, d), lambda i: (2 * layer, 0, 0)),
                    pl.BlockSpec((None, 1, d), lambda i: (2 * layer, 0, 0))],
        out_specs=pl.BlockSpec((tm, d), tile),
        scratch_shapes=[pltpu.VMEM((6, d, d), BF16), pltpu.VMEM((6, 1, d), F32)],
        compiler_params=_cparams(1),
        name="merge",
    )(x2d, mods, mods, mods, h_m, h_f, *([wts['w_in']] * 6), wts['w_tail'], *([wts['b_in']] * 6),
      wts['b_tail'], wts['conv_w'], wts['conv_b'], wts['w_m'], wts['w_f'], wts['w_c'], wts['w_o'],
      wts['ln_g'], wts['ln_b'])


def _ffn_kernel(x_ref, sh_ref, sc_ref, gt_ref, wgu_ref, wd_ref, lg_ref, lb_ref, out_ref):
    x = x_ref[...]
    u = (x * (1.0 + sc_ref[...]) + sh_ref[...]).astype(BF16)
    a = _dot(u, wgu_ref[:, :D_FF])
    b = _dot(u, wgu_ref[:, D_FF:])
    hidden = (a * _sigmoid(a) * b).astype(BF16)
    y = _dot(hidden, wd_ref[...])
    out_ref[...] = _layer_norm(ALPHA * x + gt_ref[...] * y, lg_ref[...], lb_ref[...])


def _ffn(x2d, mods, seq_len, layer, w_gu_bf, w_d_bf, ln_g, ln_b):
    n = x2d.shape[0]
    d = D_MODEL
    tm = 512
    per_mod = seq_len // tm
    mrow = (lambda i: i // per_mod) if mods.shape[0] > 1 else (lambda i: 0)
    tile = lambda i: (i, 0)
    wspec = functools.partial(_layer_window, layer)
    return pl.pallas_call(
        _ffn_kernel,
        out_shape=jax.ShapeDtypeStruct((n, d), F32),
        grid=(n // tm,),
        in_specs=[pl.BlockSpec((tm, d), tile),
                  pl.BlockSpec((None, 1, d), lambda i: (mrow(i), 0, 3)),
                  pl.BlockSpec((None, 1, d), lambda i: (mrow(i), 0, 4)),
                  pl.BlockSpec((None, 1, d), lambda i: (mrow(i), 0, 5)),
                  wspec(d, 2 * D_FF, 0), wspec(D_FF, d, 0),
                  pl.BlockSpec((None, 1, d), lambda i: (2 * layer + 1, 0, 0)),
                  pl.BlockSpec((None, 1, d), lambda i: (2 * layer + 1, 0, 0))],
        out_specs=pl.BlockSpec((tm, d), tile),
        compiler_params=_cparams(1),
        name="ffn",
    )(x2d, mods, mods, mods, w_gu_bf, w_d_bf, ln_g, ln_b)


def _prepared_weights(w_in, b_in, mlstm_norm_g, conv_w, conv_b, w_br_mlstm, w_br_fourier, w_br_conv,
                      w_out, ln_g, ln_b, w_gate_up, w_down):
    tail = N_IN - WIN_PHASE
    pad_tail = ((0, 0), (0, 0), (0, LANES - WIN_PHASE))
    b_in3 = b_in[:, None, :]
    return dict(
        w_in=w_in.astype(BF16), b_in=b_in3,
        w_tail=jnp.pad(w_in[:, :, tail:], pad_tail).astype(BF16), b_tail=jnp.pad(b_in3[:, :, tail:], pad_tail),
        norm_g=mlstm_norm_g[:, None, :], conv_w=conv_w, conv_b=conv_b[:, None, :],
        w_m=w_br_mlstm.astype(BF16), w_f=w_br_fourier.astype(BF16), w_c=w_br_conv.astype(BF16),
        w_o=w_out.astype(BF16),
        ln_g=ln_g.reshape(2 * DEPTH, 1, D_MODEL), ln_b=ln_b.reshape(2 * DEPTH, 1, D_MODEL),
        w_gu=w_gate_up.astype(BF16), w_d=w_down.astype(BF16))


def _trunk_layer(x2d, mods, batch, seq_len, row_len, layer, wts, init_state, prev_state):
    qk, vt, o, xf, gates = _inproj(x2d, mods, seq_len, layer, wts['w_in'], wts['b_in'])
    res = _mlstm(qk, vt, o, gates, wts['norm_g'], batch, seq_len, layer, init_state, prev_state)
    h_f = _fourier(xf, batch, seq_len)
    x1 = _merge(x2d, mods, seq_len, row_len, layer, res[0], h_f, wts)
    x2 = _ffn(x1, mods, seq_len, layer, wts['w_gu'], wts['w_d'], wts['ln_g'], wts['ln_b'])
    return x2, res[1:]


def kernel(x_prompt, x_sample, state_C, state_n, state_m, c, c_ctx, w_ada, b_ada, w_in, b_in,
           mlstm_norm_g, conv_w, conv_b, w_br_mlstm, w_br_fourier, w_br_conv, w_out, ln_g, ln_b,
           w_gate_up, w_down):
    bp, tp, d = x_prompt.shape
    bs, ts, _ = x_sample.shape
    cond = jnp.concatenate([c_ctx[None, :], c, jnp.zeros((8 - 1 - bs, d), F32)], axis=0)
    mods = _ada_mods(cond, w_ada, b_ada)
    wts = _prepared_weights(w_in, b_in, mlstm_norm_g, conv_w, conv_b, w_br_mlstm, w_br_fourier,
                            w_br_conv, w_out, ln_g, ln_b, w_gate_up, w_down)

    xp = x_prompt.reshape(bp * tp, d)
    xs = _add_pos(x_sample, _grid_pos_embed(ts // GRID_W, x_sample.dtype)).reshape(bs * ts, d)
    new_state = None
    for l in range(DEPTH):
        mods_ctx = mods[l, 0:1].reshape(1, 1, 6 * d)
        mods_smp = mods[l, 1:1 + bs].reshape(bs, 1, 6 * d)
        xp, new_state = _trunk_layer(xp, mods_ctx, bp, tp, tp, l, wts, None, new_state)
        xs, _ = _trunk_layer(xs, mods_smp, bs, ts, GRID_W, l, wts, (state_C, state_n, state_m), None)
    c_new, n_new, m_new = new_state
    return (xp.reshape(bp, tp, d), xs.reshape(bs, ts, d), c_new, n_new, m_new[..., 0])
```

```python
import functools

import numpy as np
import jax
import jax.numpy as jnp
from jax import lax
from jax.experimental import pallas as pl
from jax.experimental.pallas import tpu as pltpu

D_MODEL = 1024
DEPTH = 2
GRID_W = 64
MLSTM_H = 4
MLSTM_DH = 256
MLSTM_W = MLSTM_H * MLSTM_DH
CHUNK = 256
GATE_ROWS = 4 * MLSTM_H
FOURIER_G = 4
FOURIER_GW = 256
FOURIER_W = FOURIER_G * FOURIER_GW
CONV_W = 1024
D_FF = -(-8 * D_MODEL // (3 * 256)) * 256
ALPHA = (2 * DEPTH) ** 0.25
LN_EPS = 1e-5
POS_BASE = 10000.0

OFF_GATES = 4 * MLSTM_W
OFF_FOURIER = OFF_GATES + 4 * MLSTM_H
OFF_CONV = OFF_FOURIER + FOURIER_W
OFF_MERGE = OFF_CONV + 3 * CONV_W
N_IN = OFF_MERGE + 3 * D_MODEL

LANES = 128
WIN_PHASE = OFF_FOURIER % LANES
VMEM_LIMIT = 56 * 1024 * 1024

BF16 = jnp.bfloat16
F32 = jnp.float32


def _cparams(n_axes):
    return pltpu.CompilerParams(dimension_semantics=("arbitrary",) * n_axes,
                                vmem_limit_bytes=VMEM_LIMIT)


def _resident(shape):
    zeros = (0,) * len(shape)
    return pl.BlockSpec(shape, lambda *_: zeros, pipeline_mode=pl.Buffered(1))


def _sigmoid(x):
    return 1.0 / (1.0 + jnp.exp(-x))


def _log_sigmoid(x):
    return jnp.minimum(x, 0.0) - jnp.log1p(jnp.exp(-jnp.abs(x)))


def _layer_norm(x, g, b):
    mu = jnp.mean(x, axis=-1, keepdims=True)
    xc = x - mu
    var = jnp.mean(xc * xc, axis=-1, keepdims=True)
    return xc * lax.rsqrt(var + LN_EPS) * g + b


def _dot(a, b):
    return jnp.dot(a, b, preferred_element_type=F32)


def _ada_kernel(cond_ref, w_ref, b_ref, out_ref):
    cond = cond_ref[...]
    act = (cond * _sigmoid(cond)).astype(BF16)
    out_ref[...] = _dot(act, w_ref[...].astype(BF16)) + b_ref[...]


def _ada_mods(cond, w_ada, b_ada):
    tn = 1536
    nb = 6 * D_MODEL // tn
    return pl.pallas_call(
        _ada_kernel,
        out_shape=jax.ShapeDtypeStruct((DEPTH, 8, 6 * D_MODEL), F32),
        grid=(DEPTH, nb),
        in_specs=[pl.BlockSpec((8, D_MODEL), lambda l, j: (0, 0)),
                  pl.BlockSpec((None, D_MODEL, tn), lambda l, j: (l, 0, j)),
                  pl.BlockSpec((None, 1, tn), lambda l, j: (l, 0, j))],
        out_specs=pl.BlockSpec((None, 8, tn), lambda l, j: (l, 0, j)),
        compiler_params=_cparams(2),
        name="ada_mods",
    )(cond, w_ada, b_ada.reshape(DEPTH, 1, 6 * D_MODEL))


def _add_pos_kernel(x_ref, pos_ref, out_ref):
    out_ref[...] = x_ref[...] + pos_ref[...]


def _add_pos(x, pos):
    b, t, d = x.shape
    return pl.pallas_call(
        _add_pos_kernel,
        out_shape=jax.ShapeDtypeStruct(x.shape, x.dtype),
        grid=(b,),
        in_specs=[pl.BlockSpec((None, t, d), lambda i: (i, 0, 0)),
                  pl.BlockSpec((t, d), lambda i: (0, 0))],
        out_specs=pl.BlockSpec((None, t, d), lambda i: (i, 0, 0)),
        compiler_params=_cparams(1),
        name="add_pos",
    )(x, pos)


def _grid_pos_embed(rows, dtype):
    quarter = D_MODEL // 4
    freqs = (POS_BASE ** (-np.arange(quarter, dtype=np.float32) / quarter)).astype(np.float32)
    r = np.repeat(np.arange(rows, dtype=np.float32), GRID_W)[:, None] * freqs
    col = np.tile(np.arange(GRID_W, dtype=np.float32), rows)[:, None] * freqs
    return jnp.asarray(np.concatenate([np.sin(r), np.cos(r), np.sin(col), np.cos(col)], axis=-1), dtype)


def _shifted(a, b):
    return jnp.concatenate([a[:, WIN_PHASE:], b[:, :WIN_PHASE]], axis=1)


def _inproj_kernel(x_ref, sh_ref, sc_ref, wqk_ref, wv_ref, wo_ref, wf_ref, wft_ref,
                   bqk_ref, bv_ref, bo_ref, bf_ref, bft_ref, qk_ref, vt_ref, o_ref, f_ref, g_ref,
                   wfa_s, bfa_s):
    @pl.when(pl.program_id(0) == 0)
    def _():
        wfa_s[...] = _shifted(wf_ref[...], wft_ref[...])
        bfa_s[...] = _shifted(bf_ref[...], bft_ref[...])

    u = (x_ref[...] * (1.0 + sc_ref[...]) + sh_ref[...]).astype(BF16)
    d = D_MODEL
    for j in range(2):
        qk_ref[:, j * d:(j + 1) * d] = (
            _dot(u, wqk_ref[:, j * d:(j + 1) * d]) + bqk_ref[:, j * d:(j + 1) * d]).astype(BF16)
    v_t = (_dot(u, wv_ref[...]) + bv_ref[...]).T
    for c in range(vt_ref.shape[0]):
        vt_ref[c] = v_t[:, c * CHUNK:(c + 1) * CHUNK].astype(BF16)
    o_ref[...] = _dot(u, wo_ref[...]) + bo_ref[...]
    f_ref[...] = _dot(u, wfa_s[...]) + bfa_s[...]
    g_ref[...] = _dot(u, wf_ref[:, :LANES]) + bf_ref[:, :LANES]


def _layer_window(layer, rows, width, k):
    return pl.BlockSpec((None, rows, width), lambda *_: (layer, 0, k), pipeline_mode=pl.Buffered(1))


def _inproj(x2d, mods, seq_len, layer, w_in_bf, b_in_p):
    n = x2d.shape[0]
    tm = 512
    per_mod = max(seq_len // tm, 1) if mods.shape[0] > 1 else n
    mrow = (lambda i: i // per_mod) if mods.shape[0] > 1 else (lambda i: 0)
    d = D_MODEL
    wspec = functools.partial(_layer_window, layer)
    wins = [(2 * d, 0), (d, 2), (d, 3), (d, OFF_GATES // d), (LANES, (OFF_GATES + d) // LANES)]
    return pl.pallas_call(
        _inproj_kernel,
        out_shape=(jax.ShapeDtypeStruct((n, 2 * d), BF16),
                   jax.ShapeDtypeStruct((n // CHUNK, d, CHUNK), BF16),
                   jax.ShapeDtypeStruct((n, d), F32),
                   jax.ShapeDtypeStruct((n, d), F32),
                   jax.ShapeDtypeStruct((n, LANES), F32)),
        grid=(n // tm,),
        in_specs=[pl.BlockSpec((tm, d), lambda i: (i, 0)),
                  pl.BlockSpec((None, 1, d), lambda i: (mrow(i), 0, 0)),
                  pl.BlockSpec((None, 1, d), lambda i: (mrow(i), 0, 1))]
                 + [wspec(d, w, k) for w, k in wins] + [wspec(1, w, k) for w, k in wins],
        out_specs=(pl.BlockSpec((tm, 2 * d), lambda i: (i, 0)),
                   pl.BlockSpec((tm // CHUNK, d, CHUNK), lambda i: (i, 0, 0)),
                   pl.BlockSpec((tm, d), lambda i: (i, 0)),
                   pl.BlockSpec((tm, d), lambda i: (i, 0)),
                   pl.BlockSpec((tm, LANES), lambda i: (i, 0))),
        scratch_shapes=[pltpu.VMEM((d, d), BF16), pltpu.VMEM((1, d), F32)],
        compiler_params=_cparams(1),
        name="inproj",
    )(x2d, mods, mods, *([w_in_bf] * 5), *([b_in_p] * 5))


def _dot_ones_rhs(x, ones_mat):
    x0 = x.astype(BF16)
    r1 = x - x0.astype(F32)
    x1 = r1.astype(BF16)
    x2 = (r1 - x1.astype(F32)).astype(BF16)
    return _dot(x0, ones_mat) + _dot(x1, ones_mat) + _dot(x2, ones_mat)


def _mlstm_kernel(*refs, seq_len, seqs, layer, has_init, emit_state, n_prev):
    it = iter(refs)
    qk_ref, vt_ref, g_ref = (next(it) for _ in range(3))
    if has_init:
        c0_ref, n0_ref, m0_ref = (next(it) for _ in range(3))
    for _ in range(n_prev):
        next(it)
    hm_ref = next(it)
    if emit_state:
        cn_ref, nn_ref, mn_ref = (next(it) for _ in range(3))
    lft_s, prow_s, srow_s, wt_s, hf_s, hb_s, ct_s, n_s, m_s = it

    nc = seq_len // CHUNK
    ncb = nc * seqs
    H, DH, W = MLSTM_H, MLSTM_DH, MLSTM_W
    GR = GATE_ROWS
    assert CHUNK == DH
    ri = lax.broadcasted_iota(jnp.int32, (CHUNK, CHUNK), 0)
    ci = lax.broadcasted_iota(jnp.int32, (CHUNK, CHUNK), 1)
    lower = ri >= ci
    upper = ri <= ci
    tri_l = jnp.where(lower, 1.0, 0.0).astype(BF16)
    tri_u = jnp.where(upper, 1.0, 0.0).astype(BF16)

    for c in range(ncb):
        rows = slice(c * CHUNK, (c + 1) * CHUNK)
        lft_s[c * GR:(c + 1) * GR, :] = _log_sigmoid(g_ref[rows, :].T[:GR, :])
    lft = lft_s[...]
    prow_s[...] = _dot_ones_rhs(lft, tri_u)
    srow_s[...] = _dot_ones_rhs(lft, tri_l)
    fwd_cols = lax.broadcasted_iota(jnp.int32, (CHUNK, LANES), 1) < 2 * H
    zpad = jnp.zeros((LANES - GR, CHUNK), F32)
    for c in range(ncb):
        rows = slice(c * CHUNK, (c + 1) * CHUNK)
        slab = slice(c * GR, (c + 1) * GR)
        pcol = jnp.concatenate([prow_s[slab, :], zpad], axis=0).T
        scol = jnp.concatenate([srow_s[slab, :], zpad], axis=0).T
        wt_s[rows, :] = pltpu.roll(g_ref[rows, :], H, 1) - jnp.where(fwd_cols, pcol, scol)

    if has_init:
        for d in range(2):
            for h in range(H):
                ct_s[d, h] = c0_ref[d, h].T
                n_s[d, h] = n0_ref[d, h:h + 1, :]
                m_s[d, h] = jnp.full((1, CHUNK), m0_ref[pl.program_id(0), layer, d, h], F32)
    else:
        m_s[...] = jnp.zeros_like(m_s)
    ones_rows = jnp.ones((16, CHUNK), BF16)

    def chain(c, h, backward, use_state, update_state, q=0):
        static = isinstance(c, int)
        r0 = c * CHUNK if static else pl.multiple_of(c * CHUNK, CHUNK)
        rows = pl.ds(r0, CHUNK)
        d = 2 * q + (1 if backward else 0)
        hc = slice(h * DH, (h + 1) * DH)
        gf = (3 if backward else 1) * H + h
        qc = qk_ref[rows, hc]
        kc = qk_ref[rows, W + h * DH:W + (h + 1) * DH] * (MLSTM_DH ** -0.5)
        vt = vt_ref[c, hc, :]
        grow = pl.ds(c * GR + gf, 1)
        brow = (srow_s if backward else prow_s)[grow, :]
        g_tot = prow_s[grow, :] + srow_s[grow, :] - lft_s[grow, :]
        wb = jnp.broadcast_to(wt_s[rows, gf:gf + 1], (CHUNK, CHUNK))
        m_prev = m_s[d, h]

        dt = jnp.where(lower if backward else upper, brow + wb, -jnp.inf)
        inter = brow + m_prev
        m_t = jnp.maximum(inter, jnp.max(dt, axis=0, keepdims=True))
        p = jnp.exp(dt - m_t)
        nt_dims = (((1,), (1,)), ((), ()))
        if use_state:
            n_prev = n_s[d, h]
            lhs = jnp.concatenate([kc, jnp.broadcast_to(n_prev.astype(BF16), (16, DH)),
                                   ct_s[d, h].astype(BF16)], axis=0)
            r = lax.dot_general(lhs, qc, nt_dims, preferred_element_type=F32)
            qk_t, nq, num_c = r[:CHUNK], r[CHUNK:CHUNK + 1], r[CHUNK + 16:]
        else:
            qk_t = lax.dot_general(kc, qc, nt_dims, preferred_element_type=F32)
        st = qk_t * p
        num = _dot(vt, st.astype(BF16))
        den = jnp.sum(st, axis=0, keepdims=True)
        if use_state:
            a = jnp.exp(inter - m_t)
            num = a * num_c + num
            den = a * nq + den
        hval = num * (1.0 / jnp.maximum(jnp.abs(den), jnp.exp(-m_t)))
        (hb_s if backward else hf_s)[c, h] = hval

        if update_state:
            m_new = jnp.maximum(g_tot + m_prev, jnp.max(wb, axis=0, keepdims=True) + g_tot)
            wk = jnp.exp(wb + (g_tot - m_new))
            kw = kc * wk.astype(BF16)
            r2 = _dot(jnp.concatenate([vt, ones_rows], axis=0), kw)
            kv, nsum = r2[:DH], r2[DH:DH + 1]
            if use_state:
                a_c = jnp.exp(g_tot + m_prev - m_new)
                ct_s[d, h] = a_c * ct_s[d, h] + kv
                n_s[d, h] = a_c * n_prev + nsum
            else:
                ct_s[d, h] = kv
                n_s[d, h] = nsum
            m_s[d, h] = m_new

    def step(j, use_state, update_state):
        for q in range(seqs):
            for h in range(H):
                chain(q * nc + j, h, False, use_state, update_state, q)
                chain(q * nc + nc - 1 - j, h, True, use_state, update_state, q)

    first = 0
    if not has_init:
        step(0, False, True)
        first = 1
    last = nc if emit_state else nc - 1
    if last - first <= 2:
        for j in range(first, last):
            step(j, True, True)
    else:
        def body(j, carry):
            step(j, True, True)
            return carry
        lax.fori_loop(first, last, body, 0)
    if not emit_state:
        step(nc - 1, True, False)

    def finalize(c):
        static = isinstance(c, int)
        rows = pl.ds(c * CHUNK if static else pl.multiple_of(c * CHUNK, CHUNK), CHUNK)
        for h in range(H):
            hc = slice(h * DH, (h + 1) * DH)
            ht = hf_s[c, h] + hb_s[c, h]
            mu = jnp.mean(ht, axis=0, keepdims=True)
            cen = ht - mu
            var = jnp.mean(cen * cen, axis=0, keepdims=True)
            hm_ref[rows, hc] = (cen * lax.rsqrt(var + LN_EPS)).T

    if ncb <= 2:
        for c in range(ncb):
            finalize(c)
    else:
        def fbody(c, carry):
            finalize(c)
            return carry
        lax.fori_loop(0, ncb, fbody, 0)

    if emit_state:
        for q in range(seqs):
            for d in range(2):
                for h in range(H):
                    cn_ref[q, d, h] = ct_s[2 * q + d, h].T
                    nn_ref[q, d, h:h + 1, :] = n_s[2 * q + d, h]
                    mn_ref[q, d, h:h + 1, :] = m_s[2 * q + d, h][:, :LANES]


def _mlstm(qk, vt, gates, batch, seq_len, layer, init_state, prev_state):
    n = batch * seq_len
    w = MLSTM_W
    H, DH = MLSTM_H, MLSTM_DH
    has_init = init_state is not None
    emit_state = not has_init
    seqs = 2 if (seq_len == CHUNK and not has_init and batch % 2 == 0) else 1
    rows = seq_len * seqs
    ncb = rows // CHUNK
    in_specs = [pl.BlockSpec((rows, 2 * w), lambda b: (b, 0)),
                pl.BlockSpec((ncb, w, CHUNK), lambda b: (b, 0, 0)),
                pl.BlockSpec((rows, LANES), lambda b: (b, 0))]
    args = [qk, vt, gates]
    init_specs = [pl.BlockSpec((None, None, 2, H, DH, DH), lambda b: (b, layer, 0, 0, 0, 0)),
                  pl.BlockSpec((None, None, 2, H, DH), lambda b: (b, layer, 0, 0, 0))]
    new_specs = [pl.BlockSpec((seqs, None, 2, H, DH, DH), lambda b: (b, layer, 0, 0, 0, 0)),
                 pl.BlockSpec((seqs, None, 2, H, DH), lambda b: (b, layer, 0, 0, 0)),
                 pl.BlockSpec((seqs, None, 2, H, LANES), lambda b: (b, layer, 0, 0, 0))]
    aliases = {}
    if has_init:
        in_specs += init_specs + [pl.BlockSpec(memory_space=pltpu.SMEM)]
        args += list(init_state)
    out_shape = [jax.ShapeDtypeStruct((n, w), F32)]
    out_specs = [pl.BlockSpec((rows, w), lambda b: (b, 0))]
    if emit_state:
        out_shape += [jax.ShapeDtypeStruct((batch, DEPTH, 2, H, DH, DH), F32),
                      jax.ShapeDtypeStruct((batch, DEPTH, 2, H, DH), F32),
                      jax.ShapeDtypeStruct((batch, DEPTH, 2, H, LANES), F32)]
        out_specs += new_specs
        if prev_state is not None:
            aliases = {len(args) + k: 1 + k for k in range(3)}
            in_specs += [pl.BlockSpec(memory_space=pl.ANY)] * 3
            args += list(prev_state)
    scratch = [pltpu.VMEM((ncb * GATE_ROWS, CHUNK), F32),
               pltpu.VMEM((ncb * GATE_ROWS, CHUNK), F32),
               pltpu.VMEM((ncb * GATE_ROWS, CHUNK), F32),
               pltpu.VMEM((rows, LANES), F32),
               pltpu.VMEM((ncb, H, DH, CHUNK), F32),
               pltpu.VMEM((ncb, H, DH, CHUNK), F32),
               pltpu.VMEM((2 * seqs, H, DH, DH), F32),
               pltpu.VMEM((2 * seqs, H, 1, DH), F32),
               pltpu.VMEM((2 * seqs, H, 1, CHUNK), F32)]
    return pl.pallas_call(
        functools.partial(_mlstm_kernel, seq_len=seq_len, seqs=seqs, layer=layer, has_init=has_init,
                          emit_state=emit_state, n_prev=len(aliases)),
        out_shape=tuple(out_shape),
        grid=(batch // seqs,),
        in_specs=in_specs,
        out_specs=tuple(out_specs),
        scratch_shapes=scratch,
        input_output_aliases=aliases,
        compiler_params=_cparams(1),
        name="mlstm_init" if has_init else "mlstm_zero",
    )(*args)


def _dft_tables(seq_len):
    gw = FOURIER_GW
    kc = np.arange(gw)
    ang_c = 2.0 * np.pi * ((kc[:, None] * kc[None, :]) % gw) / gw
    w_chan = np.concatenate([np.cos(ang_c), np.sin(ang_c)], axis=1)
    kt = np.arange(seq_len)
    ang_t = 2.0 * np.pi * ((kt[:, None] * kt[None, :]) % seq_len) / seq_len
    scale = 1.0 / np.sqrt(seq_len * gw)
    return (jnp.asarray(w_chan, dtype=F32).astype(BF16),
            jnp.asarray(np.cos(ang_t) * scale, dtype=F32).astype(BF16),
            jnp.asarray(-np.sin(ang_t) * scale, dtype=F32).astype(BF16))


def _fourier_kernel(x_ref, wc_ref, ct_ref, st_ref, out_ref, *, seq_len):
    gw = FOURIER_GW
    for g in range(FOURIER_G):
        cols = slice(g * gw, (g + 1) * gw)
        a = _dot(x_ref[:, cols].astype(BF16), wc_ref[...])
        for r in range(0, x_ref.shape[0], seq_len):
            a_c = a[r:r + seq_len, :gw].astype(BF16)
            a_s = a[r:r + seq_len, gw:].astype(BF16)
            y = _dot(ct_ref[...], a_c) + _dot(st_ref[...], a_s)
            out_ref[r:r + seq_len, cols] = y.astype(BF16)


def _fourier(xf, batch, seq_len):
    w_chan, c_t, s_t = _dft_tables(seq_len)
    n = batch * seq_len
    tm = max(seq_len, 1024)
    return pl.pallas_call(
        functools.partial(_fourier_kernel, seq_len=seq_len),
        out_shape=jax.ShapeDtypeStruct((n, FOURIER_W), BF16),
        grid=(n // tm,),
        in_specs=[pl.BlockSpec((tm, FOURIER_W), lambda b: (b, 0)),
                  _resident(w_chan.shape), _resident(c_t.shape), _resident(s_t.shape)],
        out_specs=pl.BlockSpec((tm, FOURIER_W), lambda b: (b, 0)),
        compiler_params=_cparams(1),
        name="fourier",
    )(xf, w_chan, c_t, s_t)


def _merge_kernel(*refs, row_len):
    (x_ref, sh_ref, sc_ref, gt_ref, hn_ref, o_ref, hf_ref), rest = refs[:7], refs[7:]
    w_refs, b_refs = rest[:7], rest[7:14]
    (ng_ref, cw_ref, cb_ref, wm_ref, wf_ref, wc_ref, wo_ref, lg_ref, lb_ref, out_ref,
     wal_s, bal_s) = rest[14:]

    @pl.when(pl.program_id(0) == 0)
    def _():
        for k in range(6):
            wal_s[k] = _shifted(w_refs[k][...], w_refs[k + 1][:, :LANES])
            bal_s[k] = _shifted(b_refs[k][...], b_refs[k + 1][:, :LANES])

    x = x_ref[...]
    tm = x.shape[0]
    u = (x * (1.0 + sc_ref[...]) + sh_ref[...]).astype(BF16)

    def seg(k):
        return _dot(u, wal_s[k]) + bal_s[k]

    uc = seg(1) * seg(2)
    pos = lax.broadcasted_iota(jnp.int32, (tm, 1), 0) % row_len
    prev = jnp.where(pos == 0, 0.0, pltpu.roll(uc, 1, 0))
    nxt = jnp.where(pos == row_len - 1, 0.0, pltpu.roll(uc, tm - 1, 0))
    yc = prev * cw_ref[0:1, :] + uc * cw_ref[1:2, :] + nxt * cw_ref[2:3, :] + cb_ref[...]
    h_c = (seg(0) * yc).astype(BF16)
    h_m = (_sigmoid(o_ref[...]) * hn_ref[...] * ng_ref[...]).astype(BF16)
    merged = (_sigmoid(seg(3)) * _dot(h_m, wm_ref[...])
              + _sigmoid(seg(4)) * _dot(hf_ref[...], wf_ref[...])
              + _sigmoid(seg(5)) * _dot(h_c, wc_ref[...]))
    y = _dot(merged.astype(BF16), wo_ref[...])
    out_ref[...] = _layer_norm(ALPHA * x + gt_ref[...] * y, lg_ref[...], lb_ref[...])


def _merge(x2d, mods, seq_len, row_len, layer, h_n, o, h_f, wts):
    n = x2d.shape[0]
    d = D_MODEL
    tm = 256
    per_mod = seq_len // tm
    mrow = (lambda i: i // per_mod) if mods.shape[0] > 1 else (lambda i: 0)
    tile = lambda i: (i, 0)
    wspec = functools.partial(_layer_window, layer)
    first = (OFF_CONV - WIN_PHASE) // d
    wins = [wspec(d, d, first + k) for k in range(6)] + [wspec(d, LANES, 0)]
    bwins = [wspec(1, d, first + k) for k in range(6)] + [wspec(1, LANES, 0)]
    return pl.pallas_call(
        functools.partial(_merge_kernel, row_len=row_len),
        out_shape=jax.ShapeDtypeStruct((n, d), F32),
        grid=(n // tm,),
        in_specs=[pl.BlockSpec((tm, d), tile),
                  pl.BlockSpec((None, 1, d), lambda i: (mrow(i), 0, 0)),
                  pl.BlockSpec((None, 1, d), lambda i: (mrow(i), 0, 1)),
                  pl.BlockSpec((None, 1, d), lambda i: (mrow(i), 0, 2)),
                  pl.BlockSpec((tm, d), tile), pl.BlockSpec((tm, d), tile), pl.BlockSpec((tm, d), tile)]
                 + wins + bwins
                 + [wspec(1, d, 0),
                    wspec(3, d, 0), wspec(1, d, 0), wspec(d, d, 0), wspec(d, d, 0), wspec(d, d, 0),
                    wspec(d, d, 0),
                    pl.BlockSpec((None, 1, d), lambda i: (2 * layer, 0, 0)),
                    pl.BlockSpec((None, 1, d), lambda i: (2 * layer, 0, 0))],
        out_specs=pl.BlockSpec((tm, d), tile),
        scratch_shapes=[pltpu.VMEM((6, d, d), BF16), pltpu.VMEM((6, 1, d), F32)],
        compiler_params=_cparams(1),
        name="merge",
    )(x2d, mods, mods, mods, h_n, o, h_f, *([wts['w_in']] * 6), wts['w_tail'], *([wts['b_in']] * 6),
      wts['b_tail'], wts['norm_g'], wts['conv_w'], wts['conv_b'], wts['w_m'], wts['w_f'], wts['w_c'],
      wts['w_o'], wts['ln_g'], wts['ln_b'])


def _ffn_kernel(x_ref, sh_ref, sc_ref, gt_ref, wgu_ref, wd_ref, lg_ref, lb_ref, out_ref):
    x = x_ref[...]
    u = (x * (1.0 + sc_ref[...]) + sh_ref[...]).astype(BF16)
    a = _dot(u, wgu_ref[:, :D_FF])
    b = _dot(u, wgu_ref[:, D_FF:])
    hidden = (a * _sigmoid(a) * b).astype(BF16)
    y = _dot(hidden, wd_ref[...])
    out_ref[...] = _layer_norm(ALPHA * x + gt_ref[...] * y, lg_ref[...], lb_ref[...])


def _ffn(x2d, mods, seq_len, layer, w_gu_bf, w_d_bf, ln_g, ln_b):
    n = x2d.shape[0]
    d = D_MODEL
    tm = 512
    per_mod = seq_len // tm
    mrow = (lambda i: i // per_mod) if mods.shape[0] > 1 else (lambda i: 0)
    tile = lambda i: (i, 0)
    wspec = functools.partial(_layer_window, layer)
    return pl.pallas_call(
        _ffn_kernel,
        out_shape=jax.ShapeDtypeStruct((n, d), F32),
        grid=(n // tm,),
        in_specs=[pl.BlockSpec((tm, d), tile),
                  pl.BlockSpec((None, 1, d), lambda i: (mrow(i), 0, 3)),
                  pl.BlockSpec((None, 1, d), lambda i: (mrow(i), 0, 4)),
                  pl.BlockSpec((None, 1, d), lambda i: (mrow(i), 0, 5)),
                  wspec(d, 2 * D_FF, 0), wspec(D_FF, d, 0),
                  pl.BlockSpec((None, 1, d), lambda i: (2 * layer + 1, 0, 0)),
                  pl.BlockSpec((None, 1, d), lambda i: (2 * layer + 1, 0, 0))],
        out_specs=pl.BlockSpec((tm, d), tile),
        compiler_params=_cparams(1),
        name="ffn",
    )(x2d, mods, mods, mods, w_gu_bf, w_d_bf, ln_g, ln_b)


def _prepared_weights(w_in, b_in, mlstm_norm_g, conv_w, conv_b, w_br_mlstm, w_br_fourier, w_br_conv,
                      w_out, ln_g, ln_b, w_gate_up, w_down):
    tail = N_IN - WIN_PHASE
    pad_tail = ((0, 0), (0, 0), (0, LANES - WIN_PHASE))
    b_in3 = b_in[:, None, :]
    return dict(
        w_in=w_in.astype(BF16), b_in=b_in3,
        w_tail=jnp.pad(w_in[:, :, tail:], pad_tail).astype(BF16), b_tail=jnp.pad(b_in3[:, :, tail:], pad_tail),
        norm_g=mlstm_norm_g[:, None, :], conv_w=conv_w, conv_b=conv_b[:, None, :],
        w_m=w_br_mlstm.astype(BF16), w_f=w_br_fourier.astype(BF16), w_c=w_br_conv.astype(BF16),
        w_o=w_out.astype(BF16),
        ln_g=ln_g.reshape(2 * DEPTH, 1, D_MODEL), ln_b=ln_b.reshape(2 * DEPTH, 1, D_MODEL),
        w_gu=w_gate_up.astype(BF16), w_d=w_down.astype(BF16))


def _trunk_layer(x2d, mods, batch, seq_len, row_len, layer, wts, init_state, prev_state):
    qk, vt, o, xf, gates = _inproj(x2d, mods, seq_len, layer, wts['w_in'], wts['b_in'])
    res = _mlstm(qk, vt, gates, batch, seq_len, layer, init_state, prev_state)
    h_f = _fourier(xf, batch, seq_len)
    x1 = _merge(x2d, mods, seq_len, row_len, layer, res[0], o, h_f, wts)
    x2 = _ffn(x1, mods, seq_len, layer, wts['w_gu'], wts['w_d'], wts['ln_g'], wts['ln_b'])
    return x2, res[1:]


def kernel(x_prompt, x_sample, state_C, state_n, state_m, c, c_ctx, w_ada, b_ada, w_in, b_in,
           mlstm_norm_g, conv_w, conv_b, w_br_mlstm, w_br_fourier, w_br_conv, w_out, ln_g, ln_b,
           w_gate_up, w_down):
    bp, tp, d = x_prompt.shape
    bs, ts, _ = x_sample.shape
    cond = jnp.concatenate([c_ctx[None, :], c, jnp.zeros((8 - 1 - bs, d), F32)], axis=0)
    mods = _ada_mods(cond, w_ada, b_ada)
    wts = _prepared_weights(w_in, b_in, mlstm_norm_g, conv_w, conv_b, w_br_mlstm, w_br_fourier,
                            w_br_conv, w_out, ln_g, ln_b, w_gate_up, w_down)

    xp = x_prompt.reshape(bp * tp, d)
    xs = _add_pos(x_sample, _grid_pos_embed(ts // GRID_W, x_sample.dtype)).reshape(bs * ts, d)
    new_state = None
    for l in range(DEPTH):
        mods_ctx = mods[l, 0:1].reshape(1, 1, 6 * d)
        mods_smp = mods[l, 1:1 + bs].reshape(bs, 1, 6 * d)
        xp, new_state = _trunk_layer(xp, mods_ctx, bp, tp, tp, l, wts, None, new_state)
        xs, _ = _trunk_layer(xs, mods_smp, bs, ts, GRID_W, l, wts, (state_C, state_n, state_m), None)
    c_new, n_new, m_new = new_state
    return (xp.reshape(bp, tp, d), xs.reshape(bs, ts, d), c_new, n_new, m_new[..., 0])
```

```python
import functools

import numpy as np
import jax
import jax.numpy as jnp
from jax import lax
from jax.experimental import pallas as pl
from jax.experimental.pallas import tpu as pltpu

D_MODEL = 1024
DEPTH = 2
GRID_W = 64
MLSTM_H = 4
MLSTM_DH = 256
MLSTM_W = MLSTM_H * MLSTM_DH
CHUNK = 256
GATE_ROWS = 4 * MLSTM_H
FOURIER_G = 4
FOURIER_GW = 256
FOURIER_W = FOURIER_G * FOURIER_GW
CONV_W = 1024
D_FF = -(-8 * D_MODEL // (3 * 256)) * 256
ALPHA = (2 * DEPTH) ** 0.25
LN_EPS = 1e-5
POS_BASE = 10000.0

OFF_GATES = 4 * MLSTM_W
OFF_FOURIER = OFF_GATES + 4 * MLSTM_H
OFF_CONV = OFF_FOURIER + FOURIER_W
OFF_MERGE = OFF_CONV + 3 * CONV_W
N_IN = OFF_MERGE + 3 * D_MODEL

LANES = 128
WIN_PHASE = OFF_FOURIER % LANES
VMEM_LIMIT = 56 * 1024 * 1024

BF16 = jnp.bfloat16
F32 = jnp.float32


def _cparams(n_axes):
    return pltpu.CompilerParams(dimension_semantics=("arbitrary",) * n_axes,
                                vmem_limit_bytes=VMEM_LIMIT)


def _resident(shape):
    zeros = (0,) * len(shape)
    return pl.BlockSpec(shape, lambda *_: zeros, pipeline_mode=pl.Buffered(1))


def _sigmoid(x):
    return 1.0 / (1.0 + jnp.exp(-x))


def _log_sigmoid(x):
    return jnp.minimum(x, 0.0) - jnp.log1p(jnp.exp(-jnp.abs(x)))


def _layer_norm(x, g, b):
    mu = jnp.mean(x, axis=-1, keepdims=True)
    xc = x - mu
    var = jnp.mean(xc * xc, axis=-1, keepdims=True)
    return xc * lax.rsqrt(var + LN_EPS) * g + b


def _dot(a, b):
    return jnp.dot(a, b, preferred_element_type=F32)


def _ada_kernel(cond_ref, w_ref, b_ref, out_ref):
    cond = cond_ref[...]
    act = (cond * _sigmoid(cond)).astype(BF16)
    out_ref[...] = _dot(act, w_ref[...].astype(BF16)) + b_ref[...]


def _ada_mods(cond, w_ada, b_ada):
    tn = 1536
    nb = 6 * D_MODEL // tn
    return pl.pallas_call(
        _ada_kernel,
        out_shape=jax.ShapeDtypeStruct((DEPTH, 8, 6 * D_MODEL), F32),
        grid=(DEPTH, nb),
        in_specs=[pl.BlockSpec((8, D_MODEL), lambda l, j: (0, 0)),
                  pl.BlockSpec((None, D_MODEL, tn), lambda l, j: (l, 0, j)),
                  pl.BlockSpec((None, 1, tn), lambda l, j: (l, 0, j))],
        out_specs=pl.BlockSpec((None, 8, tn), lambda l, j: (l, 0, j)),
        compiler_params=_cparams(2),
        name="ada_mods",
    )(cond, w_ada, b_ada.reshape(DEPTH, 1, 6 * D_MODEL))


def _cast_kernel(x_ref, out_ref):
    out_ref[...] = x_ref[...].astype(out_ref.dtype)


def _to_bf16(w):
    depth, rows, cols = w.shape
    tr = 256
    return pl.pallas_call(
        _cast_kernel,
        out_shape=jax.ShapeDtypeStruct(w.shape, BF16),
        grid=(depth, rows // tr),
        in_specs=[pl.BlockSpec((None, tr, cols), lambda l, i: (l, i, 0))],
        out_specs=pl.BlockSpec((None, tr, cols), lambda l, i: (l, i, 0)),
        compiler_params=_cparams(2),
        name="to_bf16",
    )(w)


def _add_pos_kernel(x_ref, pos_ref, out_ref):
    out_ref[...] = x_ref[...] + pos_ref[...]


def _add_pos(x, pos):
    b, t, d = x.shape
    return pl.pallas_call(
        _add_pos_kernel,
        out_shape=jax.ShapeDtypeStruct(x.shape, x.dtype),
        grid=(b,),
        in_specs=[pl.BlockSpec((None, t, d), lambda i: (i, 0, 0)),
                  pl.BlockSpec((t, d), lambda i: (0, 0))],
        out_specs=pl.BlockSpec((None, t, d), lambda i: (i, 0, 0)),
        compiler_params=_cparams(1),
        name="add_pos",
    )(x, pos)


def _grid_pos_embed(rows, dtype):
    quarter = D_MODEL // 4
    freqs = (POS_BASE ** (-np.arange(quarter, dtype=np.float32) / quarter)).astype(np.float32)
    r = np.repeat(np.arange(rows, dtype=np.float32), GRID_W)[:, None] * freqs
    col = np.tile(np.arange(GRID_W, dtype=np.float32), rows)[:, None] * freqs
    return jnp.asarray(np.concatenate([np.sin(r), np.cos(r), np.sin(col), np.cos(col)], axis=-1), dtype)


def _shifted(a, b):
    return jnp.concatenate([a[:, WIN_PHASE:], b[:, :WIN_PHASE]], axis=1)


def _gate_tables(g):
    GR, H = GATE_ROWS, MLSTM_H
    g_t = g.T[:GR, :]
    lf = _log_sigmoid(g_t)
    fwd = lax.broadcasted_iota(jnp.int32, (GR, CHUNK), 0) < 2 * H
    lane = lax.broadcasted_iota(jnp.int32, (GR, CHUNK), 1)

    def scans(x, op, fill):
        left, right = x, x
        k = 1
        while k < CHUNK:
            left = op(left, jnp.where(lane >= k, pltpu.roll(left, k, 1), fill))
            right = op(right, jnp.where(lane < CHUNK - k, pltpu.roll(right, CHUNK - k, 1), fill))
            k *= 2
        return left, right

    pre, suf = scans(lf, jnp.add, 0.0)
    cum = jnp.where(fwd, pre, suf)
    tot = pre + suf - lf
    w = pltpu.roll(g_t, H, 0) - cum
    pm, sm = scans(w, jnp.maximum, -jnp.inf)
    cmax = jnp.where(fwd, pm, sm)
    wmax = jnp.broadcast_to(jnp.max(w, axis=1, keepdims=True), w.shape)
    wt = jnp.concatenate([w, jnp.zeros((LANES - GR, CHUNK), F32)], axis=0).T
    return (cum, tot, cmax, wmax), wt


def _inproj_kernel(x_ref, sh_ref, sc_ref, wqk_ref, wv_ref, wo_ref, wf_ref, wft_ref,
                   bqk_ref, bv_ref, bo_ref, bf_ref, bft_ref, qk_ref, vt_ref, o_ref, f_ref, tab_ref,
                   wt_ref, wfa_s, bfa_s):
    @pl.when(pl.program_id(0) == 0)
    def _():
        wfa_s[...] = _shifted(wf_ref[...], wft_ref[...])
        bfa_s[...] = _shifted(bf_ref[...], bft_ref[...])

    u = (x_ref[...] * (1.0 + sc_ref[...]) + sh_ref[...]).astype(BF16)
    d = D_MODEL
    g = _dot(u, wf_ref[:, :LANES]) + bf_ref[:, :LANES]
    for c in range(tab_ref.shape[0]):
        rows = slice(c * CHUNK, (c + 1) * CHUNK)
        tabs, wt = _gate_tables(g[rows, :])
        wt_ref[rows, :] = wt
        for k, tab in enumerate(tabs):
            tab_ref[c, k] = tab
    qk_ref[:, :d] = (_dot(u, wqk_ref[:, :d]) + bqk_ref[:, :d]).astype(BF16)
    qk_ref[:, d:] = ((_dot(u, wqk_ref[:, d:]) + bqk_ref[:, d:]) * (MLSTM_DH ** -0.5)).astype(BF16)
    v_t = (_dot(u, wv_ref[...]) + bv_ref[...]).T
    for c in range(vt_ref.shape[0]):
        vt_ref[c] = v_t[:, c * CHUNK:(c + 1) * CHUNK].astype(BF16)
    o_ref[...] = _dot(u, wo_ref[...]) + bo_ref[...]
    f_ref[...] = _dot(u, wfa_s[...]) + bfa_s[...]


def _layer_window(layer, rows, width, k):
    return pl.BlockSpec((None, rows, width), lambda *_: (layer, 0, k), pipeline_mode=pl.Buffered(1))


def _inproj(x2d, mods, seq_len, layer, w_in_bf, b_in_p):
    n = x2d.shape[0]
    tm = 512
    per_mod = max(seq_len // tm, 1) if mods.shape[0] > 1 else n
    mrow = (lambda i: i // per_mod) if mods.shape[0] > 1 else (lambda i: 0)
    d = D_MODEL
    wspec = functools.partial(_layer_window, layer)
    wins = [(2 * d, 0), (d, 2), (d, 3), (d, OFF_GATES // d), (LANES, (OFF_GATES + d) // LANES)]
    return pl.pallas_call(
        _inproj_kernel,
        out_shape=(jax.ShapeDtypeStruct((n, 2 * d), BF16),
                   jax.ShapeDtypeStruct((n // CHUNK, d, CHUNK), BF16),
                   jax.ShapeDtypeStruct((n, d), F32),
                   jax.ShapeDtypeStruct((n, d), F32),
                   jax.ShapeDtypeStruct((n // CHUNK, 4, GATE_ROWS, CHUNK), F32),
                   jax.ShapeDtypeStruct((n, LANES), F32)),
        grid=(n // tm,),
        in_specs=[pl.BlockSpec((tm, d), lambda i: (i, 0)),
                  pl.BlockSpec((None, 1, d), lambda i: (mrow(i), 0, 0)),
                  pl.BlockSpec((None, 1, d), lambda i: (mrow(i), 0, 1))]
                 + [wspec(d, w, k) for w, k in wins] + [wspec(1, w, k) for w, k in wins],
        out_specs=(pl.BlockSpec((tm, 2 * d), lambda i: (i, 0)),
                   pl.BlockSpec((tm // CHUNK, d, CHUNK), lambda i: (i, 0, 0)),
                   pl.BlockSpec((tm, d), lambda i: (i, 0)),
                   pl.BlockSpec((tm, d), lambda i: (i, 0)),
                   pl.BlockSpec((tm // CHUNK, 4, GATE_ROWS, CHUNK), lambda i: (i, 0, 0, 0)),
                   pl.BlockSpec((tm, LANES), lambda i: (i, 0))),
        scratch_shapes=[pltpu.VMEM((d, d), BF16), pltpu.VMEM((1, d), F32)],
        compiler_params=_cparams(1),
        name="inproj",
    )(x2d, mods, mods, *([w_in_bf] * 5), *([b_in_p] * 5))


def _mlstm_kernel(*refs, seq_len, seqs, layer, has_init, emit_state, n_prev):
    it = iter(refs)
    qk_ref, vt_ref, tab_ref, wt_ref = (next(it) for _ in range(4))
    if has_init:
        c0_ref, n0_ref, m0_ref = (next(it) for _ in range(3))
    for _ in range(n_prev):
        next(it)
    hm_ref = next(it)
    if emit_state:
        cn_ref, nn_ref, mn_ref = (next(it) for _ in range(3))
    hf_s, hb_s, ct_s, n_s, m_s = it

    nc = seq_len // CHUNK
    ncb = nc * seqs
    H, DH, W = MLSTM_H, MLSTM_DH, MLSTM_W
    HALF = CHUNK // 2
    assert CHUNK == DH
    ri = lax.broadcasted_iota(jnp.int32, (HALF, HALF), 0)
    ci = lax.broadcasted_iota(jnp.int32, (HALF, HALF), 1)
    tri_mask = {False: ri <= ci, True: ri >= ci}

    if has_init:
        for d in range(2):
            for h in range(H):
                ct_s[d, h] = c0_ref[d, h].T
                n_s[d, h] = n0_ref[d, h:h + 1, :]
                m_s[d, h] = jnp.full((1, CHUNK), m0_ref[pl.program_id(0), layer, d, h], F32)
    else:
        m_s[...] = jnp.zeros_like(m_s)
    ones_rows = jnp.ones((16, CHUNK), BF16)

    def chain(c, h, backward, use_state, update_state, q=0):
        static = isinstance(c, int)
        r0 = c * CHUNK if static else pl.multiple_of(c * CHUNK, CHUNK)
        rows = pl.ds(r0, CHUNK)
        d = 2 * q + (1 if backward else 0)
        hc = slice(h * DH, (h + 1) * DH)
        gf = (3 if backward else 1) * H + h
        qc = qk_ref[rows, hc]
        kc = qk_ref[rows, W + h * DH:W + (h + 1) * DH]
        vt1 = jnp.concatenate([vt_ref[c, hc, :], ones_rows], axis=0)
        grow = pl.ds(gf, 1)
        brow = tab_ref[c, 0, grow, :]
        g_tot = tab_ref[c, 1, grow, :]
        cmax = tab_ref[c, 2, grow, :]
        wmax = tab_ref[c, 3, grow, :]
        wb = jnp.broadcast_to(wt_ref[rows, gf:gf + 1], (CHUNK, CHUNK))
        m_prev = m_s[d, h]

        inter = brow + m_prev
        m_t = jnp.maximum(inter, brow + cmax)
        e_row = brow - m_t
        nt_dims = (((1,), (1,)), ((), ()))
        if use_state:
            n_prev = n_s[d, h]
            lhs = jnp.concatenate([kc, jnp.broadcast_to(n_prev.astype(BF16), (16, DH)),
                                   ct_s[d, h].astype(BF16)], axis=0)
            r = lax.dot_general(lhs, qc, nt_dims, preferred_element_type=F32)
            qk_t, nq, num_c = r[:CHUNK], r[CHUNK:CHUNK + 1], r[CHUNK + 16:]
        else:
            qk_t = lax.dot_general(kc, qc, nt_dims, preferred_element_type=F32)

        def quad(si, ti, masked):
            rs, cs = slice(si * HALF, (si + 1) * HALF), slice(ti * HALF, (ti + 1) * HALF)
            arg = wb[rs, cs] + e_row[:, cs]
            if masked:
                arg = jnp.where(tri_mask[backward], arg, -jnp.inf)
            return qk_t[rs, cs] * jnp.exp(arg)

        zero = jnp.zeros((HALF, HALF), F32)
        if backward:
            st = jnp.concatenate([jnp.concatenate([quad(0, 0, True), zero], axis=1),
                                  jnp.concatenate([quad(1, 0, False), quad(1, 1, True)], axis=1)], axis=0)
        else:
            st = jnp.concatenate([jnp.concatenate([quad(0, 0, True), quad(0, 1, False)], axis=1),
                                  jnp.concatenate([zero, quad(1, 1, True)], axis=1)], axis=0)
        r1 = _dot(vt1, st.astype(BF16))
        num, den = r1[:DH], r1[DH:DH + 1]
        if use_state:
            a = jnp.exp(inter - m_t)
            num = a * num_c + num
            den = a * nq + den
        hval = num * (1.0 / jnp.maximum(jnp.abs(den), jnp.exp(-m_t)))
        (hb_s if backward else hf_s)[c, h] = hval

        if update_state:
            m_new = jnp.maximum(g_tot + m_prev, wmax + g_tot)
            wk = jnp.exp(wb + (g_tot - m_new))
            kw = kc * wk.astype(BF16)
            r2 = _dot(vt1, kw)
            kv, nsum = r2[:DH], r2[DH:DH + 1]
            if use_state:
                a_c = jnp.exp(g_tot + m_prev - m_new)
                ct_s[d, h] = a_c * ct_s[d, h] + kv
                n_s[d, h] = a_c * n_prev + nsum
            else:
                ct_s[d, h] = kv
                n_s[d, h] = nsum
            m_s[d, h] = m_new

    def step(j, use_state, update_state):
        for q in range(seqs):
            for h in range(H):
                chain(q * nc + j, h, False, use_state, update_state, q)
                chain(q * nc + nc - 1 - j, h, True, use_state, update_state, q)

    first = 0
    if not has_init:
        step(0, False, True)
        first = 1
    last = nc if emit_state else nc - 1
    if last - first <= 2:
        for j in range(first, last):
            step(j, True, True)
    else:
        def body(j, carry):
            step(j, True, True)
            return carry
        lax.fori_loop(first, last, body, 0)
    if not emit_state:
        step(nc - 1, True, False)

    def finalize(c):
        static = isinstance(c, int)
        rows = pl.ds(c * CHUNK if static else pl.multiple_of(c * CHUNK, CHUNK), CHUNK)
        for h in range(H):
            hc = slice(h * DH, (h + 1) * DH)
            ht = hf_s[c, h] + hb_s[c, h]
            mu = jnp.mean(ht, axis=0, keepdims=True)
            cen = ht - mu
            var = jnp.mean(cen * cen, axis=0, keepdims=True)
            hm_ref[rows, hc] = (cen * lax.rsqrt(var + LN_EPS)).T

    if ncb <= 2:
        for c in range(ncb):
            finalize(c)
    else:
        def fbody(c, carry):
            finalize(c)
            return carry
        lax.fori_loop(0, ncb, fbody, 0)

    if emit_state:
        for q in range(seqs):
            for d in range(2):
                for h in range(H):
                    cn_ref[q, d, h] = ct_s[2 * q + d, h].T
                    nn_ref[q, d, h:h + 1, :] = n_s[2 * q + d, h]
                    mn_ref[q, d, h:h + 1, :] = m_s[2 * q + d, h][:, :LANES]


def _mlstm(qk, vt, tabs, wt, batch, seq_len, layer, init_state, prev_state):
    n = batch * seq_len
    w = MLSTM_W
    H, DH = MLSTM_H, MLSTM_DH
    has_init = init_state is not None
    emit_state = not has_init
    seqs = 2 if (seq_len == CHUNK and not has_init and batch % 2 == 0) else 1
    rows = seq_len * seqs
    ncb = rows // CHUNK
    in_specs = [pl.BlockSpec((rows, 2 * w), lambda b: (b, 0)),
                pl.BlockSpec((ncb, w, CHUNK), lambda b: (b, 0, 0)),
                pl.BlockSpec((ncb, 4, GATE_ROWS, CHUNK), lambda b: (b, 0, 0, 0)),
                pl.BlockSpec((rows, LANES), lambda b: (b, 0))]
    args = [qk, vt, tabs, wt]
    init_specs = [pl.BlockSpec((None, None, 2, H, DH, DH), lambda b: (b, layer, 0, 0, 0, 0)),
                  pl.BlockSpec((None, None, 2, H, DH), lambda b: (b, layer, 0, 0, 0))]
    new_specs = [pl.BlockSpec((seqs, None, 2, H, DH, DH), lambda b: (b, layer, 0, 0, 0, 0)),
                 pl.BlockSpec((seqs, None, 2, H, DH), lambda b: (b, layer, 0, 0, 0)),
                 pl.BlockSpec((seqs, None, 2, H, LANES), lambda b: (b, layer, 0, 0, 0))]
    aliases = {}
    if has_init:
        in_specs += init_specs + [pl.BlockSpec(memory_space=pltpu.SMEM)]
        args += list(init_state)
    out_shape = [jax.ShapeDtypeStruct((n, w), F32)]
    out_specs = [pl.BlockSpec((rows, w), lambda b: (b, 0))]
    if emit_state:
        out_shape += [jax.ShapeDtypeStruct((batch, DEPTH, 2, H, DH, DH), F32),
                      jax.ShapeDtypeStruct((batch, DEPTH, 2, H, DH), F32),
                      jax.ShapeDtypeStruct((batch, DEPTH, 2, H, LANES), F32)]
        out_specs += new_specs
        if prev_state is not None:
            aliases = {len(args) + k: 1 + k for k in range(3)}
            in_specs += [pl.BlockSpec(memory_space=pl.ANY)] * 3
            args += list(prev_state)
    scratch = [pltpu.VMEM((ncb, H, DH, CHUNK), F32),
               pltpu.VMEM((ncb, H, DH, CHUNK), F32),
               pltpu.VMEM((2 * seqs, H, DH, DH), F32),
               pltpu.VMEM((2 * seqs, H, 1, DH), F32),
               pltpu.VMEM((2 * seqs, H, 1, CHUNK), F32)]
    return pl.pallas_call(
        functools.partial(_mlstm_kernel, seq_len=seq_len, seqs=seqs, layer=layer, has_init=has_init,
                          emit_state=emit_state, n_prev=len(aliases)),
        out_shape=tuple(out_shape),
        grid=(batch // seqs,),
        in_specs=in_specs,
        out_specs=tuple(out_specs),
        scratch_shapes=scratch,
        input_output_aliases=aliases,
        compiler_params=_cparams(1),
        name="mlstm_init" if has_init else "mlstm_zero",
    )(*args)


def _dft_tables(seq_len):
    gw = FOURIER_GW
    kc = np.arange(gw)
    ang_c = 2.0 * np.pi * ((kc[:, None] * kc[None, :]) % gw) / gw
    w_chan = np.concatenate([np.cos(ang_c), np.sin(ang_c)], axis=1)
    kt = np.arange(seq_len)
    ang_t = 2.0 * np.pi * ((kt[:, None] * kt[None, :]) % seq_len) / seq_len
    scale = 1.0 / np.sqrt(seq_len * gw)
    return (jnp.asarray(w_chan, dtype=F32).astype(BF16),
            jnp.asarray(np.cos(ang_t) * scale, dtype=F32).astype(BF16),
            jnp.asarray(-np.sin(ang_t) * scale, dtype=F32).astype(BF16))


def _fourier_kernel(x_ref, wc_ref, ct_ref, st_ref, out_ref, *, seq_len):
    gw = FOURIER_GW
    for g in range(FOURIER_G):
        cols = slice(g * gw, (g + 1) * gw)
        a = _dot(x_ref[:, cols].astype(BF16), wc_ref[...])
        for r in range(0, x_ref.shape[0], seq_len):
            a_c = a[r:r + seq_len, :gw].astype(BF16)
            a_s = a[r:r + seq_len, gw:].astype(BF16)
            y = _dot(ct_ref[...], a_c) + _dot(st_ref[...], a_s)
            out_ref[r:r + seq_len, cols] = y.astype(BF16)


def _fourier(xf, batch, seq_len):
    w_chan, c_t, s_t = _dft_tables(seq_len)
    n = batch * seq_len
    tm = max(seq_len, 1024)
    return pl.pallas_call(
        functools.partial(_fourier_kernel, seq_len=seq_len),
        out_shape=jax.ShapeDtypeStruct((n, FOURIER_W), BF16),
        grid=(n // tm,),
        in_specs=[pl.BlockSpec((tm, FOURIER_W), lambda b: (b, 0)),
                  _resident(w_chan.shape), _resident(c_t.shape), _resident(s_t.shape)],
        out_specs=pl.BlockSpec((tm, FOURIER_W), lambda b: (b, 0)),
        compiler_params=_cparams(1),
        name="fourier",
    )(xf, w_chan, c_t, s_t)


def _merge_kernel(*refs, row_len):
    (x_ref, sh_ref, sc_ref, gt_ref, hn_ref, o_ref, hf_ref), rest = refs[:7], refs[7:]
    w_refs, b_refs = rest[:7], rest[7:14]
    (ng_ref, cw_ref, cb_ref, wm_ref, wf_ref, wc_ref, wo_ref, lg_ref, lb_ref, out_ref,
     wal_s, bal_s) = rest[14:]

    @pl.when(pl.program_id(0) == 0)
    def _():
        for k in range(6):
            wal_s[k] = _shifted(w_refs[k][...], w_refs[k + 1][:, :LANES])
            bal_s[k] = _shifted(b_refs[k][...], b_refs[k + 1][:, :LANES])

    x = x_ref[...]
    tm = x.shape[0]
    u = (x * (1.0 + sc_ref[...]) + sh_ref[...]).astype(BF16)

    def seg(k):
        return _dot(u, wal_s[k]) + bal_s[k]

    uc = seg(1) * seg(2)
    pos = lax.broadcasted_iota(jnp.int32, (tm, 1), 0) % row_len
    prev = jnp.where(pos == 0, 0.0, pltpu.roll(uc, 1, 0))
    nxt = jnp.where(pos == row_len - 1, 0.0, pltpu.roll(uc, tm - 1, 0))
    yc = prev * cw_ref[0:1, :] + uc * cw_ref[1:2, :] + nxt * cw_ref[2:3, :] + cb_ref[...]
    h_c = (seg(0) * yc).astype(BF16)
    h_m = (_sigmoid(o_ref[...]) * hn_ref[...] * ng_ref[...]).astype(BF16)
    merged = (_sigmoid(seg(3)) * _dot(h_m, wm_ref[...])
              + _sigmoid(seg(4)) * _dot(hf_ref[...], wf_ref[...])
              + _sigmoid(seg(5)) * _dot(h_c, wc_ref[...]))
    y = _dot(merged.astype(BF16), wo_ref[...])
    out_ref[...] = _layer_norm(ALPHA * x + gt_ref[...] * y, lg_ref[...], lb_ref[...])


def _merge(x2d, mods, seq_len, row_len, layer, h_n, o, h_f, wts):
    n = x2d.shape[0]
    d = D_MODEL
    tm = 256
    per_mod = seq_len // tm
    mrow = (lambda i: i // per_mod) if mods.shape[0] > 1 else (lambda i: 0)
    tile = lambda i: (i, 0)
    wspec = functools.partial(_layer_window, layer)
    first = (OFF_CONV - WIN_PHASE) // d
    wins = [wspec(d, d, first + k) for k in range(6)] + [wspec(d, LANES, 0)]
    bwins = [wspec(1, d, first + k) for k in range(6)] + [wspec(1, LANES, 0)]
    return pl.pallas_call(
        functools.partial(_merge_kernel, row_len=row_len),
        out_shape=jax.ShapeDtypeStruct((n, d), F32),
        grid=(n // tm,),
        in_specs=[pl.BlockSpec((tm, d), tile),
                  pl.BlockSpec((None, 1, d), lambda i: (mrow(i), 0, 0)),
                  pl.BlockSpec((None, 1, d), lambda i: (mrow(i), 0, 1)),
                  pl.BlockSpec((None, 1, d), lambda i: (mrow(i), 0, 2)),
                  pl.BlockSpec((tm, d), tile), pl.BlockSpec((tm, d), tile), pl.BlockSpec((tm, d), tile)]
                 + wins + bwins
                 + [wspec(1, d, 0),
                    wspec(3, d, 0), wspec(1, d, 0), wspec(d, d, 0), wspec(d, d, 0), wspec(d, d, 0),
                    wspec(d, d, 0),
                    pl.BlockSpec((None, 1, d), lambda i: (2 * layer, 0, 0)),
                    pl.BlockSpec((None, 1, d), lambda i: (2 * layer, 0, 0))],
        out_specs=pl.BlockSpec((tm, d), tile),
        scratch_shapes=[pltpu.VMEM((6, d, d), BF16), pltpu.VMEM((6, 1, d), F32)],
        compiler_params=_cparams(1),
        name="merge",
    )(x2d, mods, mods, mods, h_n, o, h_f, *([wts['w_in']] * 6), wts['w_tail'], *([wts['b_in']] * 6),
      wts['b_tail'], wts['norm_g'], wts['conv_w'], wts['conv_b'], wts['w_m'], wts['w_f'], wts['w_c'],
      wts['w_o'], wts['ln_g'], wts['ln_b'])


def _ffn_kernel(x_ref, sh_ref, sc_ref, gt_ref, wgu_ref, wd_ref, lg_ref, lb_ref, out_ref):
    x = x_ref[...]
    u = (x * (1.0 + sc_ref[...]) + sh_ref[...]).astype(BF16)
    a = _dot(u, wgu_ref[:, :D_FF])
    b = _dot(u, wgu_ref[:, D_FF:])
    hidden = (a * _sigmoid(a) * b).astype(BF16)
    y = _dot(hidden, wd_ref[...])
    out_ref[...] = _layer_norm(ALPHA * x + gt_ref[...] * y, lg_ref[...], lb_ref[...])


def _ffn(x2d, mods, seq_len, layer, w_gu_bf, w_d_bf, ln_g, ln_b):
    n = x2d.shape[0]
    d = D_MODEL
    tm = 512
    per_mod = seq_len // tm
    mrow = (lambda i: i // per_mod) if mods.shape[0] > 1 else (lambda i: 0)
    tile = lambda i: (i, 0)
    wspec = functools.partial(_layer_window, layer)
    return pl.pallas_call(
        _ffn_kernel,
        out_shape=jax.ShapeDtypeStruct((n, d), F32),
        grid=(n // tm,),
        in_specs=[pl.BlockSpec((tm, d), tile),
                  pl.BlockSpec((None, 1, d), lambda i: (mrow(i), 0, 3)),
                  pl.BlockSpec((None, 1, d), lambda i: (mrow(i), 0, 4)),
                  pl.BlockSpec((None, 1, d), lambda i: (mrow(i), 0, 5)),
                  wspec(d, 2 * D_FF, 0), wspec(D_FF, d, 0),
                  pl.BlockSpec((None, 1, d), lambda i: (2 * layer + 1, 0, 0)),
                  pl.BlockSpec((None, 1, d), lambda i: (2 * layer + 1, 0, 0))],
        out_specs=pl.BlockSpec((tm, d), tile),
        compiler_params=_cparams(1),
        name="ffn",
    )(x2d, mods, mods, mods, w_gu_bf, w_d_bf, ln_g, ln_b)


def _prepared_weights(w_in, b_in, mlstm_norm_g, conv_w, conv_b, w_br_mlstm, w_br_fourier, w_br_conv,
                      w_out, ln_g, ln_b, w_gate_up, w_down):
    tail = N_IN - WIN_PHASE
    pad_tail = ((0, 0), (0, 0), (0, LANES - WIN_PHASE))
    b_in3 = b_in[:, None, :]
    return dict(
        w_in=_to_bf16(w_in), b_in=b_in3,
        w_tail=jnp.pad(w_in[:, :, tail:], pad_tail).astype(BF16), b_tail=jnp.pad(b_in3[:, :, tail:], pad_tail),
        norm_g=mlstm_norm_g[:, None, :], conv_w=conv_w, conv_b=conv_b[:, None, :],
        w_m=w_br_mlstm.astype(BF16), w_f=w_br_fourier.astype(BF16), w_c=w_br_conv.astype(BF16),
        w_o=w_out.astype(BF16),
        ln_g=ln_g.reshape(2 * DEPTH, 1, D_MODEL), ln_b=ln_b.reshape(2 * DEPTH, 1, D_MODEL),
        w_gu=w_gate_up.astype(BF16), w_d=w_down.astype(BF16))


def _trunk_layer(x2d, mods, batch, seq_len, row_len, layer, wts, init_state, prev_state):
    qk, vt, o, xf, tabs, wt = _inproj(x2d, mods, seq_len, layer, wts['w_in'], wts['b_in'])
    res = _mlstm(qk, vt, tabs, wt, batch, seq_len, layer, init_state, prev_state)
    h_f = _fourier(xf, batch, seq_len)
    x1 = _merge(x2d, mods, seq_len, row_len, layer, res[0], o, h_f, wts)
    x2 = _ffn(x1, mods, seq_len, layer, wts['w_gu'], wts['w_d'], wts['ln_g'], wts['ln_b'])
    return x2, res[1:]


def kernel(x_prompt, x_sample, state_C, state_n, state_m, c, c_ctx, w_ada, b_ada, w_in, b_in,
           mlstm_norm_g, conv_w, conv_b, w_br_mlstm, w_br_fourier, w_br_conv, w_out, ln_g, ln_b,
           w_gate_up, w_down):
    bp, tp, d = x_prompt.shape
    bs, ts, _ = x_sample.shape
    cond = jnp.concatenate([c_ctx[None, :], c, jnp.zeros((8 - 1 - bs, d), F32)], axis=0)
    mods = _ada_mods(cond, w_ada, b_ada)
    wts = _prepared_weights(w_in, b_in, mlstm_norm_g, conv_w, conv_b, w_br_mlstm, w_br_fourier,
                            w_br_conv, w_out, ln_g, ln_b, w_gate_up, w_down)

    xp = x_prompt.reshape(bp * tp, d)
    xs = _add_pos(x_sample, _grid_pos_embed(ts // GRID_W, x_sample.dtype)).reshape(bs * ts, d)
    new_state = None
    for l in range(DEPTH):
        mods_ctx = mods[l, 0:1].reshape(1, 1, 6 * d)
        mods_smp = mods[l, 1:1 + bs].reshape(bs, 1, 6 * d)
        xp, new_state = _trunk_layer(xp, mods_ctx, bp, tp, tp, l, wts, None, new_state)
        xs, _ = _trunk_layer(xs, mods_smp, bs, ts, GRID_W, l, wts, (state_C, state_n, state_m), None)
    c_new, n_new, m_new = new_state
    return (xp.reshape(bp, tp, d), xs.reshape(bs, ts, d), c_new, n_new, m_new[..., 0])
```

```python
import functools

import numpy as np
import jax
import jax.numpy as jnp
from jax import lax
from jax.experimental import pallas as pl
from jax.experimental.pallas import tpu as pltpu

D_MODEL = 1024
DEPTH = 2
GRID_W = 64
MLSTM_H = 4
MLSTM_DH = 256
MLSTM_W = MLSTM_H * MLSTM_DH
CHUNK = 256
GATE_ROWS = 4 * MLSTM_H
FOURIER_G = 4
FOURIER_GW = 256
FOURIER_W = FOURIER_G * FOURIER_GW
CONV_W = 1024
D_FF = -(-8 * D_MODEL // (3 * 256)) * 256
ALPHA = (2 * DEPTH) ** 0.25
LN_EPS = 1e-5
POS_BASE = 10000.0

OFF_GATES = 4 * MLSTM_W
OFF_FOURIER = OFF_GATES + 4 * MLSTM_H
OFF_CONV = OFF_FOURIER + FOURIER_W
OFF_MERGE = OFF_CONV + 3 * CONV_W
N_IN = OFF_MERGE + 3 * D_MODEL

LANES = 128
WIN_PHASE = OFF_FOURIER % LANES
VMEM_LIMIT = 56 * 1024 * 1024

BF16 = jnp.bfloat16
F32 = jnp.float32


def _cparams(n_axes):
    return pltpu.CompilerParams(dimension_semantics=("arbitrary",) * n_axes,
                                vmem_limit_bytes=VMEM_LIMIT)


def _resident(shape):
    zeros = (0,) * len(shape)
    return pl.BlockSpec(shape, lambda *_: zeros, pipeline_mode=pl.Buffered(1))


def _sigmoid(x):
    return 1.0 / (1.0 + jnp.exp(-x))


def _log_sigmoid(x):
    return jnp.minimum(x, 0.0) - jnp.log1p(jnp.exp(-jnp.abs(x)))


def _layer_norm(x, g, b):
    mu = jnp.mean(x, axis=-1, keepdims=True)
    xc = x - mu
    var = jnp.mean(xc * xc, axis=-1, keepdims=True)
    return xc * lax.rsqrt(var + LN_EPS) * g + b


def _dot(a, b):
    return jnp.dot(a, b, preferred_element_type=F32)


def _ada_kernel(cond_ref, w_ref, b_ref, out_ref):
    cond = cond_ref[...]
    act = (cond * _sigmoid(cond)).astype(BF16)
    out_ref[...] = _dot(act, w_ref[...].astype(BF16)) + b_ref[...]


def _ada_mods(cond, w_ada, b_ada):
    tn = 1536
    nb = 6 * D_MODEL // tn
    return pl.pallas_call(
        _ada_kernel,
        out_shape=jax.ShapeDtypeStruct((DEPTH, 8, 6 * D_MODEL), F32),
        grid=(DEPTH, nb),
        in_specs=[pl.BlockSpec((8, D_MODEL), lambda l, j: (0, 0)),
                  pl.BlockSpec((None, D_MODEL, tn), lambda l, j: (l, 0, j)),
                  pl.BlockSpec((None, 1, tn), lambda l, j: (l, 0, j))],
        out_specs=pl.BlockSpec((None, 8, tn), lambda l, j: (l, 0, j)),
        compiler_params=_cparams(2),
        name="ada_mods",
    )(cond, w_ada, b_ada.reshape(DEPTH, 1, 6 * D_MODEL))


def _add_pos_kernel(x_ref, pos_ref, out_ref):
    out_ref[...] = x_ref[...] + pos_ref[...]


def _add_pos(x, pos):
    b, t, d = x.shape
    return pl.pallas_call(
        _add_pos_kernel,
        out_shape=jax.ShapeDtypeStruct(x.shape, x.dtype),
        grid=(b,),
        in_specs=[pl.BlockSpec((None, t, d), lambda i: (i, 0, 0)),
                  pl.BlockSpec((t, d), lambda i: (0, 0))],
        out_specs=pl.BlockSpec((None, t, d), lambda i: (i, 0, 0)),
        compiler_params=_cparams(1),
        name="add_pos",
    )(x, pos)


def _grid_pos_embed(rows, dtype):
    quarter = D_MODEL // 4
    freqs = (POS_BASE ** (-np.arange(quarter, dtype=np.float32) / quarter)).astype(np.float32)
    r = np.repeat(np.arange(rows, dtype=np.float32), GRID_W)[:, None] * freqs
    col = np.tile(np.arange(GRID_W, dtype=np.float32), rows)[:, None] * freqs
    return jnp.asarray(np.concatenate([np.sin(r), np.cos(r), np.sin(col), np.cos(col)], axis=-1), dtype)


def _shifted(a, b):
    return jnp.concatenate([a[:, WIN_PHASE:], b[:, :WIN_PHASE]], axis=1)


def _gate_tables(g):
    GR, H = GATE_ROWS, MLSTM_H
    g_t = g.T[:GR, :]
    lf = _log_sigmoid(g_t)
    fwd = lax.broadcasted_iota(jnp.int32, (GR, CHUNK), 0) < 2 * H
    lane = lax.broadcasted_iota(jnp.int32, (GR, CHUNK), 1)

    def scans(x, op, fill):
        left, right = x, x
        k = 1
        while k < CHUNK:
            left = op(left, jnp.where(lane >= k, pltpu.roll(left, k, 1), fill))
            right = op(right, jnp.where(lane < CHUNK - k, pltpu.roll(right, CHUNK - k, 1), fill))
            k *= 2
        return left, right

    pre, suf = scans(lf, jnp.add, 0.0)
    cum = jnp.where(fwd, pre, suf)
    tot = pre + suf - lf
    w = pltpu.roll(g_t, H, 0) - cum
    pm, sm = scans(w, jnp.maximum, -jnp.inf)
    cmax = jnp.where(fwd, pm, sm)
    wmax = jnp.broadcast_to(jnp.max(w, axis=1, keepdims=True), w.shape)
    wt = jnp.concatenate([w, jnp.zeros((LANES - GR, CHUNK), F32)], axis=0).T
    return (cum, tot, cmax, wmax), wt


def _inproj_kernel(x_ref, sh_ref, sc_ref, wqk_ref, wv_ref, wo_ref, wf_ref, wft_ref,
                   bqk_ref, bv_ref, bo_ref, bf_ref, bft_ref, qk_ref, vt_ref, o_ref, f_ref, tab_ref,
                   wt_ref, wfa_s, bfa_s):
    @pl.when(pl.program_id(0) == 0)
    def _():
        wfa_s[...] = _shifted(wf_ref[...], wft_ref[...])
        bfa_s[...] = _shifted(bf_ref[...], bft_ref[...])

    u = (x_ref[...] * (1.0 + sc_ref[...]) + sh_ref[...]).astype(BF16)
    d = D_MODEL
    g = _dot(u, wf_ref[:, :LANES]) + bf_ref[:, :LANES]
    for c in range(tab_ref.shape[0]):
        rows = slice(c * CHUNK, (c + 1) * CHUNK)
        tabs, wt = _gate_tables(g[rows, :])
        wt_ref[rows, :] = wt
        for k, tab in enumerate(tabs):
            tab_ref[c, k] = tab
    qk_ref[:, :d] = (_dot(u, wqk_ref[:, :d]) + bqk_ref[:, :d]).astype(BF16)
    qk_ref[:, d:] = ((_dot(u, wqk_ref[:, d:]) + bqk_ref[:, d:]) * (MLSTM_DH ** -0.5)).astype(BF16)
    v_t = (_dot(u, wv_ref[...]) + bv_ref[...]).T
    for c in range(vt_ref.shape[0]):
        vt_ref[c] = v_t[:, c * CHUNK:(c + 1) * CHUNK].astype(BF16)
    o_ref[...] = _dot(u, wo_ref[...]) + bo_ref[...]
    f_ref[...] = _dot(u, wfa_s[...]) + bfa_s[...]


def _layer_window(layer, rows, width, k):
    return pl.BlockSpec((None, rows, width), lambda *_: (layer, 0, k), pipeline_mode=pl.Buffered(1))


def _inproj(x2d, mods, seq_len, layer, w_in_bf, b_in_p):
    n = x2d.shape[0]
    tm = 512
    per_mod = max(seq_len // tm, 1) if mods.shape[0] > 1 else n
    mrow = (lambda i: i // per_mod) if mods.shape[0] > 1 else (lambda i: 0)
    d = D_MODEL
    wspec = functools.partial(_layer_window, layer)
    wins = [(2 * d, 0), (d, 2), (d, 3), (d, OFF_GATES // d), (LANES, (OFF_GATES + d) // LANES)]
    return pl.pallas_call(
        _inproj_kernel,
        out_shape=(jax.ShapeDtypeStruct((n, 2 * d), BF16),
                   jax.ShapeDtypeStruct((n // CHUNK, d, CHUNK), BF16),
                   jax.ShapeDtypeStruct((n, d), F32),
                   jax.ShapeDtypeStruct((n, d), F32),
                   jax.ShapeDtypeStruct((n // CHUNK, 4, GATE_ROWS, CHUNK), F32),
                   jax.ShapeDtypeStruct((n, LANES), F32)),
        grid=(n // tm,),
        in_specs=[pl.BlockSpec((tm, d), lambda i: (i, 0)),
                  pl.BlockSpec((None, 1, d), lambda i: (mrow(i), 0, 0)),
                  pl.BlockSpec((None, 1, d), lambda i: (mrow(i), 0, 1))]
                 + [wspec(d, w, k) for w, k in wins] + [wspec(1, w, k) for w, k in wins],
        out_specs=(pl.BlockSpec((tm, 2 * d), lambda i: (i, 0)),
                   pl.BlockSpec((tm // CHUNK, d, CHUNK), lambda i: (i, 0, 0)),
                   pl.BlockSpec((tm, d), lambda i: (i, 0)),
                   pl.BlockSpec((tm, d), lambda i: (i, 0)),
                   pl.BlockSpec((tm // CHUNK, 4, GATE_ROWS, CHUNK), lambda i: (i, 0, 0, 0)),
                   pl.BlockSpec((tm, LANES), lambda i: (i, 0))),
        scratch_shapes=[pltpu.VMEM((d, d), BF16), pltpu.VMEM((1, d), F32)],
        compiler_params=_cparams(1),
        name="inproj",
    )(x2d, mods, mods, *([w_in_bf] * 5), *([b_in_p] * 5))


def _mlstm_kernel(*refs, seq_len, seqs, layer, has_init, emit_state, n_prev):
    it = iter(refs)
    qk_ref, vt_ref, tab_ref, wt_ref = (next(it) for _ in range(4))
    if has_init:
        c0_ref, n0_ref, m0_ref = (next(it) for _ in range(3))
    for _ in range(n_prev):
        next(it)
    hm_ref = next(it)
    if emit_state:
        cn_ref, nn_ref, mn_ref = (next(it) for _ in range(3))
    hf_s, hb_s, ct_s, n_s, m_s = it

    nc = seq_len // CHUNK
    ncb = nc * seqs
    H, DH, W = MLSTM_H, MLSTM_DH, MLSTM_W
    HALF = CHUNK // 2
    assert CHUNK == DH
    ri = lax.broadcasted_iota(jnp.int32, (HALF, HALF), 0)
    ci = lax.broadcasted_iota(jnp.int32, (HALF, HALF), 1)
    tri_mask = {False: ri <= ci, True: ri >= ci}

    if has_init:
        for d in range(2):
            for h in range(H):
                ct_s[d, h] = c0_ref[d, h].T
                n_s[d, h] = n0_ref[d, h:h + 1, :]
                m_s[d, h] = jnp.full((1, CHUNK), m0_ref[pl.program_id(0), layer, d, h], F32)
    else:
        m_s[...] = jnp.zeros_like(m_s)
    ones_rows = jnp.ones((16, CHUNK), BF16)

    def chain(c, h, backward, use_state, update_state, q=0):
        static = isinstance(c, int)
        r0 = c * CHUNK if static else pl.multiple_of(c * CHUNK, CHUNK)
        rows = pl.ds(r0, CHUNK)
        d = 2 * q + (1 if backward else 0)
        hc = slice(h * DH, (h + 1) * DH)
        gf = (3 if backward else 1) * H + h
        qc = qk_ref[rows, hc]
        kc = qk_ref[rows, W + h * DH:W + (h + 1) * DH]
        vt1 = jnp.concatenate([vt_ref[c, hc, :], ones_rows], axis=0)
        grow = pl.ds(gf, 1)
        brow = tab_ref[c, 0, grow, :]
        g_tot = tab_ref[c, 1, grow, :]
        cmax = tab_ref[c, 2, grow, :]
        wmax = tab_ref[c, 3, grow, :]
        wb = jnp.broadcast_to(wt_ref[rows, gf:gf + 1], (CHUNK, CHUNK))
        m_prev = m_s[d, h]

        inter = brow + m_prev
        m_t = jnp.maximum(inter, brow + cmax)
        e_row = brow - m_t
        nt_dims = (((1,), (1,)), ((), ()))
        if use_state:
            n_prev = n_s[d, h]
            lhs = jnp.concatenate([kc, jnp.broadcast_to(n_prev.astype(BF16), (16, DH)),
                                   ct_s[d, h].astype(BF16)], axis=0)
            r = lax.dot_general(lhs, qc, nt_dims, preferred_element_type=F32)
            qk_t, nq, num_c = r[:CHUNK], r[CHUNK:CHUNK + 1], r[CHUNK + 16:]
        else:
            qk_t = lax.dot_general(kc, qc, nt_dims, preferred_element_type=F32)

        def quad(si, ti, masked):
            rs, cs = slice(si * HALF, (si + 1) * HALF), slice(ti * HALF, (ti + 1) * HALF)
            arg = wb[rs, cs] + e_row[:, cs]
            if masked:
                arg = jnp.where(tri_mask[backward], arg, -jnp.inf)
            return qk_t[rs, cs] * jnp.exp(arg)

        zero = jnp.zeros((HALF, HALF), F32)
        if backward:
            st = jnp.concatenate([jnp.concatenate([quad(0, 0, True), zero], axis=1),
                                  jnp.concatenate([quad(1, 0, False), quad(1, 1, True)], axis=1)], axis=0)
        else:
            st = jnp.concatenate([jnp.concatenate([quad(0, 0, True), quad(0, 1, False)], axis=1),
                                  jnp.concatenate([zero, quad(1, 1, True)], axis=1)], axis=0)
        r1 = _dot(vt1, st.astype(BF16))
        num, den = r1[:DH], r1[DH:DH + 1]
        if use_state:
            a = jnp.exp(inter - m_t)
            num = a * num_c + num
            den = a * nq + den
        hval = num * (1.0 / jnp.maximum(jnp.abs(den), jnp.exp(-m_t)))
        (hb_s if backward else hf_s)[c, h] = hval

        if update_state:
            m_new = jnp.maximum(g_tot + m_prev, wmax + g_tot)
            wk = jnp.exp(wb + (g_tot - m_new))
            kw = kc * wk.astype(BF16)
            r2 = _dot(vt1, kw)
            kv, nsum = r2[:DH], r2[DH:DH + 1]
            if use_state:
                a_c = jnp.exp(g_tot + m_prev - m_new)
                ct_s[d, h] = a_c * ct_s[d, h] + kv
                n_s[d, h] = a_c * n_prev + nsum
            else:
                ct_s[d, h] = kv
                n_s[d, h] = nsum
            m_s[d, h] = m_new

    def step(j, use_state, update_state):
        for q in range(seqs):
            for h in range(H):
                chain(q * nc + j, h, False, use_state, update_state, q)
                chain(q * nc + nc - 1 - j, h, True, use_state, update_state, q)

    first = 0
    if not has_init:
        step(0, False, True)
        first = 1
    last = nc if emit_state else nc - 1
    if last - first <= 2:
        for j in range(first, last):
            step(j, True, True)
    else:
        def body(j, carry):
            step(j, True, True)
            return carry
        lax.fori_loop(first, last, body, 0)
    if not emit_state:
        step(nc - 1, True, False)

    def finalize(c):
        static = isinstance(c, int)
        rows = pl.ds(c * CHUNK if static else pl.multiple_of(c * CHUNK, CHUNK), CHUNK)
        for h in range(H):
            hc = slice(h * DH, (h + 1) * DH)
            ht = hf_s[c, h] + hb_s[c, h]
            mu = jnp.mean(ht, axis=0, keepdims=True)
            cen = ht - mu
            var = jnp.mean(cen * cen, axis=0, keepdims=True)
            hm_ref[rows, hc] = (cen * lax.rsqrt(var + LN_EPS)).T

    if ncb <= 2:
        for c in range(ncb):
            finalize(c)
    else:
        def fbody(c, carry):
            finalize(c)
            return carry
        lax.fori_loop(0, ncb, fbody, 0)

    if emit_state:
        for q in range(seqs):
            for d in range(2):
                for h in range(H):
                    cn_ref[q, d, h] = ct_s[2 * q + d, h].T
                    nn_ref[q, d, h:h + 1, :] = n_s[2 * q + d, h]
                    mn_ref[q, d, h:h + 1, :] = m_s[2 * q + d, h][:, :LANES]


def _mlstm(qk, vt, tabs, wt, batch, seq_len, layer, init_state, prev_state):
    n = batch * seq_len
    w = MLSTM_W
    H, DH = MLSTM_H, MLSTM_DH
    has_init = init_state is not None
    emit_state = not has_init
    seqs = 2 if (seq_len == CHUNK and not has_init and batch % 2 == 0) else 1
    rows = seq_len * seqs
    ncb = rows // CHUNK
    in_specs = [pl.BlockSpec((rows, 2 * w), lambda b: (b, 0)),
                pl.BlockSpec((ncb, w, CHUNK), lambda b: (b, 0, 0)),
                pl.BlockSpec((ncb, 4, GATE_ROWS, CHUNK), lambda b: (b, 0, 0, 0)),
                pl.BlockSpec((rows, LANES), lambda b: (b, 0))]
    args = [qk, vt, tabs, wt]
    init_specs = [pl.BlockSpec((None, None, 2, H, DH, DH), lambda b: (b, layer, 0, 0, 0, 0)),
                  pl.BlockSpec((None, None, 2, H, DH), lambda b: (b, layer, 0, 0, 0))]
    new_specs = [pl.BlockSpec((seqs, None, 2, H, DH, DH), lambda b: (b, layer, 0, 0, 0, 0)),
                 pl.BlockSpec((seqs, None, 2, H, DH), lambda b: (b, layer, 0, 0, 0)),
                 pl.BlockSpec((seqs, None, 2, H, LANES), lambda b: (b, layer, 0, 0, 0))]
    aliases = {}
    if has_init:
        in_specs += init_specs + [pl.BlockSpec(memory_space=pltpu.SMEM)]
        args += list(init_state)
    out_shape = [jax.ShapeDtypeStruct((n, w), F32)]
    out_specs = [pl.BlockSpec((rows, w), lambda b: (b, 0))]
    if emit_state:
        out_shape += [jax.ShapeDtypeStruct((batch, DEPTH, 2, H, DH, DH), F32),
                      jax.ShapeDtypeStruct((batch, DEPTH, 2, H, DH), F32),
                      jax.ShapeDtypeStruct((batch, DEPTH, 2, H, LANES), F32)]
        out_specs += new_specs
        if prev_state is not None:
            aliases = {len(args) + k: 1 + k for k in range(3)}
            in_specs += [pl.BlockSpec(memory_space=pl.ANY)] * 3
            args += list(prev_state)
    scratch = [pltpu.VMEM((ncb, H, DH, CHUNK), F32),
               pltpu.VMEM((ncb, H, DH, CHUNK), F32),
               pltpu.VMEM((2 * seqs, H, DH, DH), F32),
               pltpu.VMEM((2 * seqs, H, 1, DH), F32),
               pltpu.VMEM((2 * seqs, H, 1, CHUNK), F32)]
    return pl.pallas_call(
        functools.partial(_mlstm_kernel, seq_len=seq_len, seqs=seqs, layer=layer, has_init=has_init,
                          emit_state=emit_state, n_prev=len(aliases)),
        out_shape=tuple(out_shape),
        grid=(batch // seqs,),
        in_specs=in_specs,
        out_specs=tuple(out_specs),
        scratch_shapes=scratch,
        input_output_aliases=aliases,
        compiler_params=_cparams(1),
        name="mlstm_init" if has_init else "mlstm_zero",
    )(*args)


def _dft_tables(seq_len):
    gw = FOURIER_GW
    kc = np.arange(gw)
    ang_c = 2.0 * np.pi * ((kc[:, None] * kc[None, :]) % gw) / gw
    w_chan = np.concatenate([np.cos(ang_c), np.sin(ang_c)], axis=1)
    kt = np.arange(seq_len)
    ang_t = 2.0 * np.pi * ((kt[:, None] * kt[None, :]) % seq_len) / seq_len
    scale = 1.0 / np.sqrt(seq_len * gw)
    return (jnp.asarray(w_chan, dtype=F32).astype(BF16),
            jnp.asarray(np.cos(ang_t) * scale, dtype=F32).astype(BF16),
            jnp.asarray(-np.sin(ang_t) * scale, dtype=F32).astype(BF16))


def _fourier_kernel(x_ref, wc_ref, ct_ref, st_ref, out_ref, *, seq_len):
    gw = FOURIER_GW
    for g in range(FOURIER_G):
        cols = slice(g * gw, (g + 1) * gw)
        a = _dot(x_ref[:, cols].astype(BF16), wc_ref[...])
        for r in range(0, x_ref.shape[0], seq_len):
            a_c = a[r:r + seq_len, :gw].astype(BF16)
            a_s = a[r:r + seq_len, gw:].astype(BF16)
            y = _dot(ct_ref[...], a_c) + _dot(st_ref[...], a_s)
            out_ref[r:r + seq_len, cols] = y.astype(BF16)


def _fourier(xf, batch, seq_len):
    w_chan, c_t, s_t = _dft_tables(seq_len)
    n = batch * seq_len
    tm = max(seq_len, 1024)
    return pl.pallas_call(
        functools.partial(_fourier_kernel, seq_len=seq_len),
        out_shape=jax.ShapeDtypeStruct((n, FOURIER_W), BF16),
        grid=(n // tm,),
        in_specs=[pl.BlockSpec((tm, FOURIER_W), lambda b: (b, 0)),
                  _resident(w_chan.shape), _resident(c_t.shape), _resident(s_t.shape)],
        out_specs=pl.BlockSpec((tm, FOURIER_W), lambda b: (b, 0)),
        compiler_params=_cparams(1),
        name="fourier",
    )(xf, w_chan, c_t, s_t)


def _merge_kernel(*refs, row_len):
    (x_ref, sh_ref, sc_ref, gt_ref, hn_ref, o_ref, hf_ref), rest = refs[:7], refs[7:]
    w_refs, b_refs = rest[:7], rest[7:14]
    (ng_ref, cw_ref, cb_ref, wm_ref, wf_ref, wc_ref, wo_ref, lg_ref, lb_ref, out_ref,
     wal_s, bal_s) = rest[14:]

    @pl.when(pl.program_id(0) == 0)
    def _():
        for k in range(6):
            wal_s[k] = _shifted(w_refs[k][...], w_refs[k + 1][:, :LANES])
            bal_s[k] = _shifted(b_refs[k][...], b_refs[k + 1][:, :LANES])

    x = x_ref[...]
    tm = x.shape[0]
    u = (x * (1.0 + sc_ref[...]) + sh_ref[...]).astype(BF16)

    def seg(k):
        return _dot(u, wal_s[k]) + bal_s[k]

    uc = seg(1) * seg(2)
    pos = lax.broadcasted_iota(jnp.int32, (tm, 1), 0) % row_len
    prev = jnp.where(pos == 0, 0.0, pltpu.roll(uc, 1, 0))
    nxt = jnp.where(pos == row_len - 1, 0.0, pltpu.roll(uc, tm - 1, 0))
    yc = prev * cw_ref[0:1, :] + uc * cw_ref[1:2, :] + nxt * cw_ref[2:3, :] + cb_ref[...]
    h_c = (seg(0) * yc).astype(BF16)
    h_m = (_sigmoid(o_ref[...]) * hn_ref[...] * ng_ref[...]).astype(BF16)
    merged = (_sigmoid(seg(3)) * _dot(h_m, wm_ref[...])
              + _sigmoid(seg(4)) * _dot(hf_ref[...], wf_ref[...])
              + _sigmoid(seg(5)) * _dot(h_c, wc_ref[...]))
    y = _dot(merged.astype(BF16), wo_ref[...])
    out_ref[...] = _layer_norm(ALPHA * x + gt_ref[...] * y, lg_ref[...], lb_ref[...])


def _merge(x2d, mods, seq_len, row_len, layer, h_n, o, h_f, wts):
    n = x2d.shape[0]
    d = D_MODEL
    tm = 256
    per_mod = seq_len // tm
    mrow = (lambda i: i // per_mod) if mods.shape[0] > 1 else (lambda i: 0)
    tile = lambda i: (i, 0)
    wspec = functools.partial(_layer_window, layer)
    first = (OFF_CONV - WIN_PHASE) // d
    wins = [wspec(d, d, first + k) for k in range(6)] + [wspec(d, LANES, 0)]
    bwins = [wspec(1, d, first + k) for k in range(6)] + [wspec(1, LANES, 0)]
    return pl.pallas_call(
        functools.partial(_merge_kernel, row_len=row_len),
        out_shape=jax.ShapeDtypeStruct((n, d), F32),
        grid=(n // tm,),
        in_specs=[pl.BlockSpec((tm, d), tile),
                  pl.BlockSpec((None, 1, d), lambda i: (mrow(i), 0, 0)),
                  pl.BlockSpec((None, 1, d), lambda i: (mrow(i), 0, 1)),
                  pl.BlockSpec((None, 1, d), lambda i: (mrow(i), 0, 2)),
                  pl.BlockSpec((tm, d), tile), pl.BlockSpec((tm, d), tile), pl.BlockSpec((tm, d), tile)]
                 + wins + bwins
                 + [wspec(1, d, 0),
                    wspec(3, d, 0), wspec(1, d, 0), wspec(d, d, 0), wspec(d, d, 0), wspec(d, d, 0),
                    wspec(d, d, 0),
                    pl.BlockSpec((None, 1, d), lambda i: (2 * layer, 0, 0)),
                    pl.BlockSpec((None, 1, d), lambda i: (2 * layer, 0, 0))],
        out_specs=pl.BlockSpec((tm, d), tile),
        scratch_shapes=[pltpu.VMEM((6, d, d), BF16), pltpu.VMEM((6, 1, d), F32)],
        compiler_params=_cparams(1),
        name="merge",
    )(x2d, mods, mods, mods, h_n, o, h_f, *([wts['w_in']] * 6), wts['w_tail'], *([wts['b_in']] * 6),
      wts['b_tail'], wts['norm_g'], wts['conv_w'], wts['conv_b'], wts['w_m'], wts['w_f'], wts['w_c'],
      wts['w_o'], wts['ln_g'], wts['ln_b'])


def _ffn_kernel(x_ref, sh_ref, sc_ref, gt_ref, wgu_ref, wd_ref, lg_ref, lb_ref, out_ref):
    half = x_ref.shape[0] // 2
    for r in range(2):
        rows = slice(r * half, (r + 1) * half)
        x = x_ref[rows, :]
        u = (x * (1.0 + sc_ref[...]) + sh_ref[...]).astype(BF16)
        a = _dot(u, wgu_ref[:, :D_FF])
        b = _dot(u, wgu_ref[:, D_FF:])
        hidden = (a * _sigmoid(a) * b).astype(BF16)
        y = _dot(hidden, wd_ref[...])
        out_ref[rows, :] = _layer_norm(ALPHA * x + gt_ref[...] * y, lg_ref[...], lb_ref[...])


def _ffn(x2d, mods, seq_len, layer, w_gu_bf, w_d_bf, ln_g, ln_b):
    n = x2d.shape[0]
    d = D_MODEL
    tm = 512
    per_mod = seq_len // tm
    mrow = (lambda i: i // per_mod) if mods.shape[0] > 1 else (lambda i: 0)
    tile = lambda i: (i, 0)
    wspec = functools.partial(_layer_window, layer)
    return pl.pallas_call(
        _ffn_kernel,
        out_shape=jax.ShapeDtypeStruct((n, d), F32),
        grid=(n // tm,),
        in_specs=[pl.BlockSpec((tm, d), tile),
                  pl.BlockSpec((None, 1, d), lambda i: (mrow(i), 0, 3)),
                  pl.BlockSpec((None, 1, d), lambda i: (mrow(i), 0, 4)),
                  pl.BlockSpec((None, 1, d), lambda i: (mrow(i), 0, 5)),
                  wspec(d, 2 * D_FF, 0), wspec(D_FF, d, 0),
                  pl.BlockSpec((None, 1, d), lambda i: (2 * layer + 1, 0, 0)),
                  pl.BlockSpec((None, 1, d), lambda i: (2 * layer + 1, 0, 0))],
        out_specs=pl.BlockSpec((tm, d), tile),
        compiler_params=_cparams(1),
        name="ffn",
    )(x2d, mods, mods, mods, w_gu_bf, w_d_bf, ln_g, ln_b)


def _prepared_weights(w_in, b_in, mlstm_norm_g, conv_w, conv_b, w_br_mlstm, w_br_fourier, w_br_conv,
                      w_out, ln_g, ln_b, w_gate_up, w_down):
    tail = N_IN - WIN_PHASE
    pad_tail = ((0, 0), (0, 0), (0, LANES - WIN_PHASE))
    b_in3 = b_in[:, None, :]
    return dict(
        w_in=w_in[:, :, :tail].astype(BF16), b_in=b_in3,
        w_tail=jnp.pad(w_in[:, :, tail:], pad_tail).astype(BF16), b_tail=jnp.pad(b_in3[:, :, tail:], pad_tail),
        norm_g=mlstm_norm_g[:, None, :], conv_w=conv_w, conv_b=conv_b[:, None, :],
        w_m=w_br_mlstm.astype(BF16), w_f=w_br_fourier.astype(BF16), w_c=w_br_conv.astype(BF16),
        w_o=w_out.astype(BF16),
        ln_g=ln_g.reshape(2 * DEPTH, 1, D_MODEL), ln_b=ln_b.reshape(2 * DEPTH, 1, D_MODEL),
        w_gu=w_gate_up.astype(BF16), w_d=w_down.astype(BF16))


def _trunk_layer(x2d, mods, batch, seq_len, row_len, layer, wts, init_state, prev_state):
    qk, vt, o, xf, tabs, wt = _inproj(x2d, mods, seq_len, layer, wts['w_in'], wts['b_in'])
    res = _mlstm(qk, vt, tabs, wt, batch, seq_len, layer, init_state, prev_state)
    h_f = _fourier(xf, batch, seq_len)
    x1 = _merge(x2d, mods, seq_len, row_len, layer, res[0], o, h_f, wts)
    x2 = _ffn(x1, mods, seq_len, layer, wts['w_gu'], wts['w_d'], wts['ln_g'], wts['ln_b'])
    return x2, res[1:]


def kernel(x_prompt, x_sample, state_C, state_n, state_m, c, c_ctx, w_ada, b_ada, w_in, b_in,
           mlstm_norm_g, conv_w, conv_b, w_br_mlstm, w_br_fourier, w_br_conv, w_out, ln_g, ln_b,
           w_gate_up, w_down):
    bp, tp, d = x_prompt.shape
    bs, ts, _ = x_sample.shape
    cond = jnp.concatenate([c_ctx[None, :], c, jnp.zeros((8 - 1 - bs, d), F32)], axis=0)
    mods = _ada_mods(cond, w_ada, b_ada)
    wts = _prepared_weights(w_in, b_in, mlstm_norm_g, conv_w, conv_b, w_br_mlstm, w_br_fourier,
                            w_br_conv, w_out, ln_g, ln_b, w_gate_up, w_down)

    xp = x_prompt.reshape(bp * tp, d)
    xs = _add_pos(x_sample, _grid_pos_embed(ts // GRID_W, x_sample.dtype)).reshape(bs * ts, d)
    new_state = None
    for l in range(DEPTH):
        mods_ctx = mods[l, 0:1].reshape(1, 1, 6 * d)
        mods_smp = mods[l, 1:1 + bs].reshape(bs, 1, 6 * d)
        xp, new_state = _trunk_layer(xp, mods_ctx, bp, tp, tp, l, wts, None, new_state)
        xs, _ = _trunk_layer(xs, mods_smp, bs, ts, GRID_W, l, wts, (state_C, state_n, state_m), None)
    c_new, n_new, m_new = new_state
    return (xp.reshape(bp, tp, d), xs.reshape(bs, ts, d), c_new, n_new, m_new[..., 0])
```

```python
import functools

import numpy as np
import jax
import jax.numpy as jnp
from jax import lax
from jax.experimental import pallas as pl
from jax.experimental.pallas import tpu as pltpu

D_MODEL = 1024
DEPTH = 2
GRID_W = 64
MLSTM_H = 4
MLSTM_DH = 256
MLSTM_W = MLSTM_H * MLSTM_DH
CHUNK = 256
GATE_ROWS = 4 * MLSTM_H
FOURIER_G = 4
FOURIER_GW = 256
FOURIER_W = FOURIER_G * FOURIER_GW
CONV_W = 1024
D_FF = -(-8 * D_MODEL // (3 * 256)) * 256
ALPHA = (2 * DEPTH) ** 0.25
LN_EPS = 1e-5
POS_BASE = 10000.0

OFF_GATES = 4 * MLSTM_W
OFF_FOURIER = OFF_GATES + 4 * MLSTM_H
OFF_CONV = OFF_FOURIER + FOURIER_W
OFF_MERGE = OFF_CONV + 3 * CONV_W
N_IN = OFF_MERGE + 3 * D_MODEL

LANES = 128
WIN_PHASE = OFF_FOURIER % LANES
VMEM_LIMIT = 56 * 1024 * 1024

BF16 = jnp.bfloat16
F32 = jnp.float32


def _cparams(n_axes):
    return pltpu.CompilerParams(dimension_semantics=("arbitrary",) * n_axes,
                                vmem_limit_bytes=VMEM_LIMIT)


def _resident(shape):
    zeros = (0,) * len(shape)
    return pl.BlockSpec(shape, lambda *_: zeros, pipeline_mode=pl.Buffered(1))


def _sigmoid(x):
    return 1.0 / (1.0 + jnp.exp(-x))


def _log_sigmoid(x):
    return jnp.minimum(x, 0.0) - jnp.log1p(jnp.exp(-jnp.abs(x)))


def _layer_norm(x, g, b):
    mu = jnp.mean(x, axis=-1, keepdims=True)
    xc = x - mu
    var = jnp.mean(xc * xc, axis=-1, keepdims=True)
    return xc * lax.rsqrt(var + LN_EPS) * g + b


def _dot(a, b):
    return jnp.dot(a, b, preferred_element_type=F32)


def _ada_kernel(cond_ref, w_ref, b_ref, out_ref):
    cond = cond_ref[...]
    act = (cond * _sigmoid(cond)).astype(BF16)
    out_ref[...] = _dot(act, w_ref[...].astype(BF16)) + b_ref[...]


def _ada_mods(cond, w_ada, b_ada):
    tn = 1536
    nb = 6 * D_MODEL // tn
    return pl.pallas_call(
        _ada_kernel,
        out_shape=jax.ShapeDtypeStruct((DEPTH, 8, 6 * D_MODEL), F32),
        grid=(DEPTH, nb),
        in_specs=[pl.BlockSpec((8, D_MODEL), lambda l, j: (0, 0)),
                  pl.BlockSpec((None, D_MODEL, tn), lambda l, j: (l, 0, j)),
                  pl.BlockSpec((None, 1, tn), lambda l, j: (l, 0, j))],
        out_specs=pl.BlockSpec((None, 8, tn), lambda l, j: (l, 0, j)),
        compiler_params=_cparams(2),
        name="ada_mods",
    )(cond, w_ada, b_ada.reshape(DEPTH, 1, 6 * D_MODEL))


def _add_pos_kernel(x_ref, pos_ref, out_ref):
    out_ref[...] = x_ref[...] + pos_ref[...]


def _add_pos(x, pos):
    b, t, d = x.shape
    return pl.pallas_call(
        _add_pos_kernel,
        out_shape=jax.ShapeDtypeStruct(x.shape, x.dtype),
        grid=(b,),
        in_specs=[pl.BlockSpec((None, t, d), lambda i: (i, 0, 0)),
                  pl.BlockSpec((t, d), lambda i: (0, 0))],
        out_specs=pl.BlockSpec((None, t, d), lambda i: (i, 0, 0)),
        compiler_params=_cparams(1),
        name="add_pos",
    )(x, pos)


def _grid_pos_embed(rows, dtype):
    quarter = D_MODEL // 4
    freqs = (POS_BASE ** (-np.arange(quarter, dtype=np.float32) / quarter)).astype(np.float32)
    r = np.repeat(np.arange(rows, dtype=np.float32), GRID_W)[:, None] * freqs
    col = np.tile(np.arange(GRID_W, dtype=np.float32), rows)[:, None] * freqs
    return jnp.asarray(np.concatenate([np.sin(r), np.cos(r), np.sin(col), np.cos(col)], axis=-1), dtype)


def _cast_cols_kernel(w_ref, out_ref):
    out_ref[...] = w_ref[...].astype(BF16)


def _cast_cols(w, width, first, count):
    depth, rows, _ = w.shape
    return pl.pallas_call(
        _cast_cols_kernel,
        out_shape=jax.ShapeDtypeStruct((depth, rows, count * width), BF16),
        grid=(depth, count),
        in_specs=[pl.BlockSpec((None, rows, width), lambda l, j: (l, 0, first + j))],
        out_specs=pl.BlockSpec((None, rows, width), lambda l, j: (l, 0, j)),
        compiler_params=_cparams(2),
        name="cast_cols",
    )(w)


def _cast_shifted_kernel(a_ref, b_ref, out_ref):
    out_ref[...] = jnp.concatenate([a_ref[:, WIN_PHASE:], b_ref[:, :WIN_PHASE]], axis=1).astype(BF16)


def _cast_shifted(w, first, count):
    depth, rows, _ = w.shape
    d = D_MODEL
    per = d // LANES
    return pl.pallas_call(
        _cast_shifted_kernel,
        out_shape=jax.ShapeDtypeStruct((depth, count, rows, d), BF16),
        grid=(depth, count),
        in_specs=[pl.BlockSpec((None, rows, d), lambda l, j: (l, 0, first + j)),
                  pl.BlockSpec((None, rows, LANES), lambda l, j: (l, 0, (first + j + 1) * per))],
        out_specs=pl.BlockSpec((None, None, rows, d), lambda l, j: (l, j, 0, 0)),
        compiler_params=_cparams(2),
        name="cast_shifted",
    )(w, w)


def _gate_tables(g):
    GR, H = GATE_ROWS, MLSTM_H
    g_t = g.T[:GR, :]
    lf = _log_sigmoid(g_t)
    fwd = lax.broadcasted_iota(jnp.int32, (GR, CHUNK), 0) < 2 * H
    lane = lax.broadcasted_iota(jnp.int32, (GR, CHUNK), 1)

    def scans(x, op, fill):
        left, right = x, x
        k = 1
        while k < CHUNK:
            left = op(left, jnp.where(lane >= k, pltpu.roll(left, k, 1), fill))
            right = op(right, jnp.where(lane < CHUNK - k, pltpu.roll(right, CHUNK - k, 1), fill))
            k *= 2
        return left, right

    pre, suf = scans(lf, jnp.add, 0.0)
    cum = jnp.where(fwd, pre, suf)
    tot = pre + suf - lf
    w = pltpu.roll(g_t, H, 0) - cum
    pm, sm = scans(w, jnp.maximum, -jnp.inf)
    cmax = jnp.where(fwd, pm, sm)
    wmax = jnp.broadcast_to(jnp.max(w, axis=1, keepdims=True), w.shape)
    wt = jnp.concatenate([w, jnp.zeros((LANES - GR, CHUNK), F32)], axis=0).T
    return (cum, tot, cmax, wmax), wt


def _inproj_kernel(x_ref, sh_ref, sc_ref, wqk_ref, wv_ref, wo_ref, wg_ref, wf_ref,
                   bqk_ref, bv_ref, bo_ref, bg_ref, bf_ref, qk_ref, vt_ref, o_ref, f_ref, tab_ref,
                   wt_ref):
    u = (x_ref[...] * (1.0 + sc_ref[...]) + sh_ref[...]).astype(BF16)
    d = D_MODEL
    g = _dot(u, wg_ref[...]) + bg_ref[...]
    for c in range(tab_ref.shape[0]):
        rows = slice(c * CHUNK, (c + 1) * CHUNK)
        tabs, wt = _gate_tables(g[rows, :])
        wt_ref[rows, :] = wt
        for k, tab in enumerate(tabs):
            tab_ref[c, k] = tab
    qk_ref[:, :d] = (_dot(u, wqk_ref[:, :d]) + bqk_ref[:, :d]).astype(BF16)
    qk_ref[:, d:] = ((_dot(u, wqk_ref[:, d:]) + bqk_ref[:, d:]) * (MLSTM_DH ** -0.5)).astype(BF16)
    v_t = (_dot(u, wv_ref[...]) + bv_ref[...]).T
    for c in range(vt_ref.shape[0]):
        vt_ref[c] = v_t[:, c * CHUNK:(c + 1) * CHUNK].astype(BF16)
    o_ref[...] = _dot(u, wo_ref[...]) + bo_ref[...]
    f_ref[...] = _dot(u, wf_ref[...]) + bf_ref[...]


def _layer_window(layer, rows, width, k):
    return pl.BlockSpec((None, rows, width), lambda *_: (layer, 0, k), pipeline_mode=pl.Buffered(1))


def _inproj(x2d, mods, seq_len, layer, wts):
    n = x2d.shape[0]
    tm = 512
    per_mod = max(seq_len // tm, 1) if mods.shape[0] > 1 else n
    mrow = (lambda i: i // per_mod) if mods.shape[0] > 1 else (lambda i: 0)
    d = D_MODEL
    wspec = functools.partial(_layer_window, layer)
    resident4 = lambda rows: pl.BlockSpec((None, None, rows, d), lambda i: (layer, 0, 0, 0),
                                          pipeline_mode=pl.Buffered(1))
    specs = lambda rows: [wspec(rows, 2 * d, 0), wspec(rows, d, 2), wspec(rows, d, 3),
                          wspec(rows, LANES, 0), resident4(rows)]
    return pl.pallas_call(
        _inproj_kernel,
        out_shape=(jax.ShapeDtypeStruct((n, 2 * d), BF16),
                   jax.ShapeDtypeStruct((n // CHUNK, d, CHUNK), BF16),
                   jax.ShapeDtypeStruct((n, d), F32),
                   jax.ShapeDtypeStruct((n, d), F32),
                   jax.ShapeDtypeStruct((n // CHUNK, 4, GATE_ROWS, CHUNK), F32),
                   jax.ShapeDtypeStruct((n, LANES), F32)),
        grid=(n // tm,),
        in_specs=[pl.BlockSpec((tm, d), lambda i: (i, 0)),
                  pl.BlockSpec((None, 1, d), lambda i: (mrow(i), 0, 0)),
                  pl.BlockSpec((None, 1, d), lambda i: (mrow(i), 0, 1))]
                 + specs(d) + specs(1),
        out_specs=(pl.BlockSpec((tm, 2 * d), lambda i: (i, 0)),
                   pl.BlockSpec((tm // CHUNK, d, CHUNK), lambda i: (i, 0, 0)),
                   pl.BlockSpec((tm, d), lambda i: (i, 0)),
                   pl.BlockSpec((tm, d), lambda i: (i, 0)),
                   pl.BlockSpec((tm // CHUNK, 4, GATE_ROWS, CHUNK), lambda i: (i, 0, 0, 0)),
                   pl.BlockSpec((tm, LANES), lambda i: (i, 0))),
        compiler_params=_cparams(1),
        name="inproj",
    )(x2d, mods, mods, wts['w_qkvo'], wts['w_qkvo'], wts['w_qkvo'], wts['w_g'], wts['w_fou'],
      wts['b_qkvo'], wts['b_qkvo'], wts['b_qkvo'], wts['b_g'], wts['b_fou'])


def _mlstm_kernel(*refs, seq_len, seqs, layer, has_init, emit_state, n_prev):
    it = iter(refs)
    qk_ref, vt_ref, tab_ref, wt_ref = (next(it) for _ in range(4))
    if has_init:
        c0_ref, n0_ref, m0_ref = (next(it) for _ in range(3))
    for _ in range(n_prev):
        next(it)
    hm_ref = next(it)
    if emit_state:
        cn_ref, nn_ref, mn_ref = (next(it) for _ in range(3))
    hf_s, hb_s, ct_s, n_s, m_s = it

    nc = seq_len // CHUNK
    ncb = nc * seqs
    H, DH, W = MLSTM_H, MLSTM_DH, MLSTM_W
    HALF = CHUNK // 2
    assert CHUNK == DH
    ri = lax.broadcasted_iota(jnp.int32, (HALF, HALF), 0)
    ci = lax.broadcasted_iota(jnp.int32, (HALF, HALF), 1)
    tri_mask = {False: ri <= ci, True: ri >= ci}

    if has_init:
        for d in range(2):
            for h in range(H):
                ct_s[d, h] = c0_ref[d, h].T
                n_s[d, h] = n0_ref[d, h:h + 1, :]
                m_s[d, h] = jnp.full((1, CHUNK), m0_ref[pl.program_id(0), layer, d, h], F32)
    else:
        m_s[...] = jnp.zeros_like(m_s)
    ones_rows = jnp.ones((16, CHUNK), BF16)

    def chain(c, h, backward, use_state, update_state, q=0):
        static = isinstance(c, int)
        r0 = c * CHUNK if static else pl.multiple_of(c * CHUNK, CHUNK)
        rows = pl.ds(r0, CHUNK)
        d = 2 * q + (1 if backward else 0)
        hc = slice(h * DH, (h + 1) * DH)
        gf = (3 if backward else 1) * H + h
        qc = qk_ref[rows, hc]
        kc = qk_ref[rows, W + h * DH:W + (h + 1) * DH]
        vt1 = jnp.concatenate([vt_ref[c, hc, :], ones_rows], axis=0)
        grow = pl.ds(gf, 1)
        brow = tab_ref[c, 0, grow, :]
        g_tot = tab_ref[c, 1, grow, :]
        cmax = tab_ref[c, 2, grow, :]
        wmax = tab_ref[c, 3, grow, :]
        wb = jnp.broadcast_to(wt_ref[rows, gf:gf + 1], (CHUNK, CHUNK))
        m_prev = m_s[d, h]

        inter = brow + m_prev
        m_t = jnp.maximum(inter, brow + cmax)
        e_row = brow - m_t
        nt_dims = (((1,), (1,)), ((), ()))
        if use_state:
            n_prev = n_s[d, h]
            lhs = jnp.concatenate([kc, jnp.broadcast_to(n_prev.astype(BF16), (16, DH)),
                                   ct_s[d, h].astype(BF16)], axis=0)
            r = lax.dot_general(lhs, qc, nt_dims, preferred_element_type=F32)
            qk_t, nq, num_c = r[:CHUNK], r[CHUNK:CHUNK + 1], r[CHUNK + 16:]
        else:
            qk_t = lax.dot_general(kc, qc, nt_dims, preferred_element_type=F32)

        def quad(si, ti, masked):
            rs, cs = slice(si * HALF, (si + 1) * HALF), slice(ti * HALF, (ti + 1) * HALF)
            arg = wb[rs, cs] + e_row[:, cs]
            if masked:
                arg = jnp.where(tri_mask[backward], arg, -jnp.inf)
            return qk_t[rs, cs] * jnp.exp(arg)

        zero = jnp.zeros((HALF, HALF), F32)
        if backward:
            st = jnp.concatenate([jnp.concatenate([quad(0, 0, True), zero], axis=1),
                                  jnp.concatenate([quad(1, 0, False), quad(1, 1, True)], axis=1)], axis=0)
        else:
            st = jnp.concatenate([jnp.concatenate([quad(0, 0, True), quad(0, 1, False)], axis=1),
                                  jnp.concatenate([zero, quad(1, 1, True)], axis=1)], axis=0)
        r1 = _dot(vt1, st.astype(BF16))
        num, den = r1[:DH], r1[DH:DH + 1]
        if use_state:
            a = jnp.exp(inter - m_t)
            num = a * num_c + num
            den = a * nq + den
        hval = num * (1.0 / jnp.maximum(jnp.abs(den), jnp.exp(-m_t)))
        (hb_s if backward else hf_s)[c, h] = hval

        if update_state:
            m_new = jnp.maximum(g_tot + m_prev, wmax + g_tot)
            wk = jnp.exp(wb + (g_tot - m_new))
            kw = kc * wk.astype(BF16)
            r2 = _dot(vt1, kw)
            kv, nsum = r2[:DH], r2[DH:DH + 1]
            if use_state:
                a_c = jnp.exp(g_tot + m_prev - m_new)
                ct_s[d, h] = a_c * ct_s[d, h] + kv
                n_s[d, h] = a_c * n_prev + nsum
            else:
                ct_s[d, h] = kv
                n_s[d, h] = nsum
            m_s[d, h] = m_new

    def step(j, use_state, update_state):
        for q in range(seqs):
            for h in range(H):
                chain(q * nc + j, h, False, use_state, update_state, q)
                chain(q * nc + nc - 1 - j, h, True, use_state, update_state, q)

    first = 0
    if not has_init:
        step(0, False, True)
        first = 1
    last = nc if emit_state else nc - 1
    if last - first <= 2:
        for j in range(first, last):
            step(j, True, True)
    else:
        def body(j, carry):
            step(j, True, True)
            return carry
        lax.fori_loop(first, last, body, 0)
    if not emit_state:
        step(nc - 1, True, False)

    def finalize(c):
        static = isinstance(c, int)
        rows = pl.ds(c * CHUNK if static else pl.multiple_of(c * CHUNK, CHUNK), CHUNK)
        for h in range(H):
            hc = slice(h * DH, (h + 1) * DH)
            ht = hf_s[c, h] + hb_s[c, h]
            mu = jnp.mean(ht, axis=0, keepdims=True)
            cen = ht - mu
            var = jnp.mean(cen * cen, axis=0, keepdims=True)
            hm_ref[rows, hc] = (cen * lax.rsqrt(var + LN_EPS)).T

    if ncb <= 2:
        for c in range(ncb):
            finalize(c)
    else:
        def fbody(c, carry):
            finalize(c)
            return carry
        lax.fori_loop(0, ncb, fbody, 0)

    if emit_state:
        for q in range(seqs):
            for d in range(2):
                for h in range(H):
                    cn_ref[q, d, h] = ct_s[2 * q + d, h].T
                    nn_ref[q, d, h:h + 1, :] = n_s[2 * q + d, h]
                    mn_ref[q, d, h:h + 1, :] = m_s[2 * q + d, h][:, :LANES]


def _mlstm(qk, vt, tabs, wt, batch, seq_len, layer, init_state, prev_state):
    n = batch * seq_len
    w = MLSTM_W
    H, DH = MLSTM_H, MLSTM_DH
    has_init = init_state is not None
    emit_state = not has_init
    seqs = 2 if (seq_len == CHUNK and not has_init and batch % 2 == 0) else 1
    rows = seq_len * seqs
    ncb = rows // CHUNK
    in_specs = [pl.BlockSpec((rows, 2 * w), lambda b: (b, 0)),
                pl.BlockSpec((ncb, w, CHUNK), lambda b: (b, 0, 0)),
                pl.BlockSpec((ncb, 4, GATE_ROWS, CHUNK), lambda b: (b, 0, 0, 0)),
                pl.BlockSpec((rows, LANES), lambda b: (b, 0))]
    args = [qk, vt, tabs, wt]
    init_specs = [pl.BlockSpec((None, None, 2, H, DH, DH), lambda b: (b, layer, 0, 0, 0, 0)),
                  pl.BlockSpec((None, None, 2, H, DH), lambda b: (b, layer, 0, 0, 0))]
    new_specs = [pl.BlockSpec((seqs, None, 2, H, DH, DH), lambda b: (b, layer, 0, 0, 0, 0)),
                 pl.BlockSpec((seqs, None, 2, H, DH), lambda b: (b, layer, 0, 0, 0)),
                 pl.BlockSpec((seqs, None, 2, H, LANES), lambda b: (b, layer, 0, 0, 0))]
    aliases = {}
    if has_init:
        in_specs += init_specs + [pl.BlockSpec(memory_space=pltpu.SMEM)]
        args += list(init_state)
    out_shape = [jax.ShapeDtypeStruct((n, w), F32)]
    out_specs = [pl.BlockSpec((rows, w), lambda b: (b, 0))]
    if emit_state:
        out_shape += [jax.ShapeDtypeStruct((batch, DEPTH, 2, H, DH, DH), F32),
                      jax.ShapeDtypeStruct((batch, DEPTH, 2, H, DH), F32),
                      jax.ShapeDtypeStruct((batch, DEPTH, 2, H, LANES), F32)]
        out_specs += new_specs
        if prev_state is not None:
            aliases = {len(args) + k: 1 + k for k in range(3)}
            in_specs += [pl.BlockSpec(memory_space=pl.ANY)] * 3
            args += list(prev_state)
    scratch = [pltpu.VMEM((ncb, H, DH, CHUNK), F32),
               pltpu.VMEM((ncb, H, DH, CHUNK), F32),
               pltpu.VMEM((2 * seqs, H, DH, DH), F32),
               pltpu.VMEM((2 * seqs, H, 1, DH), F32),
               pltpu.VMEM((2 * seqs, H, 1, CHUNK), F32)]
    return pl.pallas_call(
        functools.partial(_mlstm_kernel, seq_len=seq_len, seqs=seqs, layer=layer, has_init=has_init,
                          emit_state=emit_state, n_prev=len(aliases)),
        out_shape=tuple(out_shape),
        grid=(batch // seqs,),
        in_specs=in_specs,
        out_specs=tuple(out_specs),
        scratch_shapes=scratch,
        input_output_aliases=aliases,
        compiler_params=_cparams(1),
        name="mlstm_init" if has_init else "mlstm_zero",
    )(*args)


def _dft_tables(seq_len):
    gw = FOURIER_GW
    kc = np.arange(gw)
    ang_c = 2.0 * np.pi * ((kc[:, None] * kc[None, :]) % gw) / gw
    w_chan = np.concatenate([np.cos(ang_c), np.sin(ang_c)], axis=1)
    kt = np.arange(seq_len)
    ang_t = 2.0 * np.pi * ((kt[:, None] * kt[None, :]) % seq_len) / seq_len
    scale = 1.0 / np.sqrt(seq_len * gw)
    return (jnp.asarray(w_chan, dtype=F32).astype(BF16),
            jnp.asarray(np.cos(ang_t) * scale, dtype=F32).astype(BF16),
            jnp.asarray(-np.sin(ang_t) * scale, dtype=F32).astype(BF16))


def _fourier_kernel(x_ref, wc_ref, ct_ref, st_ref, out_ref, *, seq_len):
    gw = FOURIER_GW
    for g in range(FOURIER_G):
        cols = slice(g * gw, (g + 1) * gw)
        a = _dot(x_ref[:, cols].astype(BF16), wc_ref[...])
        for r in range(0, x_ref.shape[0], seq_len):
            a_c = a[r:r + seq_len, :gw].astype(BF16)
            a_s = a[r:r + seq_len, gw:].astype(BF16)
            y = _dot(ct_ref[...], a_c) + _dot(st_ref[...], a_s)
            out_ref[r:r + seq_len, cols] = y.astype(BF16)


def _fourier(xf, batch, seq_len):
    w_chan, c_t, s_t = _dft_tables(seq_len)
    n = batch * seq_len
    tm = max(seq_len, 1024)
    return pl.pallas_call(
        functools.partial(_fourier_kernel, seq_len=seq_len),
        out_shape=jax.ShapeDtypeStruct((n, FOURIER_W), BF16),
        grid=(n // tm,),
        in_specs=[pl.BlockSpec((tm, FOURIER_W), lambda b: (b, 0)),
                  _resident(w_chan.shape), _resident(c_t.shape), _resident(s_t.shape)],
        out_specs=pl.BlockSpec((tm, FOURIER_W), lambda b: (b, 0)),
        compiler_params=_cparams(1),
        name="fourier",
    )(xf, w_chan, c_t, s_t)


def _merge_kernel(*refs, row_len):
    (x_ref, sh_ref, sc_ref, gt_ref, hn_ref, o_ref, hf_ref, wl_ref, bl_ref, ng_ref, cw_ref, cb_ref,
     wm_ref, wf_ref, wc_ref, wo_ref, lg_ref, lb_ref, out_ref) = refs
    x = x_ref[...]
    tm = x.shape[0]
    u = (x * (1.0 + sc_ref[...]) + sh_ref[...]).astype(BF16)

    def seg(k):
        return _dot(u, wl_ref[k]) + bl_ref[k]

    uc = seg(1) * seg(2)
    pos = lax.broadcasted_iota(jnp.int32, (tm, 1), 0) % row_len
    prev = jnp.where(pos == 0, 0.0, pltpu.roll(uc, 1, 0))
    nxt = jnp.where(pos == row_len - 1, 0.0, pltpu.roll(uc, tm - 1, 0))
    yc = prev * cw_ref[0:1, :] + uc * cw_ref[1:2, :] + nxt * cw_ref[2:3, :] + cb_ref[...]
    h_c = (seg(0) * yc).astype(BF16)
    h_m = (_sigmoid(o_ref[...]) * hn_ref[...] * ng_ref[...]).astype(BF16)
    merged = (_sigmoid(seg(3)) * _dot(h_m, wm_ref[...])
              + _sigmoid(seg(4)) * _dot(hf_ref[...], wf_ref[...])
              + _sigmoid(seg(5)) * _dot(h_c, wc_ref[...]))
    y = _dot(merged.astype(BF16), wo_ref[...])
    out_ref[...] = _layer_norm(ALPHA * x + gt_ref[...] * y, lg_ref[...], lb_ref[...])


def _merge(x2d, mods, seq_len, row_len, layer, h_n, o, h_f, wts):
    n = x2d.shape[0]
    d = D_MODEL
    tm = 256
    per_mod = seq_len // tm
    mrow = (lambda i: i // per_mod) if mods.shape[0] > 1 else (lambda i: 0)
    tile = lambda i: (i, 0)
    wspec = functools.partial(_layer_window, layer)
    segs = lambda rows: pl.BlockSpec((None, 6, rows, d), lambda i: (layer, 0, 0, 0),
                                     pipeline_mode=pl.Buffered(1))
    return pl.pallas_call(
        functools.partial(_merge_kernel, row_len=row_len),
        out_shape=jax.ShapeDtypeStruct((n, d), F32),
        grid=(n // tm,),
        in_specs=[pl.BlockSpec((tm, d), tile),
                  pl.BlockSpec((None, 1, d), lambda i: (mrow(i), 0, 0)),
                  pl.BlockSpec((None, 1, d), lambda i: (mrow(i), 0, 1)),
                  pl.BlockSpec((None, 1, d), lambda i: (mrow(i), 0, 2)),
                  pl.BlockSpec((tm, d), tile), pl.BlockSpec((tm, d), tile), pl.BlockSpec((tm, d), tile),
                  segs(d), segs(1)]
                 + [wspec(1, d, 0),
                    wspec(3, d, 0), wspec(1, d, 0), wspec(d, d, 0), wspec(d, d, 0), wspec(d, d, 0),
                    wspec(d, d, 0),
                    pl.BlockSpec((None, 1, d), lambda i: (2 * layer, 0, 0)),
                    pl.BlockSpec((None, 1, d), lambda i: (2 * layer, 0, 0))],
        out_specs=pl.BlockSpec((tm, d), tile),
        compiler_params=_cparams(1),
        name="merge",
    )(x2d, mods, mods, mods, h_n, o, h_f, wts['w_loc'], wts['b_loc'], wts['norm_g'], wts['conv_w'],
      wts['conv_b'], wts['w_m'], wts['w_f'], wts['w_c'], wts['w_o'], wts['ln_g'], wts['ln_b'])


def _ffn_kernel(x_ref, sh_ref, sc_ref, gt_ref, wgu_ref, wd_ref, lg_ref, lb_ref, out_ref):
    half = x_ref.shape[0] // 2
    for r in range(2):
        rows = slice(r * half, (r + 1) * half)
        x = x_ref[rows, :]
        u = (x * (1.0 + sc_ref[...]) + sh_ref[...]).astype(BF16)
        a = _dot(u, wgu_ref[:, :D_FF])
        b = _dot(u, wgu_ref[:, D_FF:])
        hidden = (a * _sigmoid(a) * b).astype(BF16)
        y = _dot(hidden, wd_ref[...])
        out_ref[rows, :] = _layer_norm(ALPHA * x + gt_ref[...] * y, lg_ref[...], lb_ref[...])


def _ffn(x2d, mods, seq_len, layer, w_gu_bf, w_d_bf, ln_g, ln_b):
    n = x2d.shape[0]
    d = D_MODEL
    tm = 512
    per_mod = seq_len // tm
    mrow = (lambda i: i // per_mod) if mods.shape[0] > 1 else (lambda i: 0)
    tile = lambda i: (i, 0)
    wspec = functools.partial(_layer_window, layer)
    return pl.pallas_call(
        _ffn_kernel,
        out_shape=jax.ShapeDtypeStruct((n, d), F32),
        grid=(n // tm,),
        in_specs=[pl.BlockSpec((tm, d), tile),
                  pl.BlockSpec((None, 1, d), lambda i: (mrow(i), 0, 3)),
                  pl.BlockSpec((None, 1, d), lambda i: (mrow(i), 0, 4)),
                  pl.BlockSpec((None, 1, d), lambda i: (mrow(i), 0, 5)),
                  wspec(d, 2 * D_FF, 0), wspec(D_FF, d, 0),
                  pl.BlockSpec((None, 1, d), lambda i: (2 * layer + 1, 0, 0)),
                  pl.BlockSpec((None, 1, d), lambda i: (2 * layer + 1, 0, 0))],
        out_specs=pl.BlockSpec((tm, d), tile),
        compiler_params=_cparams(1),
        name="ffn",
    )(x2d, mods, mods, mods, w_gu_bf, w_d_bf, ln_g, ln_b)


def _prepared_weights(w_in, b_in, mlstm_norm_g, conv_w, conv_b, w_br_mlstm, w_br_fourier, w_br_conv,
                      w_out, ln_g, ln_b, w_gate_up, w_down):
    d = D_MODEL
    b_in3 = b_in[:, None, :]
    first_f = OFF_GATES // d
    first_l = (OFF_CONV - WIN_PHASE) // d
    return dict(
        w_qkvo=_cast_cols(w_in, d, 0, 4), b_qkvo=b_in3[:, :, :OFF_GATES],
        w_g=_cast_cols(w_in, LANES, OFF_GATES // LANES, 1), b_g=b_in3[:, :, OFF_GATES:OFF_GATES + LANES],
        w_fou=_cast_shifted(w_in, first_f, 1), b_fou=b_in3[:, :, OFF_FOURIER:OFF_CONV][:, None],
        w_loc=_cast_shifted(w_in, first_l, 6), b_loc=b_in[:, OFF_CONV:].reshape(DEPTH, 6, 1, d),
        norm_g=mlstm_norm_g[:, None, :], conv_w=conv_w, conv_b=conv_b[:, None, :],
        w_m=w_br_mlstm.astype(BF16), w_f=w_br_fourier.astype(BF16), w_c=w_br_conv.astype(BF16),
        w_o=w_out.astype(BF16),
        ln_g=ln_g.reshape(2 * DEPTH, 1, D_MODEL), ln_b=ln_b.reshape(2 * DEPTH, 1, D_MODEL),
        w_gu=w_gate_up.astype(BF16), w_d=w_down.astype(BF16))


def _trunk_layer(x2d, mods, batch, seq_len, row_len, layer, wts, init_state, prev_state):
    qk, vt, o, xf, tabs, wt = _inproj(x2d, mods, seq_len, layer, wts)
    res = _mlstm(qk, vt, tabs, wt, batch, seq_len, layer, init_state, prev_state)
    h_f = _fourier(xf, batch, seq_len)
    x1 = _merge(x2d, mods, seq_len, row_len, layer, res[0], o, h_f, wts)
    x2 = _ffn(x1, mods, seq_len, layer, wts['w_gu'], wts['w_d'], wts['ln_g'], wts['ln_b'])
    return x2, res[1:]


def kernel(x_prompt, x_sample, state_C, state_n, state_m, c, c_ctx, w_ada, b_ada, w_in, b_in,
           mlstm_norm_g, conv_w, conv_b, w_br_mlstm, w_br_fourier, w_br_conv, w_out, ln_g, ln_b,
           w_gate_up, w_down):
    bp, tp, d = x_prompt.shape
    bs, ts, _ = x_sample.shape
    cond = jnp.concatenate([c_ctx[None, :], c, jnp.zeros((8 - 1 - bs, d), F32)], axis=0)
    mods = _ada_mods(cond, w_ada, b_ada)
    wts = _prepared_weights(w_in, b_in, mlstm_norm_g, conv_w, conv_b, w_br_mlstm, w_br_fourier,
                            w_br_conv, w_out, ln_g, ln_b, w_gate_up, w_down)

    xp = x_prompt.reshape(bp * tp, d)
    xs = _add_pos(x_sample, _grid_pos_embed(ts // GRID_W, x_sample.dtype)).reshape(bs * ts, d)
    new_state = None
    for l in range(DEPTH):
        mods_ctx = mods[l, 0:1].reshape(1, 1, 6 * d)
        mods_smp = mods[l, 1:1 + bs].reshape(bs, 1, 6 * d)
        xp, new_state = _trunk_layer(xp, mods_ctx, bp, tp, tp, l, wts, None, new_state)
        xs, _ = _trunk_layer(xs, mods_smp, bs, ts, GRID_W, l, wts, (state_C, state_n, state_m), None)
    c_new, n_new, m_new = new_state
    return (xp.reshape(bp, tp, d), xs.reshape(bs, ts, d), c_new, n_new, m_new[..., 0])
```

```python
import functools

import numpy as np
import jax
import jax.numpy as jnp
from jax import lax
from jax.experimental import pallas as pl
from jax.experimental.pallas import tpu as pltpu

D_MODEL = 1024
DEPTH = 2
GRID_W = 64
MLSTM_H = 4
MLSTM_DH = 256
MLSTM_W = MLSTM_H * MLSTM_DH
CHUNK = 256
GATE_ROWS = 4 * MLSTM_H
FOURIER_G = 4
FOURIER_GW = 256
FOURIER_W = FOURIER_G * FOURIER_GW
CONV_W = 1024
D_FF = -(-8 * D_MODEL // (3 * 256)) * 256
ALPHA = (2 * DEPTH) ** 0.25
LN_EPS = 1e-5
POS_BASE = 10000.0

OFF_GATES = 4 * MLSTM_W
OFF_FOURIER = OFF_GATES + 4 * MLSTM_H
OFF_CONV = OFF_FOURIER + FOURIER_W
OFF_MERGE = OFF_CONV + 3 * CONV_W
N_IN = OFF_MERGE + 3 * D_MODEL

LANES = 128
WIN_PHASE = OFF_FOURIER % LANES
VMEM_LIMIT = 56 * 1024 * 1024

BF16 = jnp.bfloat16
F32 = jnp.float32


def _cparams(n_axes):
    return pltpu.CompilerParams(dimension_semantics=("arbitrary",) * n_axes,
                                vmem_limit_bytes=VMEM_LIMIT)


def _resident(shape):
    zeros = (0,) * len(shape)
    return pl.BlockSpec(shape, lambda *_: zeros, pipeline_mode=pl.Buffered(1))


def _sigmoid(x):
    return 1.0 / (1.0 + jnp.exp(-x))


def _log_sigmoid(x):
    return jnp.minimum(x, 0.0) - jnp.log1p(jnp.exp(-jnp.abs(x)))


def _layer_norm(x, g, b):
    mu = jnp.mean(x, axis=-1, keepdims=True)
    xc = x - mu
    var = jnp.mean(xc * xc, axis=-1, keepdims=True)
    return xc * lax.rsqrt(var + LN_EPS) * g + b


def _dot(a, b):
    return jnp.dot(a, b, preferred_element_type=F32)


def _ada_kernel(cond_ref, w_ref, b_ref, out_ref):
    cond = cond_ref[...]
    act = (cond * _sigmoid(cond)).astype(BF16)
    out_ref[...] = _dot(act, w_ref[...].astype(BF16)) + b_ref[...]


def _ada_mods(cond, w_ada, b_ada):
    tn = 1536
    nb = 6 * D_MODEL // tn
    return pl.pallas_call(
        _ada_kernel,
        out_shape=jax.ShapeDtypeStruct((DEPTH, 8, 6 * D_MODEL), F32),
        grid=(DEPTH, nb),
        in_specs=[pl.BlockSpec((8, D_MODEL), lambda l, j: (0, 0)),
                  pl.BlockSpec((None, D_MODEL, tn), lambda l, j: (l, 0, j)),
                  pl.BlockSpec((None, 1, tn), lambda l, j: (l, 0, j))],
        out_specs=pl.BlockSpec((None, 8, tn), lambda l, j: (l, 0, j)),
        compiler_params=_cparams(2),
        name="ada_mods",
    )(cond, w_ada, b_ada.reshape(DEPTH, 1, 6 * D_MODEL))


def _add_pos_kernel(x_ref, pos_ref, out_ref):
    out_ref[...] = x_ref[...] + pos_ref[...]


def _add_pos(x, pos):
    b, t, d = x.shape
    return pl.pallas_call(
        _add_pos_kernel,
        out_shape=jax.ShapeDtypeStruct(x.shape, x.dtype),
        grid=(b,),
        in_specs=[pl.BlockSpec((None, t, d), lambda i: (i, 0, 0)),
                  pl.BlockSpec((t, d), lambda i: (0, 0))],
        out_specs=pl.BlockSpec((None, t, d), lambda i: (i, 0, 0)),
        compiler_params=_cparams(1),
        name="add_pos",
    )(x, pos)


def _grid_pos_embed(rows, dtype):
    quarter = D_MODEL // 4
    freqs = (POS_BASE ** (-np.arange(quarter, dtype=np.float32) / quarter)).astype(np.float32)
    r = np.repeat(np.arange(rows, dtype=np.float32), GRID_W)[:, None] * freqs
    col = np.tile(np.arange(GRID_W, dtype=np.float32), rows)[:, None] * freqs
    return jnp.asarray(np.concatenate([np.sin(r), np.cos(r), np.sin(col), np.cos(col)], axis=-1), dtype)


def _realign_kernel(a_ref, b_ref, out_ref):
    out_ref[...] = jnp.concatenate([a_ref[:, WIN_PHASE:], b_ref[:, :WIN_PHASE]], axis=1)


def _realigned(w, first, count):
    depth, rows, _ = w.shape
    d = D_MODEL
    per = d // LANES
    return pl.pallas_call(
        _realign_kernel,
        out_shape=jax.ShapeDtypeStruct((depth, count, rows, d), w.dtype),
        grid=(depth, count),
        in_specs=[pl.BlockSpec((None, rows, d), lambda l, j: (l, 0, first + j)),
                  pl.BlockSpec((None, rows, LANES), lambda l, j: (l, 0, (first + j + 1) * per))],
        out_specs=pl.BlockSpec((None, None, rows, d), lambda l, j: (l, j, 0, 0)),
        compiler_params=_cparams(2),
        name="realign",
    )(w, w)


def _gate_tables(g):
    GR, H = GATE_ROWS, MLSTM_H
    g_t = g.T[:GR, :]
    lf = _log_sigmoid(g_t)
    fwd = lax.broadcasted_iota(jnp.int32, (GR, CHUNK), 0) < 2 * H
    lane = lax.broadcasted_iota(jnp.int32, (GR, CHUNK), 1)

    def scans(x, op, fill):
        left, right = x, x
        k = 1
        while k < CHUNK:
            left = op(left, jnp.where(lane >= k, pltpu.roll(left, k, 1), fill))
            right = op(right, jnp.where(lane < CHUNK - k, pltpu.roll(right, CHUNK - k, 1), fill))
            k *= 2
        return left, right

    pre, suf = scans(lf, jnp.add, 0.0)
    cum = jnp.where(fwd, pre, suf)
    tot = pre + suf - lf
    w = pltpu.roll(g_t, H, 0) - cum
    pm, sm = scans(w, jnp.maximum, -jnp.inf)
    cmax = jnp.where(fwd, pm, sm)
    wmax = jnp.broadcast_to(jnp.max(w, axis=1, keepdims=True), w.shape)
    wt = jnp.concatenate([w, jnp.zeros((LANES - GR, CHUNK), F32)], axis=0).T
    return (cum, tot, cmax, wmax), wt


def _inproj_kernel(x_ref, sh_ref, sc_ref, wqk_ref, wv_ref, wo_ref, wg_ref, wf_ref,
                   bqk_ref, bv_ref, bo_ref, bg_ref, bf_ref, qk_ref, vt_ref, o_ref, f_ref, tab_ref,
                   wt_ref):
    u = (x_ref[...] * (1.0 + sc_ref[...]) + sh_ref[...]).astype(BF16)
    d = D_MODEL
    g = _dot(u, wg_ref[...]) + bg_ref[...]
    for c in range(tab_ref.shape[0]):
        rows = slice(c * CHUNK, (c + 1) * CHUNK)
        tabs, wt = _gate_tables(g[rows, :])
        wt_ref[rows, :] = wt
        for k, tab in enumerate(tabs):
            tab_ref[c, k] = tab
    qk_ref[:, :d] = (_dot(u, wqk_ref[:, :d]) + bqk_ref[:, :d]).astype(BF16)
    qk_ref[:, d:] = ((_dot(u, wqk_ref[:, d:]) + bqk_ref[:, d:]) * (MLSTM_DH ** -0.5)).astype(BF16)
    v_t = (_dot(u, wv_ref[...]) + bv_ref[...]).T
    for c in range(vt_ref.shape[0]):
        vt_ref[c] = v_t[:, c * CHUNK:(c + 1) * CHUNK].astype(BF16)
    o_ref[...] = _dot(u, wo_ref[...]) + bo_ref[...]
    f_ref[...] = _dot(u, wf_ref[...]) + bf_ref[...]


def _layer_window(layer, rows, width, k):
    return pl.BlockSpec((None, rows, width), lambda *_: (layer, 0, k), pipeline_mode=pl.Buffered(1))


def _inproj(x2d, mods, seq_len, layer, wts):
    n = x2d.shape[0]
    tm = 512
    per_mod = max(seq_len // tm, 1) if mods.shape[0] > 1 else n
    mrow = (lambda i: i // per_mod) if mods.shape[0] > 1 else (lambda i: 0)
    d = D_MODEL
    wspec = functools.partial(_layer_window, layer)
    resident4 = lambda rows: pl.BlockSpec((None, None, rows, d), lambda i: (layer, 0, 0, 0),
                                          pipeline_mode=pl.Buffered(1))
    specs = lambda rows: [wspec(rows, 2 * d, 0), wspec(rows, d, 2), wspec(rows, d, 3),
                          wspec(rows, LANES, OFF_GATES // LANES), resident4(rows)]
    return pl.pallas_call(
        _inproj_kernel,
        out_shape=(jax.ShapeDtypeStruct((n, 2 * d), BF16),
                   jax.ShapeDtypeStruct((n // CHUNK, d, CHUNK), BF16),
                   jax.ShapeDtypeStruct((n, d), F32),
                   jax.ShapeDtypeStruct((n, d), F32),
                   jax.ShapeDtypeStruct((n // CHUNK, 4, GATE_ROWS, CHUNK), F32),
                   jax.ShapeDtypeStruct((n, LANES), F32)),
        grid=(n // tm,),
        in_specs=[pl.BlockSpec((tm, d), lambda i: (i, 0)),
                  pl.BlockSpec((None, 1, d), lambda i: (mrow(i), 0, 0)),
                  pl.BlockSpec((None, 1, d), lambda i: (mrow(i), 0, 1))]
                 + specs(d) + specs(1),
        out_specs=(pl.BlockSpec((tm, 2 * d), lambda i: (i, 0)),
                   pl.BlockSpec((tm // CHUNK, d, CHUNK), lambda i: (i, 0, 0)),
                   pl.BlockSpec((tm, d), lambda i: (i, 0)),
                   pl.BlockSpec((tm, d), lambda i: (i, 0)),
                   pl.BlockSpec((tm // CHUNK, 4, GATE_ROWS, CHUNK), lambda i: (i, 0, 0, 0)),
                   pl.BlockSpec((tm, LANES), lambda i: (i, 0))),
        compiler_params=_cparams(1),
        name="inproj",
    )(x2d, mods, mods, *([wts['w_in']] * 4), wts['w_fou'], *([wts['b_in']] * 4), wts['b_fou'])


def _mlstm_kernel(*refs, seq_len, seqs, layer, has_init, emit_state, n_prev):
    it = iter(refs)
    qk_ref, vt_ref, tab_ref, wt_ref = (next(it) for _ in range(4))
    if has_init:
        c0_ref, n0_ref, m0_ref = (next(it) for _ in range(3))
    for _ in range(n_prev):
        next(it)
    hm_ref = next(it)
    if emit_state:
        cn_ref, nn_ref, mn_ref = (next(it) for _ in range(3))
    hf_s, hb_s, ct_s, n_s, m_s = it

    nc = seq_len // CHUNK
    ncb = nc * seqs
    H, DH, W = MLSTM_H, MLSTM_DH, MLSTM_W
    HALF = CHUNK // 2
    assert CHUNK == DH
    ri = lax.broadcasted_iota(jnp.int32, (HALF, HALF), 0)
    ci = lax.broadcasted_iota(jnp.int32, (HALF, HALF), 1)
    tri_mask = {False: ri <= ci, True: ri >= ci}

    if has_init:
        for d in range(2):
            for h in range(H):
                ct_s[d, h] = c0_ref[d, h].T
                n_s[d, h] = n0_ref[d, h:h + 1, :]
                m_s[d, h] = jnp.full((1, CHUNK), m0_ref[pl.program_id(0), layer, d, h], F32)
    else:
        m_s[...] = jnp.zeros_like(m_s)
    ones_rows = jnp.ones((16, CHUNK), BF16)

    def chain(c, h, backward, use_state, update_state, q=0):
        static = isinstance(c, int)
        r0 = c * CHUNK if static else pl.multiple_of(c * CHUNK, CHUNK)
        rows = pl.ds(r0, CHUNK)
        d = 2 * q + (1 if backward else 0)
        hc = slice(h * DH, (h + 1) * DH)
        gf = (3 if backward else 1) * H + h
        qc = qk_ref[rows, hc]
        kc = qk_ref[rows, W + h * DH:W + (h + 1) * DH]
        vt1 = jnp.concatenate([vt_ref[c, hc, :], ones_rows], axis=0)
        grow = pl.ds(gf, 1)
        brow = tab_ref[c, 0, grow, :]
        g_tot = tab_ref[c, 1, grow, :]
        cmax = tab_ref[c, 2, grow, :]
        wmax = tab_ref[c, 3, grow, :]
        wb = jnp.broadcast_to(wt_ref[rows, gf:gf + 1], (CHUNK, CHUNK))
        m_prev = m_s[d, h]

        inter = brow + m_prev
        m_t = jnp.maximum(inter, brow + cmax)
        e_row = brow - m_t
        nt_dims = (((1,), (1,)), ((), ()))
        if use_state:
            n_prev = n_s[d, h]
            lhs = jnp.concatenate([kc, jnp.broadcast_to(n_prev.astype(BF16), (16, DH)),
                                   ct_s[d, h].astype(BF16)], axis=0)
            r = lax.dot_general(lhs, qc, nt_dims, preferred_element_type=F32)
            qk_t, nq, num_c = r[:CHUNK], r[CHUNK:CHUNK + 1], r[CHUNK + 16:]
        else:
            qk_t = lax.dot_general(kc, qc, nt_dims, preferred_element_type=F32)

        def quad(si, ti, masked):
            rs, cs = slice(si * HALF, (si + 1) * HALF), slice(ti * HALF, (ti + 1) * HALF)
            arg = wb[rs, cs] + e_row[:, cs]
            if masked:
                arg = jnp.where(tri_mask[backward], arg, -jnp.inf)
            return qk_t[rs, cs] * jnp.exp(arg)

        zero = jnp.zeros((HALF, HALF), F32)
        if backward:
            st = jnp.concatenate([jnp.concatenate([quad(0, 0, True), zero], axis=1),
                                  jnp.concatenate([quad(1, 0, False), quad(1, 1, True)], axis=1)], axis=0)
        else:
            st = jnp.concatenate([jnp.concatenate([quad(0, 0, True), quad(0, 1, False)], axis=1),
                                  jnp.concatenate([zero, quad(1, 1, True)], axis=1)], axis=0)
        r1 = _dot(vt1, st.astype(BF16))
        num, den = r1[:DH], r1[DH:DH + 1]
        if use_state:
            a = jnp.exp(inter - m_t)
            num = a * num_c + num
            den = a * nq + den
        hval = num * (1.0 / jnp.maximum(jnp.abs(den), jnp.exp(-m_t)))
        (hb_s if backward else hf_s)[c, h] = hval

        if update_state:
            m_new = jnp.maximum(g_tot + m_prev, wmax + g_tot)
            wk = jnp.exp(wb + (g_tot - m_new))
            kw = kc * wk.astype(BF16)
            r2 = _dot(vt1, kw)
            kv, nsum = r2[:DH], r2[DH:DH + 1]
            if use_state:
                a_c = jnp.exp(g_tot + m_prev - m_new)
                ct_s[d, h] = a_c * ct_s[d, h] + kv
                n_s[d, h] = a_c * n_prev + nsum
            else:
                ct_s[d, h] = kv
                n_s[d, h] = nsum
            m_s[d, h] = m_new

    def step(j, use_state, update_state):
        for q in range(seqs):
            for h in range(H):
                chain(q * nc + j, h, False, use_state, update_state, q)
                chain(q * nc + nc - 1 - j, h, True, use_state, update_state, q)

    first = 0
    if not has_init:
        step(0, False, True)
        first = 1
    last = nc if emit_state else nc - 1
    if last - first <= 2:
        for j in range(first, last):
            step(j, True, True)
    else:
        def body(j, carry):
            step(j, True, True)
            return carry
        lax.fori_loop(first, last, body, 0)
    if not emit_state:
        step(nc - 1, True, False)

    def finalize(c):
        static = isinstance(c, int)
        rows = pl.ds(c * CHUNK if static else pl.multiple_of(c * CHUNK, CHUNK), CHUNK)
        for h in range(H):
            hc = slice(h * DH, (h + 1) * DH)
            ht = hf_s[c, h] + hb_s[c, h]
            mu = jnp.mean(ht, axis=0, keepdims=True)
            cen = ht - mu
            var = jnp.mean(cen * cen, axis=0, keepdims=True)
            hm_ref[rows, hc] = (cen * lax.rsqrt(var + LN_EPS)).T

    if ncb <= 2:
        for c in range(ncb):
            finalize(c)
    else:
        def fbody(c, carry):
            finalize(c)
            return carry
        lax.fori_loop(0, ncb, fbody, 0)

    if emit_state:
        for q in range(seqs):
            for d in range(2):
                for h in range(H):
                    cn_ref[q, d, h] = ct_s[2 * q + d, h].T
                    nn_ref[q, d, h:h + 1, :] = n_s[2 * q + d, h]
                    mn_ref[q, d, h:h + 1, :] = m_s[2 * q + d, h][:, :LANES]


def _mlstm(qk, vt, tabs, wt, batch, seq_len, layer, init_state, prev_state):
    n = batch * seq_len
    w = MLSTM_W
    H, DH = MLSTM_H, MLSTM_DH
    has_init = init_state is not None
    emit_state = not has_init
    seqs = 2 if (seq_len == CHUNK and not has_init and batch % 2 == 0) else 1
    rows = seq_len * seqs
    ncb = rows // CHUNK
    in_specs = [pl.BlockSpec((rows, 2 * w), lambda b: (b, 0)),
                pl.BlockSpec((ncb, w, CHUNK), lambda b: (b, 0, 0)),
                pl.BlockSpec((ncb, 4, GATE_ROWS, CHUNK), lambda b: (b, 0, 0, 0)),
                pl.BlockSpec((rows, LANES), lambda b: (b, 0))]
    args = [qk, vt, tabs, wt]
    init_specs = [pl.BlockSpec((None, None, 2, H, DH, DH), lambda b: (b, layer, 0, 0, 0, 0)),
                  pl.BlockSpec((None, None, 2, H, DH), lambda b: (b, layer, 0, 0, 0))]
    new_specs = [pl.BlockSpec((seqs, None, 2, H, DH, DH), lambda b: (b, layer, 0, 0, 0, 0)),
                 pl.BlockSpec((seqs, None, 2, H, DH), lambda b: (b, layer, 0, 0, 0)),
                 pl.BlockSpec((seqs, None, 2, H, LANES), lambda b: (b, layer, 0, 0, 0))]
    aliases = {}
    if has_init:
        in_specs += init_specs + [pl.BlockSpec(memory_space=pltpu.SMEM)]
        args += list(init_state)
    out_shape = [jax.ShapeDtypeStruct((n, w), F32)]
    out_specs = [pl.BlockSpec((rows, w), lambda b: (b, 0))]
    if emit_state:
        out_shape += [jax.ShapeDtypeStruct((batch, DEPTH, 2, H, DH, DH), F32),
                      jax.ShapeDtypeStruct((batch, DEPTH, 2, H, DH), F32),
                      jax.ShapeDtypeStruct((batch, DEPTH, 2, H, LANES), F32)]
        out_specs += new_specs
        if prev_state is not None:
            aliases = {len(args) + k: 1 + k for k in range(3)}
            in_specs += [pl.BlockSpec(memory_space=pl.ANY)] * 3
            args += list(prev_state)
    scratch = [pltpu.VMEM((ncb, H, DH, CHUNK), F32),
               pltpu.VMEM((ncb, H, DH, CHUNK), F32),
               pltpu.VMEM((2 * seqs, H, DH, DH), F32),
               pltpu.VMEM((2 * seqs, H, 1, DH), F32),
               pltpu.VMEM((2 * seqs, H, 1, CHUNK), F32)]
    return pl.pallas_call(
        functools.partial(_mlstm_kernel, seq_len=seq_len, seqs=seqs, layer=layer, has_init=has_init,
                          emit_state=emit_state, n_prev=len(aliases)),
        out_shape=tuple(out_shape),
        grid=(batch // seqs,),
        in_specs=in_specs,
        out_specs=tuple(out_specs),
        scratch_shapes=scratch,
        input_output_aliases=aliases,
        compiler_params=_cparams(1),
        name="mlstm_init" if has_init else "mlstm_zero",
    )(*args)


def _dft_tables(seq_len):
    gw = FOURIER_GW
    kc = np.arange(gw)
    ang_c = 2.0 * np.pi * ((kc[:, None] * kc[None, :]) % gw) / gw
    w_chan = np.concatenate([np.cos(ang_c), np.sin(ang_c)], axis=1)
    kt = np.arange(seq_len)
    ang_t = 2.0 * np.pi * ((kt[:, None] * kt[None, :]) % seq_len) / seq_len
    scale = 1.0 / np.sqrt(seq_len * gw)
    return (jnp.asarray(w_chan, dtype=F32).astype(BF16),
            jnp.asarray(np.cos(ang_t) * scale, dtype=F32).astype(BF16),
            jnp.asarray(-np.sin(ang_t) * scale, dtype=F32).astype(BF16))


def _fourier_kernel(x_ref, wc_ref, ct_ref, st_ref, out_ref, *, seq_len):
    gw = FOURIER_GW
    for g in range(FOURIER_G):
        cols = slice(g * gw, (g + 1) * gw)
        a = _dot(x_ref[:, cols].astype(BF16), wc_ref[...])
        for r in range(0, x_ref.shape[0], seq_len):
            a_c = a[r:r + seq_len, :gw].astype(BF16)
            a_s = a[r:r + seq_len, gw:].astype(BF16)
            y = _dot(ct_ref[...], a_c) + _dot(st_ref[...], a_s)
            out_ref[r:r + seq_len, cols] = y.astype(BF16)


def _fourier(xf, batch, seq_len):
    w_chan, c_t, s_t = _dft_tables(seq_len)
    n = batch * seq_len
    tm = max(seq_len, 1024)
    return pl.pallas_call(
        functools.partial(_fourier_kernel, seq_len=seq_len),
        out_shape=jax.ShapeDtypeStruct((n, FOURIER_W), BF16),
        grid=(n // tm,),
        in_specs=[pl.BlockSpec((tm, FOURIER_W), lambda b: (b, 0)),
                  _resident(w_chan.shape), _resident(c_t.shape), _resident(s_t.shape)],
        out_specs=pl.BlockSpec((tm, FOURIER_W), lambda b: (b, 0)),
        compiler_params=_cparams(1),
        name="fourier",
    )(xf, w_chan, c_t, s_t)


def _merge_kernel(*refs, row_len):
    (x_ref, sh_ref, sc_ref, gt_ref, hn_ref, o_ref, hf_ref, wl_ref, bl_ref, ng_ref, cw_ref, cb_ref,
     wm_ref, wf_ref, wc_ref, wo_ref, lg_ref, lb_ref, out_ref) = refs
    x = x_ref[...]
    tm = x.shape[0]
    u = (x * (1.0 + sc_ref[...]) + sh_ref[...]).astype(BF16)

    def seg(k):
        return _dot(u, wl_ref[k]) + bl_ref[k]

    uc = seg(1) * seg(2)
    pos = lax.broadcasted_iota(jnp.int32, (tm, 1), 0) % row_len
    prev = jnp.where(pos == 0, 0.0, pltpu.roll(uc, 1, 0))
    nxt = jnp.where(pos == row_len - 1, 0.0, pltpu.roll(uc, tm - 1, 0))
    yc = prev * cw_ref[0:1, :] + uc * cw_ref[1:2, :] + nxt * cw_ref[2:3, :] + cb_ref[...]
    h_c = (seg(0) * yc).astype(BF16)
    h_m = (_sigmoid(o_ref[...]) * hn_ref[...] * ng_ref[...]).astype(BF16)
    merged = (_sigmoid(seg(3)) * _dot(h_m, wm_ref[...])
              + _sigmoid(seg(4)) * _dot(hf_ref[...], wf_ref[...])
              + _sigmoid(seg(5)) * _dot(h_c, wc_ref[...]))
    y = _dot(merged.astype(BF16), wo_ref[...])
    out_ref[...] = _layer_norm(ALPHA * x + gt_ref[...] * y, lg_ref[...], lb_ref[...])


def _merge(x2d, mods, seq_len, row_len, layer, h_n, o, h_f, wts):
    n = x2d.shape[0]
    d = D_MODEL
    tm = 256
    per_mod = seq_len // tm
    mrow = (lambda i: i // per_mod) if mods.shape[0] > 1 else (lambda i: 0)
    tile = lambda i: (i, 0)
    wspec = functools.partial(_layer_window, layer)
    segs = lambda rows: pl.BlockSpec((None, 6, rows, d), lambda i: (layer, 0, 0, 0),
                                     pipeline_mode=pl.Buffered(1))
    return pl.pallas_call(
        functools.partial(_merge_kernel, row_len=row_len),
        out_shape=jax.ShapeDtypeStruct((n, d), F32),
        grid=(n // tm,),
        in_specs=[pl.BlockSpec((tm, d), tile),
                  pl.BlockSpec((None, 1, d), lambda i: (mrow(i), 0, 0)),
                  pl.BlockSpec((None, 1, d), lambda i: (mrow(i), 0, 1)),
                  pl.BlockSpec((None, 1, d), lambda i: (mrow(i), 0, 2)),
                  pl.BlockSpec((tm, d), tile), pl.BlockSpec((tm, d), tile), pl.BlockSpec((tm, d), tile),
                  segs(d), segs(1)]
                 + [wspec(1, d, 0),
                    wspec(3, d, 0), wspec(1, d, 0), wspec(d, d, 0), wspec(d, d, 0), wspec(d, d, 0),
                    wspec(d, d, 0),
                    pl.BlockSpec((None, 1, d), lambda i: (2 * layer, 0, 0)),
                    pl.BlockSpec((None, 1, d), lambda i: (2 * layer, 0, 0))],
        out_specs=pl.BlockSpec((tm, d), tile),
        compiler_params=_cparams(1),
        name="merge",
    )(x2d, mods, mods, mods, h_n, o, h_f, wts['w_loc'], wts['b_loc'], wts['norm_g'], wts['conv_w'],
      wts['conv_b'], wts['w_m'], wts['w_f'], wts['w_c'], wts['w_o'], wts['ln_g'], wts['ln_b'])


def _ffn_kernel(x_ref, sh_ref, sc_ref, gt_ref, wgu_ref, wd_ref, lg_ref, lb_ref, out_ref):
    half = x_ref.shape[0] // 2
    for r in range(2):
        rows = slice(r * half, (r + 1) * half)
        x = x_ref[rows, :]
        u = (x * (1.0 + sc_ref[...]) + sh_ref[...]).astype(BF16)
        a = _dot(u, wgu_ref[:, :D_FF])
        b = _dot(u, wgu_ref[:, D_FF:])
        hidden = (a * _sigmoid(a) * b).astype(BF16)
        y = _dot(hidden, wd_ref[...])
        out_ref[rows, :] = _layer_norm(ALPHA * x + gt_ref[...] * y, lg_ref[...], lb_ref[...])


def _ffn(x2d, mods, seq_len, layer, w_gu_bf, w_d_bf, ln_g, ln_b):
    n = x2d.shape[0]
    d = D_MODEL
    tm = 512
    per_mod = seq_len // tm
    mrow = (lambda i: i // per_mod) if mods.shape[0] > 1 else (lambda i: 0)
    tile = lambda i: (i, 0)
    wspec = functools.partial(_layer_window, layer)
    return pl.pallas_call(
        _ffn_kernel,
        out_shape=jax.ShapeDtypeStruct((n, d), F32),
        grid=(n // tm,),
        in_specs=[pl.BlockSpec((tm, d), tile),
                  pl.BlockSpec((None, 1, d), lambda i: (mrow(i), 0, 3)),
                  pl.BlockSpec((None, 1, d), lambda i: (mrow(i), 0, 4)),
                  pl.BlockSpec((None, 1, d), lambda i: (mrow(i), 0, 5)),
                  wspec(d, 2 * D_FF, 0), wspec(D_FF, d, 0),
                  pl.BlockSpec((None, 1, d), lambda i: (2 * layer + 1, 0, 0)),
                  pl.BlockSpec((None, 1, d), lambda i: (2 * layer + 1, 0, 0))],
        out_specs=pl.BlockSpec((tm, d), tile),
        compiler_params=_cparams(1),
        name="ffn",
    )(x2d, mods, mods, mods, w_gu_bf, w_d_bf, ln_g, ln_b)


def _prepared_weights(w_in, b_in, mlstm_norm_g, conv_w, conv_b, w_br_mlstm, w_br_fourier, w_br_conv,
                      w_out, ln_g, ln_b, w_gate_up, w_down):
    d = D_MODEL
    b_in3 = b_in[:, None, :]
    first_f = OFF_GATES // d
    first_l = (OFF_CONV - WIN_PHASE) // d
    w_in_bf = w_in.astype(BF16)
    return dict(
        w_in=w_in_bf, b_in=b_in3,
        w_fou=_realigned(w_in_bf, first_f, 1), b_fou=b_in3[:, :, OFF_FOURIER:OFF_CONV][:, None],
        w_loc=_realigned(w_in_bf, first_l, 6), b_loc=b_in[:, OFF_CONV:].reshape(DEPTH, 6, 1, d),
        norm_g=mlstm_norm_g[:, None, :], conv_w=conv_w, conv_b=conv_b[:, None, :],
        w_m=w_br_mlstm.astype(BF16), w_f=w_br_fourier.astype(BF16), w_c=w_br_conv.astype(BF16),
        w_o=w_out.astype(BF16),
        ln_g=ln_g.reshape(2 * DEPTH, 1, D_MODEL), ln_b=ln_b.reshape(2 * DEPTH, 1, D_MODEL),
        w_gu=w_gate_up.astype(BF16), w_d=w_down.astype(BF16))


def _trunk_layer(x2d, mods, batch, seq_len, row_len, layer, wts, init_state, prev_state):
    qk, vt, o, xf, tabs, wt = _inproj(x2d, mods, seq_len, layer, wts)
    res = _mlstm(qk, vt, tabs, wt, batch, seq_len, layer, init_state, prev_state)
    h_f = _fourier(xf, batch, seq_len)
    x1 = _merge(x2d, mods, seq_len, row_len, layer, res[0], o, h_f, wts)
    x2 = _ffn(x1, mods, seq_len, layer, wts['w_gu'], wts['w_d'], wts['ln_g'], wts['ln_b'])
    return x2, res[1:]


def kernel(x_prompt, x_sample, state_C, state_n, state_m, c, c_ctx, w_ada, b_ada, w_in, b_in,
           mlstm_norm_g, conv_w, conv_b, w_br_mlstm, w_br_fourier, w_br_conv, w_out, ln_g, ln_b,
           w_gate_up, w_down):
    bp, tp, d = x_prompt.shape
    bs, ts, _ = x_sample.shape
    cond = jnp.concatenate([c_ctx[None, :], c, jnp.zeros((8 - 1 - bs, d), F32)], axis=0)
    mods = _ada_mods(cond, w_ada, b_ada)
    wts = _prepared_weights(w_in, b_in, mlstm_norm_g, conv_w, conv_b, w_br_mlstm, w_br_fourier,
                            w_br_conv, w_out, ln_g, ln_b, w_gate_up, w_down)

    xp = x_prompt.reshape(bp * tp, d)
    xs = _add_pos(x_sample, _grid_pos_embed(ts // GRID_W, x_sample.dtype)).reshape(bs * ts, d)
    new_state = None
    for l in range(DEPTH):
        mods_ctx = mods[l, 0:1].reshape(1, 1, 6 * d)
        mods_smp = mods[l, 1:1 + bs].reshape(bs, 1, 6 * d)
        xp, new_state = _trunk_layer(xp, mods_ctx, bp, tp, tp, l, wts, None, new_state)
        xs, _ = _trunk_layer(xs, mods_smp, bs, ts, GRID_W, l, wts, (state_C, state_n, state_m), None)
    c_new, n_new, m_new = new_state
    return (xp.reshape(bp, tp, d), xs.reshape(bs, ts, d), c_new, n_new, m_new[..., 0])
```

```python
import functools

import numpy as np
import jax
import jax.numpy as jnp
from jax import lax
from jax.experimental import pallas as pl
from jax.experimental.pallas import tpu as pltpu

D_MODEL = 1024
DEPTH = 2
GRID_W = 64
MLSTM_H = 4
MLSTM_DH = 256
MLSTM_W = MLSTM_H * MLSTM_DH
CHUNK = 256
GATE_ROWS = 4 * MLSTM_H
FOURIER_G = 4
FOURIER_GW = 256
FOURIER_W = FOURIER_G * FOURIER_GW
CONV_W = 1024
D_FF = -(-8 * D_MODEL // (3 * 256)) * 256
ALPHA = (2 * DEPTH) ** 0.25
LN_EPS = 1e-5
POS_BASE = 10000.0

OFF_GATES = 4 * MLSTM_W
OFF_FOURIER = OFF_GATES + 4 * MLSTM_H
OFF_CONV = OFF_FOURIER + FOURIER_W
OFF_MERGE = OFF_CONV + 3 * CONV_W
N_IN = OFF_MERGE + 3 * D_MODEL

LANES = 128
WIN_PHASE = OFF_FOURIER % LANES
VMEM_LIMIT = 56 * 1024 * 1024

BF16 = jnp.bfloat16
F32 = jnp.float32


def _cparams(n_axes):
    return pltpu.CompilerParams(dimension_semantics=("arbitrary",) * n_axes,
                                vmem_limit_bytes=VMEM_LIMIT)


def _resident(shape):
    zeros = (0,) * len(shape)
    return pl.BlockSpec(shape, lambda *_: zeros, pipeline_mode=pl.Buffered(1))


def _sigmoid(x):
    return 1.0 / (1.0 + jnp.exp(-x))


def _log_sigmoid(x):
    return jnp.minimum(x, 0.0) - jnp.log1p(jnp.exp(-jnp.abs(x)))


def _layer_norm(x, g, b):
    mu = jnp.mean(x, axis=-1, keepdims=True)
    xc = x - mu
    var = jnp.mean(xc * xc, axis=-1, keepdims=True)
    return xc * lax.rsqrt(var + LN_EPS) * g + b


def _dot(a, b):
    return jnp.dot(a, b, preferred_element_type=F32)


def _ada_kernel(cond_ref, w_ref, b_ref, out_ref):
    cond = cond_ref[...]
    act = (cond * _sigmoid(cond)).astype(BF16)
    out_ref[...] = _dot(act, w_ref[...].astype(BF16)) + b_ref[...]


def _ada_mods(cond, w_ada, b_ada):
    tn = 1536
    nb = 6 * D_MODEL // tn
    return pl.pallas_call(
        _ada_kernel,
        out_shape=jax.ShapeDtypeStruct((DEPTH, 8, 6 * D_MODEL), F32),
        grid=(DEPTH, nb),
        in_specs=[pl.BlockSpec((8, D_MODEL), lambda l, j: (0, 0)),
                  pl.BlockSpec((None, D_MODEL, tn), lambda l, j: (l, 0, j)),
                  pl.BlockSpec((None, 1, tn), lambda l, j: (l, 0, j))],
        out_specs=pl.BlockSpec((None, 8, tn), lambda l, j: (l, 0, j)),
        compiler_params=_cparams(2),
        name="ada_mods",
    )(cond, w_ada, b_ada.reshape(DEPTH, 1, 6 * D_MODEL))


def _add_pos_kernel(x_ref, pos_ref, out_ref):
    out_ref[...] = x_ref[...] + pos_ref[...]


def _add_pos(x, pos):
    b, t, d = x.shape
    return pl.pallas_call(
        _add_pos_kernel,
        out_shape=jax.ShapeDtypeStruct(x.shape, x.dtype),
        grid=(b,),
        in_specs=[pl.BlockSpec((None, t, d), lambda i: (i, 0, 0)),
                  pl.BlockSpec((t, d), lambda i: (0, 0))],
        out_specs=pl.BlockSpec((None, t, d), lambda i: (i, 0, 0)),
        compiler_params=_cparams(1),
        name="add_pos",
    )(x, pos)


def _grid_pos_embed(rows, dtype):
    quarter = D_MODEL // 4
    freqs = (POS_BASE ** (-np.arange(quarter, dtype=np.float32) / quarter)).astype(np.float32)
    r = np.repeat(np.arange(rows, dtype=np.float32), GRID_W)[:, None] * freqs
    col = np.tile(np.arange(GRID_W, dtype=np.float32), rows)[:, None] * freqs
    return jnp.asarray(np.concatenate([np.sin(r), np.cos(r), np.sin(col), np.cos(col)], axis=-1), dtype)


def _realign_kernel(a_ref, b_ref, out_ref):
    out_ref[...] = jnp.concatenate([a_ref[:, WIN_PHASE:], b_ref[:, :WIN_PHASE]], axis=1)


def _realigned(w, first, count):
    depth, rows, _ = w.shape
    d = D_MODEL
    per = d // LANES
    return pl.pallas_call(
        _realign_kernel,
        out_shape=jax.ShapeDtypeStruct((depth, count, rows, d), w.dtype),
        grid=(depth, count),
        in_specs=[pl.BlockSpec((None, rows, d), lambda l, j: (l, 0, first + j)),
                  pl.BlockSpec((None, rows, LANES), lambda l, j: (l, 0, (first + j + 1) * per))],
        out_specs=pl.BlockSpec((None, None, rows, d), lambda l, j: (l, j, 0, 0)),
        compiler_params=_cparams(2),
        name="realign",
    )(w, w)


def _gate_tables(g):
    GR, H = GATE_ROWS, MLSTM_H
    g_t = g.T[:GR, :]
    lf = _log_sigmoid(g_t)
    fwd = lax.broadcasted_iota(jnp.int32, (GR, CHUNK), 0) < 2 * H
    lane = lax.broadcasted_iota(jnp.int32, (GR, CHUNK), 1)

    def scans(x, op, fill):
        left, right = x, x
        k = 1
        while k < CHUNK:
            left = op(left, jnp.where(lane >= k, pltpu.roll(left, k, 1), fill))
            right = op(right, jnp.where(lane < CHUNK - k, pltpu.roll(right, CHUNK - k, 1), fill))
            k *= 2
        return left, right

    pre, suf = scans(lf, jnp.add, 0.0)
    cum = jnp.where(fwd, pre, suf)
    tot = pre + suf - lf
    w = pltpu.roll(g_t, H, 0) - cum
    pm, sm = scans(w, jnp.maximum, -jnp.inf)
    cmax = jnp.where(fwd, pm, sm)
    wmax = jnp.broadcast_to(jnp.max(w, axis=1, keepdims=True), w.shape)
    wt = jnp.concatenate([w, jnp.zeros((LANES - GR, CHUNK), F32)], axis=0).T
    return (cum, tot, cmax, wmax), wt


def _inproj_kernel(x_ref, sh_ref, sc_ref, wqk_ref, wv_ref, wo_ref, wg_ref, wf_ref,
                   bqk_ref, bv_ref, bo_ref, bg_ref, bf_ref, qk_ref, vt_ref, o_ref, f_ref, tab_ref,
                   wt_ref):
    u = (x_ref[...] * (1.0 + sc_ref[...]) + sh_ref[...]).astype(BF16)
    d = D_MODEL
    g = _dot(u, wg_ref[...]) + bg_ref[...]
    for c in range(tab_ref.shape[0]):
        rows = slice(c * CHUNK, (c + 1) * CHUNK)
        tabs, wt = _gate_tables(g[rows, :])
        wt_ref[rows, :] = wt
        for k, tab in enumerate(tabs):
            tab_ref[c, k] = tab
    qk_ref[:, :d] = (_dot(u, wqk_ref[:, :d]) + bqk_ref[:, :d]).astype(BF16)
    qk_ref[:, d:] = ((_dot(u, wqk_ref[:, d:]) + bqk_ref[:, d:]) * (MLSTM_DH ** -0.5)).astype(BF16)
    v_t = (_dot(u, wv_ref[...]) + bv_ref[...]).T
    for c in range(vt_ref.shape[0]):
        vt_ref[c] = v_t[:, c * CHUNK:(c + 1) * CHUNK].astype(BF16)
    o_ref[...] = _dot(u, wo_ref[...]) + bo_ref[...]
    f_ref[...] = _dot(u, wf_ref[...]) + bf_ref[...]


def _layer_window(layer, rows, width, k):
    return pl.BlockSpec((None, rows, width), lambda *_: (layer, 0, k), pipeline_mode=pl.Buffered(1))


def _inproj(x2d, mods, seq_len, layer, wts):
    n = x2d.shape[0]
    tm = 512
    per_mod = max(seq_len // tm, 1) if mods.shape[0] > 1 else n
    mrow = (lambda i: i // per_mod) if mods.shape[0] > 1 else (lambda i: 0)
    d = D_MODEL
    wspec = functools.partial(_layer_window, layer)
    resident4 = lambda rows: pl.BlockSpec((None, None, rows, d), lambda i: (layer, 0, 0, 0),
                                          pipeline_mode=pl.Buffered(1))
    specs = lambda rows: [wspec(rows, 2 * d, 0), wspec(rows, d, 2), wspec(rows, d, 3),
                          wspec(rows, LANES, OFF_GATES // LANES), resident4(rows)]
    return pl.pallas_call(
        _inproj_kernel,
        out_shape=(jax.ShapeDtypeStruct((n, 2 * d), BF16),
                   jax.ShapeDtypeStruct((n // CHUNK, d, CHUNK), BF16),
                   jax.ShapeDtypeStruct((n, d), F32),
                   jax.ShapeDtypeStruct((n, d), F32),
                   jax.ShapeDtypeStruct((n // CHUNK, 4, GATE_ROWS, CHUNK), F32),
                   jax.ShapeDtypeStruct((n, LANES), F32)),
        grid=(n // tm,),
        in_specs=[pl.BlockSpec((tm, d), lambda i: (i, 0)),
                  pl.BlockSpec((None, 1, d), lambda i: (mrow(i), 0, 0)),
                  pl.BlockSpec((None, 1, d), lambda i: (mrow(i), 0, 1))]
                 + specs(d) + specs(1),
        out_specs=(pl.BlockSpec((tm, 2 * d), lambda i: (i, 0)),
                   pl.BlockSpec((tm // CHUNK, d, CHUNK), lambda i: (i, 0, 0)),
                   pl.BlockSpec((tm, d), lambda i: (i, 0)),
                   pl.BlockSpec((tm, d), lambda i: (i, 0)),
                   pl.BlockSpec((tm // CHUNK, 4, GATE_ROWS, CHUNK), lambda i: (i, 0, 0, 0)),
                   pl.BlockSpec((tm, LANES), lambda i: (i, 0))),
        compiler_params=_cparams(1),
        name="inproj",
    )(x2d, mods, mods, *([wts['w_in']] * 4), wts['w_fou'], *([wts['b_in']] * 4), wts['b_fou'])


def _mlstm_kernel(*refs, seq_len, seqs, layer, has_init, emit_state, n_prev):
    it = iter(refs)
    qk_ref, vt_ref, tab_ref, wt_ref = (next(it) for _ in range(4))
    if has_init:
        c0_ref, n0_ref, m0_ref = (next(it) for _ in range(3))
    for _ in range(n_prev):
        next(it)
    hm_ref = next(it)
    if emit_state:
        cn_ref, nn_ref, mn_ref = (next(it) for _ in range(3))
    hf_s, hb_s, ct_s, n_s, m_s = it

    nc = seq_len // CHUNK
    ncb = nc * seqs
    H, DH, W = MLSTM_H, MLSTM_DH, MLSTM_W
    HALF = CHUNK // 2
    assert CHUNK == DH
    ri = lax.broadcasted_iota(jnp.int32, (HALF, HALF), 0)
    ci = lax.broadcasted_iota(jnp.int32, (HALF, HALF), 1)
    tri_mask = {False: ri <= ci, True: ri >= ci}

    if has_init:
        for d in range(2):
            for h in range(H):
                ct_s[d, h] = c0_ref[d, h].T
                n_s[d, h] = n0_ref[d, h:h + 1, :]
                m_s[d, h] = jnp.full((1, CHUNK), m0_ref[pl.program_id(0), layer, d, h], F32)
    else:
        m_s[...] = jnp.zeros_like(m_s)
    ones_rows = jnp.ones((16, CHUNK), BF16)

    def chain(c, h, backward, use_state, update_state, q=0):
        static = isinstance(c, int)
        r0 = c * CHUNK if static else pl.multiple_of(c * CHUNK, CHUNK)
        rows = pl.ds(r0, CHUNK)
        d = 2 * q + (1 if backward else 0)
        hc = slice(h * DH, (h + 1) * DH)
        gf = (3 if backward else 1) * H + h
        qc = qk_ref[rows, hc]
        kc = qk_ref[rows, W + h * DH:W + (h + 1) * DH]
        vt1 = jnp.concatenate([vt_ref[c, hc, :], ones_rows], axis=0)
        grow = pl.ds(gf, 1)
        brow = tab_ref[c, 0, grow, :]
        g_tot = tab_ref[c, 1, grow, :]
        cmax = tab_ref[c, 2, grow, :]
        wmax = tab_ref[c, 3, grow, :]
        wb = jnp.broadcast_to(wt_ref[rows, gf:gf + 1], (CHUNK, CHUNK))
        m_prev = m_s[d, h]

        inter = brow + m_prev
        m_t = jnp.maximum(inter, brow + cmax)
        e_row = brow - m_t
        nt_dims = (((1,), (1,)), ((), ()))
        if use_state:
            n_prev = n_s[d, h]
            lhs = jnp.concatenate([kc, jnp.broadcast_to(n_prev.astype(BF16), (16, DH)),
                                   ct_s[d, h].astype(BF16)], axis=0)
            r = lax.dot_general(lhs, qc, nt_dims, preferred_element_type=F32)
            qk_t, nq, num_c = r[:CHUNK], r[CHUNK:CHUNK + 1], r[CHUNK + 16:]
        else:
            qk_t = lax.dot_general(kc, qc, nt_dims, preferred_element_type=F32)

        def quad(si, ti, masked):
            rs, cs = slice(si * HALF, (si + 1) * HALF), slice(ti * HALF, (ti + 1) * HALF)
            arg = wb[rs, cs] + e_row[:, cs]
            if masked:
                arg = jnp.where(tri_mask[backward], arg, -jnp.inf)
            return qk_t[rs, cs] * jnp.exp(arg)

        zero = jnp.zeros((HALF, HALF), F32)
        if backward:
            st = jnp.concatenate([jnp.concatenate([quad(0, 0, True), zero], axis=1),
                                  jnp.concatenate([quad(1, 0, False), quad(1, 1, True)], axis=1)], axis=0)
        else:
            st = jnp.concatenate([jnp.concatenate([quad(0, 0, True), quad(0, 1, False)], axis=1),
                                  jnp.concatenate([zero, quad(1, 1, True)], axis=1)], axis=0)
        r1 = _dot(vt1, st.astype(BF16))
        num, den = r1[:DH], r1[DH:DH + 1]
        if use_state:
            a = jnp.exp(inter - m_t)
            num = a * num_c + num
            den = a * nq + den
        hval = num * (1.0 / jnp.maximum(jnp.abs(den), jnp.exp(-m_t)))
        (hb_s if backward else hf_s)[c, h] = hval

        if update_state:
            m_new = jnp.maximum(g_tot + m_prev, wmax + g_tot)
            wk = jnp.exp(wb + (g_tot - m_new))
            kw = kc * wk.astype(BF16)
            r2 = _dot(vt1, kw)
            kv, nsum = r2[:DH], r2[DH:DH + 1]
            if use_state:
                a_c = jnp.exp(g_tot + m_prev - m_new)
                ct_s[d, h] = a_c * ct_s[d, h] + kv
                n_s[d, h] = a_c * n_prev + nsum
            else:
                ct_s[d, h] = kv
                n_s[d, h] = nsum
            m_s[d, h] = m_new

    def step(j, use_state, update_state):
        for q in range(seqs):
            for h in range(H):
                chain(q * nc + j, h, False, use_state, update_state, q)
                chain(q * nc + nc - 1 - j, h, True, use_state, update_state, q)

    first = 0
    if not has_init:
        step(0, False, True)
        first = 1
    last = nc if emit_state else nc - 1
    if last - first <= 2:
        for j in range(first, last):
            step(j, True, True)
    else:
        def body(j, carry):
            step(j, True, True)
            return carry
        lax.fori_loop(first, last, body, 0)
    if not emit_state:
        step(nc - 1, True, False)

    def finalize(c):
        static = isinstance(c, int)
        rows = pl.ds(c * CHUNK if static else pl.multiple_of(c * CHUNK, CHUNK), CHUNK)
        for h in range(H):
            hc = slice(h * DH, (h + 1) * DH)
            ht = hf_s[c, h] + hb_s[c, h]
            mu = jnp.mean(ht, axis=0, keepdims=True)
            cen = ht - mu
            var = jnp.mean(cen * cen, axis=0, keepdims=True)
            hm_ref[rows, hc] = (cen * lax.rsqrt(var + LN_EPS)).T

    if ncb <= 2:
        for c in range(ncb):
            finalize(c)
    else:
        def fbody(c, carry):
            finalize(c)
            return carry
        lax.fori_loop(0, ncb, fbody, 0)

    if emit_state:
        for q in range(seqs):
            for d in range(2):
                for h in range(H):
                    cn_ref[q, d, h] = ct_s[2 * q + d, h].T
                    nn_ref[q, d, h:h + 1, :] = n_s[2 * q + d, h]
                    mn_ref[q, d, h:h + 1, :] = m_s[2 * q + d, h][:, :LANES]


def _mlstm(qk, vt, tabs, wt, batch, seq_len, layer, init_state, prev_state):
    n = batch * seq_len
    w = MLSTM_W
    H, DH = MLSTM_H, MLSTM_DH
    has_init = init_state is not None
    emit_state = not has_init
    seqs = 2 if (seq_len == CHUNK and not has_init and batch % 2 == 0) else 1
    rows = seq_len * seqs
    ncb = rows // CHUNK
    in_specs = [pl.BlockSpec((rows, 2 * w), lambda b: (b, 0)),
                pl.BlockSpec((ncb, w, CHUNK), lambda b: (b, 0, 0)),
                pl.BlockSpec((ncb, 4, GATE_ROWS, CHUNK), lambda b: (b, 0, 0, 0)),
                pl.BlockSpec((rows, LANES), lambda b: (b, 0))]
    args = [qk, vt, tabs, wt]
    init_specs = [pl.BlockSpec((None, None, 2, H, DH, DH), lambda b: (b, layer, 0, 0, 0, 0)),
                  pl.BlockSpec((None, None, 2, H, DH), lambda b: (b, layer, 0, 0, 0))]
    new_specs = [pl.BlockSpec((seqs, None, 2, H, DH, DH), lambda b: (b, layer, 0, 0, 0, 0)),
                 pl.BlockSpec((seqs, None, 2, H, DH), lambda b: (b, layer, 0, 0, 0)),
                 pl.BlockSpec((seqs, None, 2, H, LANES), lambda b: (b, layer, 0, 0, 0))]
    aliases = {}
    if has_init:
        in_specs += init_specs + [pl.BlockSpec(memory_space=pltpu.SMEM)]
        args += list(init_state)
    out_shape = [jax.ShapeDtypeStruct((n, w), F32)]
    out_specs = [pl.BlockSpec((rows, w), lambda b: (b, 0))]
    if emit_state:
        out_shape += [jax.ShapeDtypeStruct((batch, DEPTH, 2, H, DH, DH), F32),
                      jax.ShapeDtypeStruct((batch, DEPTH, 2, H, DH), F32),
                      jax.ShapeDtypeStruct((batch, DEPTH, 2, H, LANES), F32)]
        out_specs += new_specs
        if prev_state is not None:
            aliases = {len(args) + k: 1 + k for k in range(3)}
            in_specs += [pl.BlockSpec(memory_space=pl.ANY)] * 3
            args += list(prev_state)
    scratch = [pltpu.VMEM((ncb, H, DH, CHUNK), F32),
               pltpu.VMEM((ncb, H, DH, CHUNK), F32),
               pltpu.VMEM((2 * seqs, H, DH, DH), F32),
               pltpu.VMEM((2 * seqs, H, 1, DH), F32),
               pltpu.VMEM((2 * seqs, H, 1, CHUNK), F32)]
    return pl.pallas_call(
        functools.partial(_mlstm_kernel, seq_len=seq_len, seqs=seqs, layer=layer, has_init=has_init,
                          emit_state=emit_state, n_prev=len(aliases)),
        out_shape=tuple(out_shape),
        grid=(batch // seqs,),
        in_specs=in_specs,
        out_specs=tuple(out_specs),
        scratch_shapes=scratch,
        input_output_aliases=aliases,
        compiler_params=_cparams(1),
        name="mlstm_init" if has_init else "mlstm_zero",
    )(*args)


def _dft_tables(seq_len):
    gw = FOURIER_GW
    kc = np.arange(gw)
    ang_c = 2.0 * np.pi * ((kc[:, None] * kc[None, :]) % gw) / gw
    w_chan = np.concatenate([np.cos(ang_c), np.sin(ang_c)], axis=1)
    kt = np.arange(seq_len)
    ang_t = 2.0 * np.pi * ((kt[:, None] * kt[None, :]) % seq_len) / seq_len
    scale = 1.0 / np.sqrt(seq_len * gw)
    return (jnp.asarray(w_chan, dtype=F32).astype(BF16),
            jnp.asarray(np.cos(ang_t) * scale, dtype=F32).astype(BF16),
            jnp.asarray(-np.sin(ang_t) * scale, dtype=F32).astype(BF16))


def _fourier_kernel(x_ref, wc_ref, ct_ref, st_ref, out_ref, *, seq_len):
    gw = FOURIER_GW
    for g in range(FOURIER_G):
        cols = slice(g * gw, (g + 1) * gw)
        a = _dot(x_ref[:, cols].astype(BF16), wc_ref[...])
        for r in range(0, x_ref.shape[0], seq_len):
            a_c = a[r:r + seq_len, :gw].astype(BF16)
            a_s = a[r:r + seq_len, gw:].astype(BF16)
            y = _dot(ct_ref[...], a_c) + _dot(st_ref[...], a_s)
            out_ref[r:r + seq_len, cols] = y.astype(BF16)


def _fourier(xf, batch, seq_len):
    w_chan, c_t, s_t = _dft_tables(seq_len)
    n = batch * seq_len
    tm = max(seq_len, 1024)
    return pl.pallas_call(
        functools.partial(_fourier_kernel, seq_len=seq_len),
        out_shape=jax.ShapeDtypeStruct((n, FOURIER_W), BF16),
        grid=(n // tm,),
        in_specs=[pl.BlockSpec((tm, FOURIER_W), lambda b: (b, 0)),
                  _resident(w_chan.shape), _resident(c_t.shape), _resident(s_t.shape)],
        out_specs=pl.BlockSpec((tm, FOURIER_W), lambda b: (b, 0)),
        compiler_params=_cparams(1),
        name="fourier",
    )(xf, w_chan, c_t, s_t)


def _merge_kernel(*refs, row_len):
    (x_ref, sh_ref, sc_ref, gt_ref, hn_ref, o_ref, hf_ref, wl_ref, bl_ref, ng_ref, cw_ref, cb_ref,
     wm_ref, wf_ref, wc_ref, wo_ref, lg_ref, lb_ref, out_ref) = refs
    half = x_ref.shape[0] // 2
    assert half % row_len == 0
    for r in range(2):
        rows = slice(r * half, (r + 1) * half)
        x = x_ref[rows, :]
        u = (x * (1.0 + sc_ref[...]) + sh_ref[...]).astype(BF16)

        def seg(k):
            return _dot(u, wl_ref[k]) + bl_ref[k]

        uc = seg(1) * seg(2)
        pos = lax.broadcasted_iota(jnp.int32, (half, 1), 0) % row_len
        prev = jnp.where(pos == 0, 0.0, pltpu.roll(uc, 1, 0))
        nxt = jnp.where(pos == row_len - 1, 0.0, pltpu.roll(uc, half - 1, 0))
        yc = prev * cw_ref[0:1, :] + uc * cw_ref[1:2, :] + nxt * cw_ref[2:3, :] + cb_ref[...]
        h_c = (seg(0) * yc).astype(BF16)
        h_m = (_sigmoid(o_ref[rows, :]) * hn_ref[rows, :] * ng_ref[...]).astype(BF16)
        merged = (_sigmoid(seg(3)) * _dot(h_m, wm_ref[...])
                  + _sigmoid(seg(4)) * _dot(hf_ref[rows, :], wf_ref[...])
                  + _sigmoid(seg(5)) * _dot(h_c, wc_ref[...]))
        y = _dot(merged.astype(BF16), wo_ref[...])
        out_ref[rows, :] = _layer_norm(ALPHA * x + gt_ref[...] * y, lg_ref[...], lb_ref[...])


def _merge(x2d, mods, seq_len, row_len, layer, h_n, o, h_f, wts):
    n = x2d.shape[0]
    d = D_MODEL
    tm = 512
    per_mod = seq_len // tm
    mrow = (lambda i: i // per_mod) if mods.shape[0] > 1 else (lambda i: 0)
    tile = lambda i: (i, 0)
    wspec = functools.partial(_layer_window, layer)
    segs = lambda rows: pl.BlockSpec((None, 6, rows, d), lambda i: (layer, 0, 0, 0),
                                     pipeline_mode=pl.Buffered(1))
    return pl.pallas_call(
        functools.partial(_merge_kernel, row_len=row_len),
        out_shape=jax.ShapeDtypeStruct((n, d), F32),
        grid=(n // tm,),
        in_specs=[pl.BlockSpec((tm, d), tile),
                  pl.BlockSpec((None, 1, d), lambda i: (mrow(i), 0, 0)),
                  pl.BlockSpec((None, 1, d), lambda i: (mrow(i), 0, 1)),
                  pl.BlockSpec((None, 1, d), lambda i: (mrow(i), 0, 2)),
                  pl.BlockSpec((tm, d), tile), pl.BlockSpec((tm, d), tile), pl.BlockSpec((tm, d), tile),
                  segs(d), segs(1)]
                 + [wspec(1, d, 0),
                    wspec(3, d, 0), wspec(1, d, 0), wspec(d, d, 0), wspec(d, d, 0), wspec(d, d, 0),
                    wspec(d, d, 0),
                    pl.BlockSpec((None, 1, d), lambda i: (2 * layer, 0, 0)),
                    pl.BlockSpec((None, 1, d), lambda i: (2 * layer, 0, 0))],
        out_specs=pl.BlockSpec((tm, d), tile),
        compiler_params=_cparams(1),
        name="merge",
    )(x2d, mods, mods, mods, h_n, o, h_f, wts['w_loc'], wts['b_loc'], wts['norm_g'], wts['conv_w'],
      wts['conv_b'], wts['w_m'], wts['w_f'], wts['w_c'], wts['w_o'], wts['ln_g'], wts['ln_b'])


def _ffn_kernel(x_ref, sh_ref, sc_ref, gt_ref, wgu_ref, wd_ref, lg_ref, lb_ref, out_ref):
    half = x_ref.shape[0] // 2
    for r in range(2):
        rows = slice(r * half, (r + 1) * half)
        x = x_ref[rows, :]
        u = (x * (1.0 + sc_ref[...]) + sh_ref[...]).astype(BF16)
        a = _dot(u, wgu_ref[:, :D_FF])
        b = _dot(u, wgu_ref[:, D_FF:])
        hidden = (a * _sigmoid(a) * b).astype(BF16)
        y = _dot(hidden, wd_ref[...])
        out_ref[rows, :] = _layer_norm(ALPHA * x + gt_ref[...] * y, lg_ref[...], lb_ref[...])


def _ffn(x2d, mods, seq_len, layer, w_gu_bf, w_d_bf, ln_g, ln_b):
    n = x2d.shape[0]
    d = D_MODEL
    tm = 512
    per_mod = seq_len // tm
    mrow = (lambda i: i // per_mod) if mods.shape[0] > 1 else (lambda i: 0)
    tile = lambda i: (i, 0)
    wspec = functools.partial(_layer_window, layer)
    return pl.pallas_call(
        _ffn_kernel,
        out_shape=jax.ShapeDtypeStruct((n, d), F32),
        grid=(n // tm,),
        in_specs=[pl.BlockSpec((tm, d), tile),
                  pl.BlockSpec((None, 1, d), lambda i: (mrow(i), 0, 3)),
                  pl.BlockSpec((None, 1, d), lambda i: (mrow(i), 0, 4)),
                  pl.BlockSpec((None, 1, d), lambda i: (mrow(i), 0, 5)),
                  wspec(d, 2 * D_FF, 0), wspec(D_FF, d, 0),
                  pl.BlockSpec((None, 1, d), lambda i: (2 * layer + 1, 0, 0)),
                  pl.BlockSpec((None, 1, d), lambda i: (2 * layer + 1, 0, 0))],
        out_specs=pl.BlockSpec((tm, d), tile),
        compiler_params=_cparams(1),
        name="ffn",
    )(x2d, mods, mods, mods, w_gu_bf, w_d_bf, ln_g, ln_b)


def _prepared_weights(w_in, b_in, mlstm_norm_g, conv_w, conv_b, w_br_mlstm, w_br_fourier, w_br_conv,
                      w_out, ln_g, ln_b, w_gate_up, w_down):
    d = D_MODEL
    b_in3 = b_in[:, None, :]
    first_f = OFF_GATES // d
    first_l = (OFF_CONV - WIN_PHASE) // d
    w_in_bf = w_in.astype(BF16)
    return dict(
        w_in=w_in_bf, b_in=b_in3,
        w_fou=_realigned(w_in_bf, first_f, 1), b_fou=b_in3[:, :, OFF_FOURIER:OFF_CONV][:, None],
        w_loc=_realigned(w_in_bf, first_l, 6), b_loc=b_in[:, OFF_CONV:].reshape(DEPTH, 6, 1, d),
        norm_g=mlstm_norm_g[:, None, :], conv_w=conv_w, conv_b=conv_b[:, None, :],
        w_m=w_br_mlstm.astype(BF16), w_f=w_br_fourier.astype(BF16), w_c=w_br_conv.astype(BF16),
        w_o=w_out.astype(BF16),
        ln_g=ln_g.reshape(2 * DEPTH, 1, D_MODEL), ln_b=ln_b.reshape(2 * DEPTH, 1, D_MODEL),
        w_gu=w_gate_up.astype(BF16), w_d=w_down.astype(BF16))


def _trunk_layer(x2d, mods, batch, seq_len, row_len, layer, wts, init_state, prev_state):
    qk, vt, o, xf, tabs, wt = _inproj(x2d, mods, seq_len, layer, wts)
    res = _mlstm(qk, vt, tabs, wt, batch, seq_len, layer, init_state, prev_state)
    h_f = _fourier(xf, batch, seq_len)
    x1 = _merge(x2d, mods, seq_len, row_len, layer, res[0], o, h_f, wts)
    x2 = _ffn(x1, mods, seq_len, layer, wts['w_gu'], wts['w_d'], wts['ln_g'], wts['ln_b'])
    return x2, res[1:]


def kernel(x_prompt, x_sample, state_C, state_n, state_m, c, c_ctx, w_ada, b_ada, w_in, b_in,
           mlstm_norm_g, conv_w, conv_b, w_br_mlstm, w_br_fourier, w_br_conv, w_out, ln_g, ln_b,
           w_gate_up, w_down):
    bp, tp, d = x_prompt.shape
    bs, ts, _ = x_sample.shape
    cond = jnp.concatenate([c_ctx[None, :], c, jnp.zeros((8 - 1 - bs, d), F32)], axis=0)
    mods = _ada_mods(cond, w_ada, b_ada)
    wts = _prepared_weights(w_in, b_in, mlstm_norm_g, conv_w, conv_b, w_br_mlstm, w_br_fourier,
                            w_br_conv, w_out, ln_g, ln_b, w_gate_up, w_down)

    xp = x_prompt.reshape(bp * tp, d)
    xs = _add_pos(x_sample, _grid_pos_embed(ts // GRID_W, x_sample.dtype)).reshape(bs * ts, d)
    new_state = None
    for l in range(DEPTH):
        mods_ctx = mods[l, 0:1].reshape(1, 1, 6 * d)
        mods_smp = mods[l, 1:1 + bs].reshape(bs, 1, 6 * d)
        xp, new_state = _trunk_layer(xp, mods_ctx, bp, tp, tp, l, wts, None, new_state)
        xs, _ = _trunk_layer(xs, mods_smp, bs, ts, GRID_W, l, wts, (state_C, state_n, state_m), None)
    c_new, n_new, m_new = new_state
    return (xp.reshape(bp, tp, d), xs.reshape(bs, ts, d), c_new, n_new, m_new[..., 0])
```

```python
import functools

import numpy as np
import jax
import jax.numpy as jnp
from jax import lax
from jax.experimental import pallas as pl
from jax.experimental.pallas import tpu as pltpu

D_MODEL = 1024
DEPTH = 2
GRID_W = 64
MLSTM_H = 4
MLSTM_DH = 256
MLSTM_W = MLSTM_H * MLSTM_DH
CHUNK = 256
GATE_ROWS = 4 * MLSTM_H
FOURIER_G = 4
FOURIER_GW = 256
FOURIER_W = FOURIER_G * FOURIER_GW
CONV_W = 1024
D_FF = -(-8 * D_MODEL // (3 * 256)) * 256
ALPHA = (2 * DEPTH) ** 0.25
LN_EPS = 1e-5
POS_BASE = 10000.0

OFF_GATES = 4 * MLSTM_W
OFF_FOURIER = OFF_GATES + 4 * MLSTM_H
OFF_CONV = OFF_FOURIER + FOURIER_W
OFF_MERGE = OFF_CONV + 3 * CONV_W
N_IN = OFF_MERGE + 3 * D_MODEL

LANES = 128
WIN_PHASE = OFF_FOURIER % LANES
VMEM_LIMIT = 56 * 1024 * 1024

BF16 = jnp.bfloat16
F32 = jnp.float32


def _cparams(n_axes):
    return pltpu.CompilerParams(dimension_semantics=("arbitrary",) * n_axes,
                                vmem_limit_bytes=VMEM_LIMIT)


def _resident(shape):
    zeros = (0,) * len(shape)
    return pl.BlockSpec(shape, lambda *_: zeros, pipeline_mode=pl.Buffered(1))


def _sigmoid(x):
    return 1.0 / (1.0 + jnp.exp(-x))


def _log_sigmoid(x):
    return jnp.minimum(x, 0.0) - jnp.log1p(jnp.exp(-jnp.abs(x)))


def _layer_norm(x, g, b):
    mu = jnp.mean(x, axis=-1, keepdims=True)
    xc = x - mu
    var = jnp.mean(xc * xc, axis=-1, keepdims=True)
    return xc * lax.rsqrt(var + LN_EPS) * g + b


def _dot(a, b):
    return jnp.dot(a, b, preferred_element_type=F32)


def _ada_kernel(cond_ref, w_ref, b_ref, out_ref):
    cond = cond_ref[...]
    act = (cond * _sigmoid(cond)).astype(BF16)
    out_ref[...] = _dot(act, w_ref[...].astype(BF16)) + b_ref[...]


def _ada_mods(cond, w_ada, b_ada):
    tn = 1536
    nb = 6 * D_MODEL // tn
    return pl.pallas_call(
        _ada_kernel,
        out_shape=jax.ShapeDtypeStruct((DEPTH, 8, 6 * D_MODEL), F32),
        grid=(DEPTH, nb),
        in_specs=[pl.BlockSpec((8, D_MODEL), lambda l, j: (0, 0)),
                  pl.BlockSpec((None, D_MODEL, tn), lambda l, j: (l, 0, j)),
                  pl.BlockSpec((None, 1, tn), lambda l, j: (l, 0, j))],
        out_specs=pl.BlockSpec((None, 8, tn), lambda l, j: (l, 0, j)),
        compiler_params=_cparams(2),
        name="ada_mods",
    )(cond, w_ada, b_ada.reshape(DEPTH, 1, 6 * D_MODEL))


def _add_pos_kernel(x_ref, pos_ref, out_ref):
    out_ref[...] = x_ref[...] + pos_ref[...]


def _add_pos(x, pos):
    b, t, d = x.shape
    return pl.pallas_call(
        _add_pos_kernel,
        out_shape=jax.ShapeDtypeStruct(x.shape, x.dtype),
        grid=(b,),
        in_specs=[pl.BlockSpec((None, t, d), lambda i: (i, 0, 0)),
                  pl.BlockSpec((t, d), lambda i: (0, 0))],
        out_specs=pl.BlockSpec((None, t, d), lambda i: (i, 0, 0)),
        compiler_params=_cparams(1),
        name="add_pos",
    )(x, pos)


def _grid_pos_embed(rows, dtype):
    quarter = D_MODEL // 4
    freqs = (POS_BASE ** (-np.arange(quarter, dtype=np.float32) / quarter)).astype(np.float32)
    r = np.repeat(np.arange(rows, dtype=np.float32), GRID_W)[:, None] * freqs
    col = np.tile(np.arange(GRID_W, dtype=np.float32), rows)[:, None] * freqs
    return jnp.asarray(np.concatenate([np.sin(r), np.cos(r), np.sin(col), np.cos(col)], axis=-1), dtype)


def _realign_kernel(a_ref, b_ref, out_ref):
    out_ref[...] = jnp.concatenate([a_ref[:, WIN_PHASE:], b_ref[:, :WIN_PHASE]], axis=1)


def _realigned(w, first, count):
    depth, rows, _ = w.shape
    d = D_MODEL
    per = d // LANES
    return pl.pallas_call(
        _realign_kernel,
        out_shape=jax.ShapeDtypeStruct((depth, count, rows, d), w.dtype),
        grid=(depth, count),
        in_specs=[pl.BlockSpec((None, rows, d), lambda l, j: (l, 0, first + j)),
                  pl.BlockSpec((None, rows, LANES), lambda l, j: (l, 0, (first + j + 1) * per))],
        out_specs=pl.BlockSpec((None, None, rows, d), lambda l, j: (l, j, 0, 0)),
        compiler_params=_cparams(2),
        name="realign",
    )(w, w)


def _gate_tables(g):
    GR, H = GATE_ROWS, MLSTM_H
    g_t = g.T[:GR, :]
    lf = _log_sigmoid(g_t)
    fwd = lax.broadcasted_iota(jnp.int32, (GR, CHUNK), 0) < 2 * H
    lane = lax.broadcasted_iota(jnp.int32, (GR, CHUNK), 1)

    def scans(x, op, fill):
        left, right = x, x
        k = 1
        while k < CHUNK:
            left = op(left, jnp.where(lane >= k, pltpu.roll(left, k, 1), fill))
            right = op(right, jnp.where(lane < CHUNK - k, pltpu.roll(right, CHUNK - k, 1), fill))
            k *= 2
        return left, right

    pre, suf = scans(lf, jnp.add, 0.0)
    cum = jnp.where(fwd, pre, suf)
    tot = pre + suf - lf
    w = pltpu.roll(g_t, H, 0) - cum
    pm, sm = scans(w, jnp.maximum, -jnp.inf)
    cmax = jnp.where(fwd, pm, sm)
    wmax = jnp.broadcast_to(jnp.max(w, axis=1, keepdims=True), w.shape)
    wt = jnp.concatenate([w, jnp.zeros((LANES - GR, CHUNK), F32)], axis=0).T
    return (cum, tot, cmax, wmax), wt


def _inproj_kernel(*refs, n_side):
    (x_ref, sh_ref, sc_ref, wqk_ref, wv_ref, wo_ref, wg_ref, wf_ref,
     bqk_ref, bv_ref, bo_ref, bg_ref, bf_ref) = refs[:13]
    side_in = refs[13:13 + n_side]
    qk_ref, vt_ref, o_ref, f_ref, tab_ref, wt_ref = refs[13 + n_side:19 + n_side]
    side_out = refs[19 + n_side:]
    for src_ref, dst_ref in zip(side_in, side_out):
        dst_ref[...] = src_ref[...].astype(BF16)

    u = (x_ref[...] * (1.0 + sc_ref[...]) + sh_ref[...]).astype(BF16)
    d = D_MODEL
    g = _dot(u, wg_ref[...]) + bg_ref[...]
    for c in range(tab_ref.shape[0]):
        rows = slice(c * CHUNK, (c + 1) * CHUNK)
        tabs, wt = _gate_tables(g[rows, :])
        wt_ref[rows, :] = wt
        for k, tab in enumerate(tabs):
            tab_ref[c, k] = tab
    qk_ref[:, :d] = (_dot(u, wqk_ref[:, :d]) + bqk_ref[:, :d]).astype(BF16)
    qk_ref[:, d:] = ((_dot(u, wqk_ref[:, d:]) + bqk_ref[:, d:]) * (MLSTM_DH ** -0.5)).astype(BF16)
    v_t = (_dot(u, wv_ref[...]) + bv_ref[...]).T
    for c in range(vt_ref.shape[0]):
        vt_ref[c] = v_t[:, c * CHUNK:(c + 1) * CHUNK].astype(BF16)
    o_ref[...] = _dot(u, wo_ref[...]) + bo_ref[...]
    f_ref[...] = _dot(u, wf_ref[...]) + bf_ref[...]


def _layer_window(layer, rows, width, k):
    return pl.BlockSpec((None, rows, width), lambda *_: (layer, 0, k), pipeline_mode=pl.Buffered(1))


def _inproj(x2d, mods, seq_len, layer, wts, side=()):
    n = x2d.shape[0]
    tm = 512
    steps = n // tm
    side_rows = [a.shape[1] // steps for a in side]
    assert all(r * steps == a.shape[1] and r % 16 == 0 for r, a in zip(side_rows, side))
    per_mod = max(seq_len // tm, 1) if mods.shape[0] > 1 else n
    mrow = (lambda i: i // per_mod) if mods.shape[0] > 1 else (lambda i: 0)
    d = D_MODEL
    wspec = functools.partial(_layer_window, layer)
    resident4 = lambda rows: pl.BlockSpec((None, None, rows, d), lambda i: (layer, 0, 0, 0),
                                          pipeline_mode=pl.Buffered(1))
    specs = lambda rows: [wspec(rows, 2 * d, 0), wspec(rows, d, 2), wspec(rows, d, 3),
                          wspec(rows, LANES, OFF_GATES // LANES), resident4(rows)]
    return pl.pallas_call(
        functools.partial(_inproj_kernel, n_side=len(side)),
        out_shape=(jax.ShapeDtypeStruct((n, 2 * d), BF16),
                   jax.ShapeDtypeStruct((n // CHUNK, d, CHUNK), BF16),
                   jax.ShapeDtypeStruct((n, d), F32),
                   jax.ShapeDtypeStruct((n, d), F32),
                   jax.ShapeDtypeStruct((n // CHUNK, 4, GATE_ROWS, CHUNK), F32),
                   jax.ShapeDtypeStruct((n, LANES), F32))
                  + tuple(jax.ShapeDtypeStruct(a.shape[1:], BF16) for a in side),
        grid=(steps,),
        in_specs=[pl.BlockSpec((tm, d), lambda i: (i, 0)),
                  pl.BlockSpec((None, 1, d), lambda i: (mrow(i), 0, 0)),
                  pl.BlockSpec((None, 1, d), lambda i: (mrow(i), 0, 1))]
                 + specs(d) + specs(1)
                 + [pl.BlockSpec((None, r, a.shape[2]), lambda i: (layer, i, 0))
                    for r, a in zip(side_rows, side)],
        out_specs=(pl.BlockSpec((tm, 2 * d), lambda i: (i, 0)),
                   pl.BlockSpec((tm // CHUNK, d, CHUNK), lambda i: (i, 0, 0)),
                   pl.BlockSpec((tm, d), lambda i: (i, 0)),
                   pl.BlockSpec((tm, d), lambda i: (i, 0)),
                   pl.BlockSpec((tm // CHUNK, 4, GATE_ROWS, CHUNK), lambda i: (i, 0, 0, 0)),
                   pl.BlockSpec((tm, LANES), lambda i: (i, 0)))
                  + tuple(pl.BlockSpec((r, a.shape[2]), lambda i: (i, 0)) for r, a in zip(side_rows, side)),
        compiler_params=_cparams(1),
        name="inproj",
    )(x2d, mods, mods, *([wts['w_in']] * 4), wts['w_fou'], *([wts['b_in']] * 4), wts['b_fou'], *side)


def _mlstm_kernel(*refs, seq_len, seqs, layer, has_init, emit_state, n_prev):
    it = iter(refs)
    qk_ref, vt_ref, tab_ref, wt_ref = (next(it) for _ in range(4))
    if has_init:
        c0_ref, n0_ref, m0_ref = (next(it) for _ in range(3))
    for _ in range(n_prev):
        next(it)
    hm_ref = next(it)
    if emit_state:
        cn_ref, nn_ref, mn_ref = (next(it) for _ in range(3))
    hf_s, hb_s, ct_s, n_s, m_s = it

    nc = seq_len // CHUNK
    ncb = nc * seqs
    H, DH, W = MLSTM_H, MLSTM_DH, MLSTM_W
    HALF = CHUNK // 2
    assert CHUNK == DH
    ri = lax.broadcasted_iota(jnp.int32, (HALF, HALF), 0)
    ci = lax.broadcasted_iota(jnp.int32, (HALF, HALF), 1)
    tri_mask = {False: ri <= ci, True: ri >= ci}

    if has_init:
        for d in range(2):
            for h in range(H):
                ct_s[d, h] = c0_ref[d, h].T
                n_s[d, h] = n0_ref[d, h:h + 1, :]
                m_s[d, h] = jnp.full((1, CHUNK), m0_ref[pl.program_id(0), layer, d, h], F32)
    else:
        m_s[...] = jnp.zeros_like(m_s)
    ones_rows = jnp.ones((16, CHUNK), BF16)

    def chain(c, h, backward, use_state, update_state, q=0):
        static = isinstance(c, int)
        r0 = c * CHUNK if static else pl.multiple_of(c * CHUNK, CHUNK)
        rows = pl.ds(r0, CHUNK)
        d = 2 * q + (1 if backward else 0)
        hc = slice(h * DH, (h + 1) * DH)
        gf = (3 if backward else 1) * H + h
        qc = qk_ref[rows, hc]
        kc = qk_ref[rows, W + h * DH:W + (h + 1) * DH]
        vt1 = jnp.concatenate([vt_ref[c, hc, :], ones_rows], axis=0)
        grow = pl.ds(gf, 1)
        brow = tab_ref[c, 0, grow, :]
        g_tot = tab_ref[c, 1, grow, :]
        cmax = tab_ref[c, 2, grow, :]
        wmax = tab_ref[c, 3, grow, :]
        wb = jnp.broadcast_to(wt_ref[rows, gf:gf + 1], (CHUNK, CHUNK))
        m_prev = m_s[d, h]

        inter = brow + m_prev
        m_t = jnp.maximum(inter, brow + cmax)
        e_row = brow - m_t
        nt_dims = (((1,), (1,)), ((), ()))
        if use_state:
            n_prev = n_s[d, h]
            lhs = jnp.concatenate([kc, jnp.broadcast_to(n_prev.astype(BF16), (16, DH)),
                                   ct_s[d, h].astype(BF16)], axis=0)
            r = lax.dot_general(lhs, qc, nt_dims, preferred_element_type=F32)
            qk_t, nq, num_c = r[:CHUNK], r[CHUNK:CHUNK + 1], r[CHUNK + 16:]
        else:
            qk_t = lax.dot_general(kc, qc, nt_dims, preferred_element_type=F32)

        def quad(si, ti, masked):
            rs, cs = slice(si * HALF, (si + 1) * HALF), slice(ti * HALF, (ti + 1) * HALF)
            arg = wb[rs, cs] + e_row[:, cs]
            if masked:
                arg = jnp.where(tri_mask[backward], arg, -jnp.inf)
            return qk_t[rs, cs] * jnp.exp(arg)

        zero = jnp.zeros((HALF, HALF), F32)
        if backward:
            st = jnp.concatenate([jnp.concatenate([quad(0, 0, True), zero], axis=1),
                                  jnp.concatenate([quad(1, 0, False), quad(1, 1, True)], axis=1)], axis=0)
        else:
            st = jnp.concatenate([jnp.concatenate([quad(0, 0, True), quad(0, 1, False)], axis=1),
                                  jnp.concatenate([zero, quad(1, 1, True)], axis=1)], axis=0)
        r1 = _dot(vt1, st.astype(BF16))
        num, den = r1[:DH], r1[DH:DH + 1]
        if use_state:
            a = jnp.exp(inter - m_t)
            num = a * num_c + num
            den = a * nq + den
        hval = num * (1.0 / jnp.maximum(jnp.abs(den), jnp.exp(-m_t)))
        (hb_s if backward else hf_s)[c, h] = hval

        if update_state:
            m_new = jnp.maximum(g_tot + m_prev, wmax + g_tot)
            wk = jnp.exp(wb + (g_tot - m_new))
            kw = kc * wk.astype(BF16)
            r2 = _dot(vt1, kw)
            kv, nsum = r2[:DH], r2[DH:DH + 1]
            if use_state:
                a_c = jnp.exp(g_tot + m_prev - m_new)
                ct_s[d, h] = a_c * ct_s[d, h] + kv
                n_s[d, h] = a_c * n_prev + nsum
            else:
                ct_s[d, h] = kv
                n_s[d, h] = nsum
            m_s[d, h] = m_new

    def step(j, use_state, update_state):
        for q in range(seqs):
            for h in range(H):
                chain(q * nc + j, h, False, use_state, update_state, q)
                chain(q * nc + nc - 1 - j, h, True, use_state, update_state, q)

    first = 0
    if not has_init:
        step(0, False, True)
        first = 1
    last = nc if emit_state else nc - 1
    if last - first <= 2:
        for j in range(first, last):
            step(j, True, True)
    else:
        def body(j, carry):
            step(j, True, True)
            return carry
        lax.fori_loop(first, last, body, 0)
    if not emit_state:
        step(nc - 1, True, False)

    def finalize(c):
        static = isinstance(c, int)
        rows = pl.ds(c * CHUNK if static else pl.multiple_of(c * CHUNK, CHUNK), CHUNK)
        for h in range(H):
            hc = slice(h * DH, (h + 1) * DH)
            ht = hf_s[c, h] + hb_s[c, h]
            mu = jnp.mean(ht, axis=0, keepdims=True)
            cen = ht - mu
            var = jnp.mean(cen * cen, axis=0, keepdims=True)
            hm_ref[rows, hc] = (cen * lax.rsqrt(var + LN_EPS)).T

    if ncb <= 2:
        for c in range(ncb):
            finalize(c)
    else:
        def fbody(c, carry):
            finalize(c)
            return carry
        lax.fori_loop(0, ncb, fbody, 0)

    if emit_state:
        for q in range(seqs):
            for d in range(2):
                for h in range(H):
                    cn_ref[q, d, h] = ct_s[2 * q + d, h].T
                    nn_ref[q, d, h:h + 1, :] = n_s[2 * q + d, h]
                    mn_ref[q, d, h:h + 1, :] = m_s[2 * q + d, h][:, :LANES]


def _mlstm(qk, vt, tabs, wt, batch, seq_len, layer, init_state, prev_state):
    n = batch * seq_len
    w = MLSTM_W
    H, DH = MLSTM_H, MLSTM_DH
    has_init = init_state is not None
    emit_state = not has_init
    seqs = 2 if (seq_len == CHUNK and not has_init and batch % 2 == 0) else 1
    rows = seq_len * seqs
    ncb = rows // CHUNK
    in_specs = [pl.BlockSpec((rows, 2 * w), lambda b: (b, 0)),
                pl.BlockSpec((ncb, w, CHUNK), lambda b: (b, 0, 0)),
                pl.BlockSpec((ncb, 4, GATE_ROWS, CHUNK), lambda b: (b, 0, 0, 0)),
                pl.BlockSpec((rows, LANES), lambda b: (b, 0))]
    args = [qk, vt, tabs, wt]
    init_specs = [pl.BlockSpec((None, None, 2, H, DH, DH), lambda b: (b, layer, 0, 0, 0, 0)),
                  pl.BlockSpec((None, None, 2, H, DH), lambda b: (b, layer, 0, 0, 0))]
    new_specs = [pl.BlockSpec((seqs, None, 2, H, DH, DH), lambda b: (b, layer, 0, 0, 0, 0)),
                 pl.BlockSpec((seqs, None, 2, H, DH), lambda b: (b, layer, 0, 0, 0)),
                 pl.BlockSpec((seqs, None, 2, H, LANES), lambda b: (b, layer, 0, 0, 0))]
    aliases = {}
    if has_init:
        in_specs += init_specs + [pl.BlockSpec(memory_space=pltpu.SMEM)]
        args += list(init_state)
    out_shape = [jax.ShapeDtypeStruct((n, w), F32)]
    out_specs = [pl.BlockSpec((rows, w), lambda b: (b, 0))]
    if emit_state:
        out_shape += [jax.ShapeDtypeStruct((batch, DEPTH, 2, H, DH, DH), F32),
                      jax.ShapeDtypeStruct((batch, DEPTH, 2, H, DH), F32),
                      jax.ShapeDtypeStruct((batch, DEPTH, 2, H, LANES), F32)]
        out_specs += new_specs
        if prev_state is not None:
            aliases = {len(args) + k: 1 + k for k in range(3)}
            in_specs += [pl.BlockSpec(memory_space=pl.ANY)] * 3
            args += list(prev_state)
    scratch = [pltpu.VMEM((ncb, H, DH, CHUNK), F32),
               pltpu.VMEM((ncb, H, DH, CHUNK), F32),
               pltpu.VMEM((2 * seqs, H, DH, DH), F32),
               pltpu.VMEM((2 * seqs, H, 1, DH), F32),
               pltpu.VMEM((2 * seqs, H, 1, CHUNK), F32)]
    return pl.pallas_call(
        functools.partial(_mlstm_kernel, seq_len=seq_len, seqs=seqs, layer=layer, has_init=has_init,
                          emit_state=emit_state, n_prev=len(aliases)),
        out_shape=tuple(out_shape),
        grid=(batch // seqs,),
        in_specs=in_specs,
        out_specs=tuple(out_specs),
        scratch_shapes=scratch,
        input_output_aliases=aliases,
        compiler_params=_cparams(1),
        name="mlstm_init" if has_init else "mlstm_zero",
    )(*args)


def _dft_tables(seq_len):
    gw = FOURIER_GW
    kc = np.arange(gw)
    ang_c = 2.0 * np.pi * ((kc[:, None] * kc[None, :]) % gw) / gw
    w_chan = np.concatenate([np.cos(ang_c), np.sin(ang_c)], axis=1)
    kt = np.arange(seq_len)
    ang_t = 2.0 * np.pi * ((kt[:, None] * kt[None, :]) % seq_len) / seq_len
    scale = 1.0 / np.sqrt(seq_len * gw)
    return (jnp.asarray(w_chan, dtype=F32).astype(BF16),
            jnp.asarray(np.cos(ang_t) * scale, dtype=F32).astype(BF16),
            jnp.asarray(-np.sin(ang_t) * scale, dtype=F32).astype(BF16))


def _fourier_kernel(x_ref, wc_ref, ct_ref, st_ref, out_ref, *, seq_len):
    gw = FOURIER_GW
    for g in range(FOURIER_G):
        cols = slice(g * gw, (g + 1) * gw)
        a = _dot(x_ref[:, cols].astype(BF16), wc_ref[...])
        for r in range(0, x_ref.shape[0], seq_len):
            a_c = a[r:r + seq_len, :gw].astype(BF16)
            a_s = a[r:r + seq_len, gw:].astype(BF16)
            y = _dot(ct_ref[...], a_c) + _dot(st_ref[...], a_s)
            out_ref[r:r + seq_len, cols] = y.astype(BF16)


def _fourier(xf, batch, seq_len):
    w_chan, c_t, s_t = _dft_tables(seq_len)
    n = batch * seq_len
    tm = max(seq_len, 1024)
    return pl.pallas_call(
        functools.partial(_fourier_kernel, seq_len=seq_len),
        out_shape=jax.ShapeDtypeStruct((n, FOURIER_W), BF16),
        grid=(n // tm,),
        in_specs=[pl.BlockSpec((tm, FOURIER_W), lambda b: (b, 0)),
                  _resident(w_chan.shape), _resident(c_t.shape), _resident(s_t.shape)],
        out_specs=pl.BlockSpec((tm, FOURIER_W), lambda b: (b, 0)),
        compiler_params=_cparams(1),
        name="fourier",
    )(xf, w_chan, c_t, s_t)


def _merge_kernel(*refs, row_len):
    (x_ref, sh_ref, sc_ref, gt_ref, hn_ref, o_ref, hf_ref, wl_ref, bl_ref, ng_ref, cw_ref, cb_ref,
     wm_ref, wf_ref, wc_ref, wo_ref, lg_ref, lb_ref, out_ref) = refs
    half = x_ref.shape[0] // 2
    assert half % row_len == 0
    for r in range(2):
        rows = slice(r * half, (r + 1) * half)
        x = x_ref[rows, :]
        u = (x * (1.0 + sc_ref[...]) + sh_ref[...]).astype(BF16)

        def seg(k):
            return _dot(u, wl_ref[k]) + bl_ref[k]

        uc = seg(1) * seg(2)
        pos = lax.broadcasted_iota(jnp.int32, (half, 1), 0) % row_len
        prev = jnp.where(pos == 0, 0.0, pltpu.roll(uc, 1, 0))
        nxt = jnp.where(pos == row_len - 1, 0.0, pltpu.roll(uc, half - 1, 0))
        yc = prev * cw_ref[0:1, :] + uc * cw_ref[1:2, :] + nxt * cw_ref[2:3, :] + cb_ref[...]
        h_c = (seg(0) * yc).astype(BF16)
        h_m = (_sigmoid(o_ref[rows, :]) * hn_ref[rows, :] * ng_ref[...]).astype(BF16)
        merged = (_sigmoid(seg(3)) * _dot(h_m, wm_ref[...])
                  + _sigmoid(seg(4)) * _dot(hf_ref[rows, :], wf_ref[...])
                  + _sigmoid(seg(5)) * _dot(h_c, wc_ref[...]))
        y = _dot(merged.astype(BF16), wo_ref[...])
        out_ref[rows, :] = _layer_norm(ALPHA * x + gt_ref[...] * y, lg_ref[...], lb_ref[...])


def _merge(x2d, mods, seq_len, row_len, layer, h_n, o, h_f, wts, branch_w):
    n = x2d.shape[0]
    d = D_MODEL
    tm = 512
    per_mod = seq_len // tm
    mrow = (lambda i: i // per_mod) if mods.shape[0] > 1 else (lambda i: 0)
    tile = lambda i: (i, 0)
    wspec = functools.partial(_layer_window, layer)
    segs = lambda rows: pl.BlockSpec((None, 6, rows, d), lambda i: (layer, 0, 0, 0),
                                     pipeline_mode=pl.Buffered(1))
    return pl.pallas_call(
        functools.partial(_merge_kernel, row_len=row_len),
        out_shape=jax.ShapeDtypeStruct((n, d), F32),
        grid=(n // tm,),
        in_specs=[pl.BlockSpec((tm, d), tile),
                  pl.BlockSpec((None, 1, d), lambda i: (mrow(i), 0, 0)),
                  pl.BlockSpec((None, 1, d), lambda i: (mrow(i), 0, 1)),
                  pl.BlockSpec((None, 1, d), lambda i: (mrow(i), 0, 2)),
                  pl.BlockSpec((tm, d), tile), pl.BlockSpec((tm, d), tile), pl.BlockSpec((tm, d), tile),
                  segs(d), segs(1)]
                 + [wspec(1, d, 0), wspec(3, d, 0), wspec(1, d, 0)]
                 + [_resident(w.shape) for w in branch_w]
                 + [pl.BlockSpec((None, 1, d), lambda i: (2 * layer, 0, 0)),
                    pl.BlockSpec((None, 1, d), lambda i: (2 * layer, 0, 0))],
        out_specs=pl.BlockSpec((tm, d), tile),
        compiler_params=_cparams(1),
        name="merge",
    )(x2d, mods, mods, mods, h_n, o, h_f, wts['w_loc'], wts['b_loc'], wts['norm_g'], wts['conv_w'],
      wts['conv_b'], *branch_w, wts['ln_g'], wts['ln_b'])


def _ffn_kernel(x_ref, sh_ref, sc_ref, gt_ref, wgu_ref, wd_ref, lg_ref, lb_ref, out_ref):
    half = x_ref.shape[0] // 2
    for r in range(2):
        rows = slice(r * half, (r + 1) * half)
        x = x_ref[rows, :]
        u = (x * (1.0 + sc_ref[...]) + sh_ref[...]).astype(BF16)
        a = _dot(u, wgu_ref[:, :D_FF])
        b = _dot(u, wgu_ref[:, D_FF:])
        hidden = (a * _sigmoid(a) * b).astype(BF16)
        y = _dot(hidden, wd_ref[...])
        out_ref[rows, :] = _layer_norm(ALPHA * x + gt_ref[...] * y, lg_ref[...], lb_ref[...])


def _ffn(x2d, mods, seq_len, layer, w_gu_bf, w_d_bf, ln_g, ln_b):
    n = x2d.shape[0]
    d = D_MODEL
    tm = 512
    per_mod = seq_len // tm
    mrow = (lambda i: i // per_mod) if mods.shape[0] > 1 else (lambda i: 0)
    tile = lambda i: (i, 0)
    return pl.pallas_call(
        _ffn_kernel,
        out_shape=jax.ShapeDtypeStruct((n, d), F32),
        grid=(n // tm,),
        in_specs=[pl.BlockSpec((tm, d), tile),
                  pl.BlockSpec((None, 1, d), lambda i: (mrow(i), 0, 3)),
                  pl.BlockSpec((None, 1, d), lambda i: (mrow(i), 0, 4)),
                  pl.BlockSpec((None, 1, d), lambda i: (mrow(i), 0, 5)),
                  _resident(w_gu_bf.shape), _resident(w_d_bf.shape),
                  pl.BlockSpec((None, 1, d), lambda i: (2 * layer + 1, 0, 0)),
                  pl.BlockSpec((None, 1, d), lambda i: (2 * layer + 1, 0, 0))],
        out_specs=pl.BlockSpec((tm, d), tile),
        compiler_params=_cparams(1),
        name="ffn",
    )(x2d, mods, mods, mods, w_gu_bf, w_d_bf, ln_g, ln_b)


def _prepared_weights(w_in, b_in, mlstm_norm_g, conv_w, conv_b, w_br_mlstm, w_br_fourier, w_br_conv,
                      w_out, ln_g, ln_b, w_gate_up, w_down):
    d = D_MODEL
    b_in3 = b_in[:, None, :]
    first_f = OFF_GATES // d
    first_l = (OFF_CONV - WIN_PHASE) // d
    w_in_bf = w_in.astype(BF16)
    return dict(
        w_in=w_in_bf, b_in=b_in3,
        w_fou=_realigned(w_in_bf, first_f, 1), b_fou=b_in3[:, :, OFF_FOURIER:OFF_CONV][:, None],
        w_loc=_realigned(w_in_bf, first_l, 6), b_loc=b_in[:, OFF_CONV:].reshape(DEPTH, 6, 1, d),
        norm_g=mlstm_norm_g[:, None, :], conv_w=conv_w, conv_b=conv_b[:, None, :],
        ln_g=ln_g.reshape(2 * DEPTH, 1, D_MODEL), ln_b=ln_b.reshape(2 * DEPTH, 1, D_MODEL),
        late=(w_br_mlstm, w_br_fourier, w_br_conv, w_out, w_gate_up, w_down))


def _trunk_layer(x2d, mods, batch, seq_len, row_len, layer, wts, init_state, prev_state, late_bf):
    res_in = _inproj(x2d, mods, seq_len, layer, wts, () if late_bf is not None else wts['late'])
    qk, vt, o, xf, tabs, wt = res_in[:6]
    if late_bf is None:
        late_bf = res_in[6:]
    res = _mlstm(qk, vt, tabs, wt, batch, seq_len, layer, init_state, prev_state)
    h_f = _fourier(xf, batch, seq_len)
    x1 = _merge(x2d, mods, seq_len, row_len, layer, res[0], o, h_f, wts, late_bf[:4])
    x2 = _ffn(x1, mods, seq_len, layer, late_bf[4], late_bf[5], wts['ln_g'], wts['ln_b'])
    return x2, res[1:], late_bf


def kernel(x_prompt, x_sample, state_C, state_n, state_m, c, c_ctx, w_ada, b_ada, w_in, b_in,
           mlstm_norm_g, conv_w, conv_b, w_br_mlstm, w_br_fourier, w_br_conv, w_out, ln_g, ln_b,
           w_gate_up, w_down):
    bp, tp, d = x_prompt.shape
    bs, ts, _ = x_sample.shape
    cond = jnp.concatenate([c_ctx[None, :], c, jnp.zeros((8 - 1 - bs, d), F32)], axis=0)
    mods = _ada_mods(cond, w_ada, b_ada)
    wts = _prepared_weights(w_in, b_in, mlstm_norm_g, conv_w, conv_b, w_br_mlstm, w_br_fourier,
                            w_br_conv, w_out, ln_g, ln_b, w_gate_up, w_down)

    xp = x_prompt.reshape(bp * tp, d)
    xs = _add_pos(x_sample, _grid_pos_embed(ts // GRID_W, x_sample.dtype)).reshape(bs * ts, d)
    new_state = None
    for l in range(DEPTH):
        mods_ctx = mods[l, 0:1].reshape(1, 1, 6 * d)
        mods_smp = mods[l, 1:1 + bs].reshape(bs, 1, 6 * d)
        xp, new_state, late_bf = _trunk_layer(xp, mods_ctx, bp, tp, tp, l, wts, None, new_state, None)
        xs, _, _ = _trunk_layer(xs, mods_smp, bs, ts, GRID_W, l, wts, (state_C, state_n, state_m), None,
                                late_bf)
    c_new, n_new, m_new = new_state
    return (xp.reshape(bp, tp, d), xs.reshape(bs, ts, d), c_new, n_new, m_new[..., 0])
```

```python
import functools

import numpy as np
import jax
import jax.numpy as jnp
from jax import lax
from jax.experimental import pallas as pl
from jax.experimental.pallas import tpu as pltpu

D_MODEL = 1024
DEPTH = 2
GRID_W = 64
MLSTM_H = 4
MLSTM_DH = 256
MLSTM_W = MLSTM_H * MLSTM_DH
CHUNK = 256
GATE_ROWS = 4 * MLSTM_H
FOURIER_G = 4
FOURIER_GW = 256
FOURIER_W = FOURIER_G * FOURIER_GW
CONV_W = 1024
D_FF = -(-8 * D_MODEL // (3 * 256)) * 256
ALPHA = (2 * DEPTH) ** 0.25
LN_EPS = 1e-5
POS_BASE = 10000.0

OFF_GATES = 4 * MLSTM_W
OFF_FOURIER = OFF_GATES + 4 * MLSTM_H
OFF_CONV = OFF_FOURIER + FOURIER_W
OFF_MERGE = OFF_CONV + 3 * CONV_W
N_IN = OFF_MERGE + 3 * D_MODEL

LANES = 128
WIN_PHASE = OFF_FOURIER % LANES
N_LOC = 6
VMEM_LIMIT = 56 * 1024 * 1024

BF16 = jnp.bfloat16
F32 = jnp.float32


def _cparams(n_axes):
    return pltpu.CompilerParams(dimension_semantics=("arbitrary",) * n_axes,
                                vmem_limit_bytes=VMEM_LIMIT)


def _resident(shape):
    zeros = (0,) * len(shape)
    return pl.BlockSpec(shape, lambda *_: zeros, pipeline_mode=pl.Buffered(1))


def _sigmoid(x):
    return 1.0 / (1.0 + jnp.exp(-x))


def _log_sigmoid(x):
    return jnp.minimum(x, 0.0) - jnp.log1p(jnp.exp(-jnp.abs(x)))


def _layer_norm(x, g, b):
    mu = jnp.mean(x, axis=-1, keepdims=True)
    xc = x - mu
    var = jnp.mean(xc * xc, axis=-1, keepdims=True)
    return xc * lax.rsqrt(var + LN_EPS) * g + b


def _dot(a, b):
    return jnp.dot(a, b, preferred_element_type=F32)


def _ada_kernel(cond_ref, w_ref, b_ref, out_ref):
    cond = cond_ref[...]
    act = (cond * _sigmoid(cond)).astype(BF16)
    out_ref[...] = _dot(act, w_ref[...].astype(BF16)) + b_ref[...]


def _ada_mods(cond, w_ada, b_ada):
    tn = 1536
    nb = 6 * D_MODEL // tn
    return pl.pallas_call(
        _ada_kernel,
        out_shape=jax.ShapeDtypeStruct((DEPTH, 8, 6 * D_MODEL), F32),
        grid=(DEPTH, nb),
        in_specs=[pl.BlockSpec((8, D_MODEL), lambda l, j: (0, 0)),
                  pl.BlockSpec((None, D_MODEL, tn), lambda l, j: (l, 0, j)),
                  pl.BlockSpec((None, 1, tn), lambda l, j: (l, 0, j))],
        out_specs=pl.BlockSpec((None, 8, tn), lambda l, j: (l, 0, j)),
        compiler_params=_cparams(2),
        name="ada_mods",
    )(cond, w_ada, b_ada.reshape(DEPTH, 1, 6 * D_MODEL))


def _add_pos_kernel(x_ref, pos_ref, out_ref):
    out_ref[...] = x_ref[...] + pos_ref[...]


def _add_pos(x, pos):
    b, t, d = x.shape
    return pl.pallas_call(
        _add_pos_kernel,
        out_shape=jax.ShapeDtypeStruct(x.shape, x.dtype),
        grid=(b,),
        in_specs=[pl.BlockSpec((None, t, d), lambda i: (i, 0, 0)),
                  pl.BlockSpec((t, d), lambda i: (0, 0))],
        out_specs=pl.BlockSpec((None, t, d), lambda i: (i, 0, 0)),
        compiler_params=_cparams(1),
        name="add_pos",
    )(x, pos)


def _grid_pos_embed(rows, dtype):
    quarter = D_MODEL // 4
    freqs = (POS_BASE ** (-np.arange(quarter, dtype=np.float32) / quarter)).astype(np.float32)
    r = np.repeat(np.arange(rows, dtype=np.float32), GRID_W)[:, None] * freqs
    col = np.tile(np.arange(GRID_W, dtype=np.float32), rows)[:, None] * freqs
    return jnp.asarray(np.concatenate([np.sin(r), np.cos(r), np.sin(col), np.cos(col)], axis=-1), dtype)


def _realign_kernel(a_ref, b_ref, out_ref):
    out_ref[...] = jnp.concatenate([a_ref[:, WIN_PHASE:], b_ref[:, :WIN_PHASE]], axis=1)


def _realigned(w, first, count):
    depth, rows, _ = w.shape
    d = D_MODEL
    per = d // LANES
    return pl.pallas_call(
        _realign_kernel,
        out_shape=jax.ShapeDtypeStruct((depth, count, rows, d), w.dtype),
        grid=(depth, count),
        in_specs=[pl.BlockSpec((None, rows, d), lambda l, j: (l, 0, first + j)),
                  pl.BlockSpec((None, rows, LANES), lambda l, j: (l, 0, (first + j + 1) * per))],
        out_specs=pl.BlockSpec((None, None, rows, d), lambda l, j: (l, j, 0, 0)),
        compiler_params=_cparams(2),
        name="realign",
    )(w, w)


def _gate_tables(g):
    GR, H = GATE_ROWS, MLSTM_H
    g_t = g.T[:GR, :]
    lf = _log_sigmoid(g_t)
    fwd = lax.broadcasted_iota(jnp.int32, (GR, CHUNK), 0) < 2 * H
    lane = lax.broadcasted_iota(jnp.int32, (GR, CHUNK), 1)

    def scans(x, op, fill):
        left, right = x, x
        k = 1
        while k < CHUNK:
            left = op(left, jnp.where(lane >= k, pltpu.roll(left, k, 1), fill))
            right = op(right, jnp.where(lane < CHUNK - k, pltpu.roll(right, CHUNK - k, 1), fill))
            k *= 2
        return left, right

    pre, suf = scans(lf, jnp.add, 0.0)
    cum = jnp.where(fwd, pre, suf)
    tot = pre + suf - lf
    w = pltpu.roll(g_t, H, 0) - cum
    pm, sm = scans(w, jnp.maximum, -jnp.inf)
    cmax = jnp.where(fwd, pm, sm)
    wmax = jnp.broadcast_to(jnp.max(w, axis=1, keepdims=True), w.shape)
    wt = jnp.concatenate([w, jnp.zeros((LANES - GR, CHUNK), F32)], axis=0).T
    return (cum, tot, cmax, wmax), wt


def _inproj_kernel(*refs, n_side):
    (x_ref, sh_ref, sc_ref, wqk_ref, wv_ref, wo_ref, wg_ref, wf_ref,
     bqk_ref, bv_ref, bo_ref, bg_ref, bf_ref) = refs[:13]
    n_in = 13 + (n_side + N_LOC + 1 if n_side else 0)
    qk_ref, vt_ref, o_ref, f_ref, tab_ref, wt_ref = refs[n_in:n_in + 6]
    if n_side:
        side_in, wins = refs[13:13 + n_side], refs[13 + n_side:n_in]
        side_out, wloc_ref = refs[n_in + 6:n_in + 6 + n_side], refs[n_in + 6 + n_side]
        for src_ref, dst_ref in zip(side_in, side_out):
            dst_ref[...] = src_ref[...].astype(BF16)
        for k in range(N_LOC):
            wloc_ref[k] = jnp.concatenate([wins[k][:, WIN_PHASE:], wins[k + 1][:, :WIN_PHASE]], axis=1)

    u = (x_ref[...] * (1.0 + sc_ref[...]) + sh_ref[...]).astype(BF16)
    d = D_MODEL
    g = _dot(u, wg_ref[...]) + bg_ref[...]
    for c in range(tab_ref.shape[0]):
        rows = slice(c * CHUNK, (c + 1) * CHUNK)
        tabs, wt = _gate_tables(g[rows, :])
        wt_ref[rows, :] = wt
        for k, tab in enumerate(tabs):
            tab_ref[c, k] = tab
    qk_ref[:, :d] = (_dot(u, wqk_ref[:, :d]) + bqk_ref[:, :d]).astype(BF16)
    qk_ref[:, d:] = ((_dot(u, wqk_ref[:, d:]) + bqk_ref[:, d:]) * (MLSTM_DH ** -0.5)).astype(BF16)
    v_t = (_dot(u, wv_ref[...]) + bv_ref[...]).T
    for c in range(vt_ref.shape[0]):
        vt_ref[c] = v_t[:, c * CHUNK:(c + 1) * CHUNK].astype(BF16)
    o_ref[...] = _dot(u, wo_ref[...]) + bo_ref[...]
    f_ref[...] = _dot(u, wf_ref[...]) + bf_ref[...]


def _layer_window(layer, rows, width, k):
    return pl.BlockSpec((None, rows, width), lambda *_: (layer, 0, k), pipeline_mode=pl.Buffered(1))


def _inproj(x2d, mods, seq_len, layer, wts, side=()):
    n = x2d.shape[0]
    tm = 512
    steps = n // tm
    side_rows = [a.shape[1] // steps for a in side]
    assert all(r * steps == a.shape[1] and r % 16 == 0 for r, a in zip(side_rows, side))
    d = D_MODEL
    side_args, side_in, side_shapes, side_out = list(side), [], [], []
    if side:
        side_in = [pl.BlockSpec((None, r, a.shape[2]), lambda i: (layer, i, 0))
                   for r, a in zip(side_rows, side)]
        side_shapes = [jax.ShapeDtypeStruct(a.shape[1:], BF16) for a in side]
        side_out = [pl.BlockSpec((r, a.shape[2]), lambda i: (i, 0)) for r, a in zip(side_rows, side)]
        r = d // steps
        assert r * steps == d and r % 16 == 0
        first = (OFF_CONV - WIN_PHASE) // d
        side_in += [pl.BlockSpec((None, r, d), functools.partial(lambda k, i: (layer, i, first + k), k))
                    for k in range(N_LOC)]
        side_in += [pl.BlockSpec((None, r, LANES), lambda i: (layer, i, (first + N_LOC) * (d // LANES)))]
        side_args += [wts['w_in']] * (N_LOC + 1)
        side_shapes += [jax.ShapeDtypeStruct((N_LOC, d, d), BF16)]
        side_out += [pl.BlockSpec((N_LOC, r, d), lambda i: (0, i, 0))]
    per_mod = max(seq_len // tm, 1) if mods.shape[0] > 1 else n
    mrow = (lambda i: i // per_mod) if mods.shape[0] > 1 else (lambda i: 0)
    wspec = functools.partial(_layer_window, layer)
    resident4 = lambda rows: pl.BlockSpec((None, None, rows, d), lambda i: (layer, 0, 0, 0),
                                          pipeline_mode=pl.Buffered(1))
    specs = lambda rows: [wspec(rows, 2 * d, 0), wspec(rows, d, 2), wspec(rows, d, 3),
                          wspec(rows, LANES, OFF_GATES // LANES), resident4(rows)]
    return pl.pallas_call(
        functools.partial(_inproj_kernel, n_side=len(side)),
        out_shape=(jax.ShapeDtypeStruct((n, 2 * d), BF16),
                   jax.ShapeDtypeStruct((n // CHUNK, d, CHUNK), BF16),
                   jax.ShapeDtypeStruct((n, d), F32),
                   jax.ShapeDtypeStruct((n, d), F32),
                   jax.ShapeDtypeStruct((n // CHUNK, 4, GATE_ROWS, CHUNK), F32),
                   jax.ShapeDtypeStruct((n, LANES), F32)) + tuple(side_shapes),
        grid=(steps,),
        in_specs=[pl.BlockSpec((tm, d), lambda i: (i, 0)),
                  pl.BlockSpec((None, 1, d), lambda i: (mrow(i), 0, 0)),
                  pl.BlockSpec((None, 1, d), lambda i: (mrow(i), 0, 1))]
                 + specs(d) + specs(1) + side_in,
        out_specs=(pl.BlockSpec((tm, 2 * d), lambda i: (i, 0)),
                   pl.BlockSpec((tm // CHUNK, d, CHUNK), lambda i: (i, 0, 0)),
                   pl.BlockSpec((tm, d), lambda i: (i, 0)),
                   pl.BlockSpec((tm, d), lambda i: (i, 0)),
                   pl.BlockSpec((tm // CHUNK, 4, GATE_ROWS, CHUNK), lambda i: (i, 0, 0, 0)),
                   pl.BlockSpec((tm, LANES), lambda i: (i, 0))) + tuple(side_out),
        compiler_params=_cparams(1),
        name="inproj",
    )(x2d, mods, mods, *([wts['w_in']] * 4), wts['w_fou'], *([wts['b_in']] * 4), wts['b_fou'],
      *side_args)


def _mlstm_kernel(*refs, seq_len, seqs, layer, has_init, emit_state, n_prev):
    it = iter(refs)
    qk_ref, vt_ref, tab_ref, wt_ref = (next(it) for _ in range(4))
    if has_init:
        c0_ref, n0_ref, m0_ref = (next(it) for _ in range(3))
    for _ in range(n_prev):
        next(it)
    hm_ref = next(it)
    if emit_state:
        cn_ref, nn_ref, mn_ref = (next(it) for _ in range(3))
    hf_s, hb_s, ct_s, n_s, m_s = it

    nc = seq_len // CHUNK
    ncb = nc * seqs
    H, DH, W = MLSTM_H, MLSTM_DH, MLSTM_W
    HALF = CHUNK // 2
    assert CHUNK == DH
    ri = lax.broadcasted_iota(jnp.int32, (HALF, HALF), 0)
    ci = lax.broadcasted_iota(jnp.int32, (HALF, HALF), 1)
    tri_mask = {False: ri <= ci, True: ri >= ci}

    if has_init:
        for d in range(2):
            for h in range(H):
                ct_s[d, h] = c0_ref[d, h].T
                n_s[d, h] = n0_ref[d, h:h + 1, :]
                m_s[d, h] = jnp.full((1, CHUNK), m0_ref[pl.program_id(0), layer, d, h], F32)
    else:
        m_s[...] = jnp.zeros_like(m_s)
    ones_rows = jnp.ones((16, CHUNK), BF16)

    def chain(c, h, backward, use_state, update_state, q=0):
        static = isinstance(c, int)
        r0 = c * CHUNK if static else pl.multiple_of(c * CHUNK, CHUNK)
        rows = pl.ds(r0, CHUNK)
        d = 2 * q + (1 if backward else 0)
        hc = slice(h * DH, (h + 1) * DH)
        gf = (3 if backward else 1) * H + h
        qc = qk_ref[rows, hc]
        kc = qk_ref[rows, W + h * DH:W + (h + 1) * DH]
        vt1 = jnp.concatenate([vt_ref[c, hc, :], ones_rows], axis=0)
        grow = pl.ds(gf, 1)
        brow = tab_ref[c, 0, grow, :]
        g_tot = tab_ref[c, 1, grow, :]
        cmax = tab_ref[c, 2, grow, :]
        wmax = tab_ref[c, 3, grow, :]
        wb = jnp.broadcast_to(wt_ref[rows, gf:gf + 1], (CHUNK, CHUNK))
        m_prev = m_s[d, h]

        inter = brow + m_prev
        m_t = jnp.maximum(inter, brow + cmax)
        e_row = brow - m_t
        nt_dims = (((1,), (1,)), ((), ()))
        if use_state:
            n_prev = n_s[d, h]
            lhs = jnp.concatenate([kc, jnp.broadcast_to(n_prev.astype(BF16), (16, DH)),
                                   ct_s[d, h].astype(BF16)], axis=0)
            r = lax.dot_general(lhs, qc, nt_dims, preferred_element_type=F32)
            qk_t, nq, num_c = r[:CHUNK], r[CHUNK:CHUNK + 1], r[CHUNK + 16:]
        else:
            qk_t = lax.dot_general(kc, qc, nt_dims, preferred_element_type=F32)

        def quad(si, ti, masked):
            rs, cs = slice(si * HALF, (si + 1) * HALF), slice(ti * HALF, (ti + 1) * HALF)
            arg = wb[rs, cs] + e_row[:, cs]
            if masked:
                arg = jnp.where(tri_mask[backward], arg, -jnp.inf)
            return qk_t[rs, cs] * jnp.exp(arg)

        zero = jnp.zeros((HALF, HALF), F32)
        if backward:
            st = jnp.concatenate([jnp.concatenate([quad(0, 0, True), zero], axis=1),
                                  jnp.concatenate([quad(1, 0, False), quad(1, 1, True)], axis=1)], axis=0)
        else:
            st = jnp.concatenate([jnp.concatenate([quad(0, 0, True), quad(0, 1, False)], axis=1),
                                  jnp.concatenate([zero, quad(1, 1, True)], axis=1)], axis=0)
        r1 = _dot(vt1, st.astype(BF16))
        num, den = r1[:DH], r1[DH:DH + 1]
        if use_state:
            a = jnp.exp(inter - m_t)
            num = a * num_c + num
            den = a * nq + den
        hval = num * (1.0 / jnp.maximum(jnp.abs(den), jnp.exp(-m_t)))
        (hb_s if backward else hf_s)[c, h] = hval

        if update_state:
            m_new = jnp.maximum(g_tot + m_prev, wmax + g_tot)
            wk = jnp.exp(wb + (g_tot - m_new))
            kw = kc * wk.astype(BF16)
            r2 = _dot(vt1, kw)
            kv, nsum = r2[:DH], r2[DH:DH + 1]
            if use_state:
                a_c = jnp.exp(g_tot + m_prev - m_new)
                ct_s[d, h] = a_c * ct_s[d, h] + kv
                n_s[d, h] = a_c * n_prev + nsum
            else:
                ct_s[d, h] = kv
                n_s[d, h] = nsum
            m_s[d, h] = m_new

    def step(j, use_state, update_state):
        for q in range(seqs):
            for h in range(H):
                chain(q * nc + j, h, False, use_state, update_state, q)
                chain(q * nc + nc - 1 - j, h, True, use_state, update_state, q)

    first = 0
    if not has_init:
        step(0, False, True)
        first = 1
    last = nc if emit_state else nc - 1
    if last - first <= 2:
        for j in range(first, last):
            step(j, True, True)
    else:
        def body(j, carry):
            step(j, True, True)
            return carry
        lax.fori_loop(first, last, body, 0)
    if not emit_state:
        step(nc - 1, True, False)

    def finalize(c):
        static = isinstance(c, int)
        rows = pl.ds(c * CHUNK if static else pl.multiple_of(c * CHUNK, CHUNK), CHUNK)
        for h in range(H):
            hc = slice(h * DH, (h + 1) * DH)
            ht = hf_s[c, h] + hb_s[c, h]
            mu = jnp.mean(ht, axis=0, keepdims=True)
            cen = ht - mu
            var = jnp.mean(cen * cen, axis=0, keepdims=True)
            hm_ref[rows, hc] = (cen * lax.rsqrt(var + LN_EPS)).T

    if ncb <= 2:
        for c in range(ncb):
            finalize(c)
    else:
        def fbody(c, carry):
            finalize(c)
            return carry
        lax.fori_loop(0, ncb, fbody, 0)

    if emit_state:
        for q in range(seqs):
            for d in range(2):
                for h in range(H):
                    cn_ref[q, d, h] = ct_s[2 * q + d, h].T
                    nn_ref[q, d, h:h + 1, :] = n_s[2 * q + d, h]
                    mn_ref[q, d, h:h + 1, :] = m_s[2 * q + d, h][:, :LANES]


def _mlstm(qk, vt, tabs, wt, batch, seq_len, layer, init_state, prev_state):
    n = batch * seq_len
    w = MLSTM_W
    H, DH = MLSTM_H, MLSTM_DH
    has_init = init_state is not None
    emit_state = not has_init
    seqs = 2 if (seq_len == CHUNK and not has_init and batch % 2 == 0) else 1
    rows = seq_len * seqs
    ncb = rows // CHUNK
    in_specs = [pl.BlockSpec((rows, 2 * w), lambda b: (b, 0)),
                pl.BlockSpec((ncb, w, CHUNK), lambda b: (b, 0, 0)),
                pl.BlockSpec((ncb, 4, GATE_ROWS, CHUNK), lambda b: (b, 0, 0, 0)),
                pl.BlockSpec((rows, LANES), lambda b: (b, 0))]
    args = [qk, vt, tabs, wt]
    init_specs = [pl.BlockSpec((None, None, 2, H, DH, DH), lambda b: (b, layer, 0, 0, 0, 0)),
                  pl.BlockSpec((None, None, 2, H, DH), lambda b: (b, layer, 0, 0, 0))]
    new_specs = [pl.BlockSpec((seqs, None, 2, H, DH, DH), lambda b: (b, layer, 0, 0, 0, 0)),
                 pl.BlockSpec((seqs, None, 2, H, DH), lambda b: (b, layer, 0, 0, 0)),
                 pl.BlockSpec((seqs, None, 2, H, LANES), lambda b: (b, layer, 0, 0, 0))]
    aliases = {}
    if has_init:
        in_specs += init_specs + [pl.BlockSpec(memory_space=pltpu.SMEM)]
        args += list(init_state)
    out_shape = [jax.ShapeDtypeStruct((n, w), F32)]
    out_specs = [pl.BlockSpec((rows, w), lambda b: (b, 0))]
    if emit_state:
        out_shape += [jax.ShapeDtypeStruct((batch, DEPTH, 2, H, DH, DH), F32),
                      jax.ShapeDtypeStruct((batch, DEPTH, 2, H, DH), F32),
                      jax.ShapeDtypeStruct((batch, DEPTH, 2, H, LANES), F32)]
        out_specs += new_specs
        if prev_state is not None:
            aliases = {len(args) + k: 1 + k for k in range(3)}
            in_specs += [pl.BlockSpec(memory_space=pl.ANY)] * 3
            args += list(prev_state)
    scratch = [pltpu.VMEM((ncb, H, DH, CHUNK), F32),
               pltpu.VMEM((ncb, H, DH, CHUNK), F32),
               pltpu.VMEM((2 * seqs, H, DH, DH), F32),
               pltpu.VMEM((2 * seqs, H, 1, DH), F32),
               pltpu.VMEM((2 * seqs, H, 1, CHUNK), F32)]
    return pl.pallas_call(
        functools.partial(_mlstm_kernel, seq_len=seq_len, seqs=seqs, layer=layer, has_init=has_init,
                          emit_state=emit_state, n_prev=len(aliases)),
        out_shape=tuple(out_shape),
        grid=(batch // seqs,),
        in_specs=in_specs,
        out_specs=tuple(out_specs),
        scratch_shapes=scratch,
        input_output_aliases=aliases,
        compiler_params=_cparams(1),
        name="mlstm_init" if has_init else "mlstm_zero",
    )(*args)


def _dft_tables(seq_len):
    gw = FOURIER_GW
    kc = np.arange(gw)
    ang_c = 2.0 * np.pi * ((kc[:, None] * kc[None, :]) % gw) / gw
    w_chan = np.concatenate([np.cos(ang_c), np.sin(ang_c)], axis=1)
    kt = np.arange(seq_len)
    ang_t = 2.0 * np.pi * ((kt[:, None] * kt[None, :]) % seq_len) / seq_len
    scale = 1.0 / np.sqrt(seq_len * gw)
    return (jnp.asarray(w_chan, dtype=F32).astype(BF16),
            jnp.asarray(np.cos(ang_t) * scale, dtype=F32).astype(BF16),
            jnp.asarray(-np.sin(ang_t) * scale, dtype=F32).astype(BF16))


def _fourier_kernel(x_ref, wc_ref, ct_ref, st_ref, out_ref, *, seq_len):
    gw = FOURIER_GW
    for g in range(FOURIER_G):
        cols = slice(g * gw, (g + 1) * gw)
        a = _dot(x_ref[:, cols].astype(BF16), wc_ref[...])
        for r in range(0, x_ref.shape[0], seq_len):
            a_c = a[r:r + seq_len, :gw].astype(BF16)
            a_s = a[r:r + seq_len, gw:].astype(BF16)
            y = _dot(ct_ref[...], a_c) + _dot(st_ref[...], a_s)
            out_ref[r:r + seq_len, cols] = y.astype(BF16)


def _fourier(xf, batch, seq_len):
    w_chan, c_t, s_t = _dft_tables(seq_len)
    n = batch * seq_len
    tm = max(seq_len, 1024)
    return pl.pallas_call(
        functools.partial(_fourier_kernel, seq_len=seq_len),
        out_shape=jax.ShapeDtypeStruct((n, FOURIER_W), BF16),
        grid=(n // tm,),
        in_specs=[pl.BlockSpec((tm, FOURIER_W), lambda b: (b, 0)),
                  _resident(w_chan.shape), _resident(c_t.shape), _resident(s_t.shape)],
        out_specs=pl.BlockSpec((tm, FOURIER_W), lambda b: (b, 0)),
        compiler_params=_cparams(1),
        name="fourier",
    )(xf, w_chan, c_t, s_t)


def _merge_kernel(*refs, row_len):
    (x_ref, sh_ref, sc_ref, gt_ref, hn_ref, o_ref, hf_ref, wl_ref, bl_ref, ng_ref, cw_ref, cb_ref,
     wm_ref, wf_ref, wc_ref, wo_ref, lg_ref, lb_ref, out_ref) = refs
    half = x_ref.shape[0] // 2
    assert half % row_len == 0
    for r in range(2):
        rows = slice(r * half, (r + 1) * half)
        x = x_ref[rows, :]
        u = (x * (1.0 + sc_ref[...]) + sh_ref[...]).astype(BF16)

        def seg(k):
            return _dot(u, wl_ref[k]) + bl_ref[k]

        uc = seg(1) * seg(2)
        pos = lax.broadcasted_iota(jnp.int32, (half, 1), 0) % row_len
        prev = jnp.where(pos == 0, 0.0, pltpu.roll(uc, 1, 0))
        nxt = jnp.where(pos == row_len - 1, 0.0, pltpu.roll(uc, half - 1, 0))
        yc = prev * cw_ref[0:1, :] + uc * cw_ref[1:2, :] + nxt * cw_ref[2:3, :] + cb_ref[...]
        h_c = (seg(0) * yc).astype(BF16)
        h_m = (_sigmoid(o_ref[rows, :]) * hn_ref[rows, :] * ng_ref[...]).astype(BF16)
        merged = (_sigmoid(seg(3)) * _dot(h_m, wm_ref[...])
                  + _sigmoid(seg(4)) * _dot(hf_ref[rows, :], wf_ref[...])
                  + _sigmoid(seg(5)) * _dot(h_c, wc_ref[...]))
        y = _dot(merged.astype(BF16), wo_ref[...])
        out_ref[rows, :] = _layer_norm(ALPHA * x + gt_ref[...] * y, lg_ref[...], lb_ref[...])


def _merge(x2d, mods, seq_len, row_len, layer, h_n, o, h_f, wts, branch_w):
    n = x2d.shape[0]
    d = D_MODEL
    tm = 512
    per_mod = seq_len // tm
    mrow = (lambda i: i // per_mod) if mods.shape[0] > 1 else (lambda i: 0)
    tile = lambda i: (i, 0)
    wspec = functools.partial(_layer_window, layer)
    w_loc = branch_w[4]
    return pl.pallas_call(
        functools.partial(_merge_kernel, row_len=row_len),
        out_shape=jax.ShapeDtypeStruct((n, d), F32),
        grid=(n // tm,),
        in_specs=[pl.BlockSpec((tm, d), tile),
                  pl.BlockSpec((None, 1, d), lambda i: (mrow(i), 0, 0)),
                  pl.BlockSpec((None, 1, d), lambda i: (mrow(i), 0, 1)),
                  pl.BlockSpec((None, 1, d), lambda i: (mrow(i), 0, 2)),
                  pl.BlockSpec((tm, d), tile), pl.BlockSpec((tm, d), tile), pl.BlockSpec((tm, d), tile),
                  _resident(w_loc.shape),
                  pl.BlockSpec((None, N_LOC, 1, d), lambda i: (layer, 0, 0, 0), pipeline_mode=pl.Buffered(1))]
                 + [wspec(1, d, 0), wspec(3, d, 0), wspec(1, d, 0)]
                 + [_resident(w.shape) for w in branch_w[:4]]
                 + [pl.BlockSpec((None, 1, d), lambda i: (2 * layer, 0, 0)),
                    pl.BlockSpec((None, 1, d), lambda i: (2 * layer, 0, 0))],
        out_specs=pl.BlockSpec((tm, d), tile),
        compiler_params=_cparams(1),
        name="merge",
    )(x2d, mods, mods, mods, h_n, o, h_f, w_loc, wts['b_loc'], wts['norm_g'], wts['conv_w'],
      wts['conv_b'], *branch_w[:4], wts['ln_g'], wts['ln_b'])


def _ffn_kernel(x_ref, sh_ref, sc_ref, gt_ref, wgu_ref, wd_ref, lg_ref, lb_ref, out_ref):
    half = x_ref.shape[0] // 2
    for r in range(2):
        rows = slice(r * half, (r + 1) * half)
        x = x_ref[rows, :]
        u = (x * (1.0 + sc_ref[...]) + sh_ref[...]).astype(BF16)
        a = _dot(u, wgu_ref[:, :D_FF])
        b = _dot(u, wgu_ref[:, D_FF:])
        hidden = (a * _sigmoid(a) * b).astype(BF16)
        y = _dot(hidden, wd_ref[...])
        out_ref[rows, :] = _layer_norm(ALPHA * x + gt_ref[...] * y, lg_ref[...], lb_ref[...])


def _ffn(x2d, mods, seq_len, layer, w_gu_bf, w_d_bf, ln_g, ln_b):
    n = x2d.shape[0]
    d = D_MODEL
    tm = 512
    per_mod = seq_len // tm
    mrow = (lambda i: i // per_mod) if mods.shape[0] > 1 else (lambda i: 0)
    tile = lambda i: (i, 0)
    return pl.pallas_call(
        _ffn_kernel,
        out_shape=jax.ShapeDtypeStruct((n, d), F32),
        grid=(n // tm,),
        in_specs=[pl.BlockSpec((tm, d), tile),
                  pl.BlockSpec((None, 1, d), lambda i: (mrow(i), 0, 3)),
                  pl.BlockSpec((None, 1, d), lambda i: (mrow(i), 0, 4)),
                  pl.BlockSpec((None, 1, d), lambda i: (mrow(i), 0, 5)),
                  _resident(w_gu_bf.shape), _resident(w_d_bf.shape),
                  pl.BlockSpec((None, 1, d), lambda i: (2 * layer + 1, 0, 0)),
                  pl.BlockSpec((None, 1, d), lambda i: (2 * layer + 1, 0, 0))],
        out_specs=pl.BlockSpec((tm, d), tile),
        compiler_params=_cparams(1),
        name="ffn",
    )(x2d, mods, mods, mods, w_gu_bf, w_d_bf, ln_g, ln_b)


def _prepared_weights(w_in, b_in, mlstm_norm_g, conv_w, conv_b, w_br_mlstm, w_br_fourier, w_br_conv,
                      w_out, ln_g, ln_b, w_gate_up, w_down):
    d = D_MODEL
    b_in3 = b_in[:, None, :]
    first_f = OFF_GATES // d
    w_in_bf = w_in.astype(BF16)
    return dict(
        w_in=w_in_bf, b_in=b_in3,
        w_fou=_realigned(w_in_bf, first_f, 1), b_fou=b_in3[:, :, OFF_FOURIER:OFF_CONV][:, None],
        b_loc=b_in[:, OFF_CONV:].reshape(DEPTH, N_LOC, 1, d),
        norm_g=mlstm_norm_g[:, None, :], conv_w=conv_w, conv_b=conv_b[:, None, :],
        ln_g=ln_g.reshape(2 * DEPTH, 1, D_MODEL), ln_b=ln_b.reshape(2 * DEPTH, 1, D_MODEL),
        late=(w_br_mlstm, w_br_fourier, w_br_conv, w_out, w_gate_up, w_down))


def _trunk_layer(x2d, mods, batch, seq_len, row_len, layer, wts, init_state, prev_state, late_bf):
    res_in = _inproj(x2d, mods, seq_len, layer, wts, () if late_bf is not None else wts['late'])
    qk, vt, o, xf, tabs, wt = res_in[:6]
    if late_bf is None:
        late_bf = res_in[6:]
    res = _mlstm(qk, vt, tabs, wt, batch, seq_len, layer, init_state, prev_state)
    h_f = _fourier(xf, batch, seq_len)
    x1 = _merge(x2d, mods, seq_len, row_len, layer, res[0], o, h_f, wts, late_bf[:4] + late_bf[6:])
    x2 = _ffn(x1, mods, seq_len, layer, late_bf[4], late_bf[5], wts['ln_g'], wts['ln_b'])
    return x2, res[1:], late_bf


def kernel(x_prompt, x_sample, state_C, state_n, state_m, c, c_ctx, w_ada, b_ada, w_in, b_in,
           mlstm_norm_g, conv_w, conv_b, w_br_mlstm, w_br_fourier, w_br_conv, w_out, ln_g, ln_b,
           w_gate_up, w_down):
    bp, tp, d = x_prompt.shape
    bs, ts, _ = x_sample.shape
    cond = jnp.concatenate([c_ctx[None, :], c, jnp.zeros((8 - 1 - bs, d), F32)], axis=0)
    mods = _ada_mods(cond, w_ada, b_ada)
    wts = _prepared_weights(w_in, b_in, mlstm_norm_g, conv_w, conv_b, w_br_mlstm, w_br_fourier,
                            w_br_conv, w_out, ln_g, ln_b, w_gate_up, w_down)

    xp = x_prompt.reshape(bp * tp, d)
    xs = _add_pos(x_sample, _grid_pos_embed(ts // GRID_W, x_sample.dtype)).reshape(bs * ts, d)
    new_state = None
    for l in range(DEPTH):
        mods_ctx = mods[l, 0:1].reshape(1, 1, 6 * d)
        mods_smp = mods[l, 1:1 + bs].reshape(bs, 1, 6 * d)
        xp, new_state, late_bf = _trunk_layer(xp, mods_ctx, bp, tp, tp, l, wts, None, new_state, None)
        xs, _, _ = _trunk_layer(xs, mods_smp, bs, ts, GRID_W, l, wts, (state_C, state_n, state_m), None,
                                late_bf)
    c_new, n_new, m_new = new_state
    return (xp.reshape(bp, tp, d), xs.reshape(bs, ts, d), c_new, n_new, m_new[..., 0])
```

```python
import functools

import numpy as np
import jax
import jax.numpy as jnp
from jax import lax
from jax.experimental import pallas as pl
from jax.experimental.pallas import tpu as pltpu

D_MODEL = 1024
DEPTH = 2
GRID_W = 64
MLSTM_H = 4
MLSTM_DH = 256
MLSTM_W = MLSTM_H * MLSTM_DH
CHUNK = 256
GATE_ROWS = 4 * MLSTM_H
FOURIER_G = 4
FOURIER_GW = 256
FOURIER_W = FOURIER_G * FOURIER_GW
CONV_W = 1024
D_FF = -(-8 * D_MODEL // (3 * 256)) * 256
ALPHA = (2 * DEPTH) ** 0.25
LN_EPS = 1e-5
POS_BASE = 10000.0

OFF_GATES = 4 * MLSTM_W
OFF_FOURIER = OFF_GATES + 4 * MLSTM_H
OFF_CONV = OFF_FOURIER + FOURIER_W
OFF_MERGE = OFF_CONV + 3 * CONV_W
N_IN = OFF_MERGE + 3 * D_MODEL

LANES = 128
WIN_PHASE = OFF_FOURIER % LANES
N_LOC = 6
VMEM_LIMIT = 56 * 1024 * 1024

BF16 = jnp.bfloat16
F32 = jnp.float32


def _cparams(n_axes):
    return pltpu.CompilerParams(dimension_semantics=("arbitrary",) * n_axes,
                                vmem_limit_bytes=VMEM_LIMIT)


def _resident(shape):
    zeros = (0,) * len(shape)
    return pl.BlockSpec(shape, lambda *_: zeros, pipeline_mode=pl.Buffered(1))


def _sigmoid(x):
    return 1.0 / (1.0 + jnp.exp(-x))


def _log_sigmoid(x):
    return jnp.minimum(x, 0.0) - jnp.log1p(jnp.exp(-jnp.abs(x)))


def _layer_norm(x, g, b):
    mu = jnp.mean(x, axis=-1, keepdims=True)
    xc = x - mu
    var = jnp.mean(xc * xc, axis=-1, keepdims=True)
    return xc * lax.rsqrt(var + LN_EPS) * g + b


def _dot(a, b):
    return jnp.dot(a, b, preferred_element_type=F32)


def _ada_kernel(cond_ref, w_ref, b_ref, out_ref):
    cond = cond_ref[...]
    act = (cond * _sigmoid(cond)).astype(BF16)
    out_ref[...] = _dot(act, w_ref[...].astype(BF16)) + b_ref[...]


def _ada_mods(cond, w_ada, b_ada):
    tn = 1536
    nb = 6 * D_MODEL // tn
    return pl.pallas_call(
        _ada_kernel,
        out_shape=jax.ShapeDtypeStruct((DEPTH, 8, 6 * D_MODEL), F32),
        grid=(DEPTH, nb),
        in_specs=[pl.BlockSpec((8, D_MODEL), lambda l, j: (0, 0)),
                  pl.BlockSpec((None, D_MODEL, tn), lambda l, j: (l, 0, j)),
                  pl.BlockSpec((None, 1, tn), lambda l, j: (l, 0, j))],
        out_specs=pl.BlockSpec((None, 8, tn), lambda l, j: (l, 0, j)),
        compiler_params=_cparams(2),
        name="ada_mods",
    )(cond, w_ada, b_ada.reshape(DEPTH, 1, 6 * D_MODEL))


def _grid_pos_embed(rows, dtype):
    quarter = D_MODEL // 4
    freqs = (POS_BASE ** (-np.arange(quarter, dtype=np.float32) / quarter)).astype(np.float32)
    r = np.repeat(np.arange(rows, dtype=np.float32), GRID_W)[:, None] * freqs
    col = np.tile(np.arange(GRID_W, dtype=np.float32), rows)[:, None] * freqs
    return jnp.asarray(np.concatenate([np.sin(r), np.cos(r), np.sin(col), np.cos(col)], axis=-1), dtype)


def _realign_kernel(a_ref, b_ref, out_ref):
    out_ref[...] = jnp.concatenate([a_ref[:, WIN_PHASE:], b_ref[:, :WIN_PHASE]], axis=1)


def _realigned(w, first, count):
    depth, rows, _ = w.shape
    d = D_MODEL
    per = d // LANES
    return pl.pallas_call(
        _realign_kernel,
        out_shape=jax.ShapeDtypeStruct((depth, count, rows, d), w.dtype),
        grid=(depth, count),
        in_specs=[pl.BlockSpec((None, rows, d), lambda l, j: (l, 0, first + j)),
                  pl.BlockSpec((None, rows, LANES), lambda l, j: (l, 0, (first + j + 1) * per))],
        out_specs=pl.BlockSpec((None, None, rows, d), lambda l, j: (l, j, 0, 0)),
        compiler_params=_cparams(2),
        name="realign",
    )(w, w)


def _gate_tables(g):
    GR, H = GATE_ROWS, MLSTM_H
    g_t = g.T[:GR, :]
    lf = _log_sigmoid(g_t)
    fwd = lax.broadcasted_iota(jnp.int32, (GR, CHUNK), 0) < 2 * H
    lane = lax.broadcasted_iota(jnp.int32, (GR, CHUNK), 1)

    def scans(x, op, fill):
        left, right = x, x
        k = 1
        while k < CHUNK:
            left = op(left, jnp.where(lane >= k, pltpu.roll(left, k, 1), fill))
            right = op(right, jnp.where(lane < CHUNK - k, pltpu.roll(right, CHUNK - k, 1), fill))
            k *= 2
        return left, right

    pre, suf = scans(lf, jnp.add, 0.0)
    cum = jnp.where(fwd, pre, suf)
    tot = pre + suf - lf
    w = pltpu.roll(g_t, H, 0) - cum
    pm, sm = scans(w, jnp.maximum, -jnp.inf)
    cmax = jnp.where(fwd, pm, sm)
    wmax = jnp.broadcast_to(jnp.max(w, axis=1, keepdims=True), w.shape)
    wt = jnp.concatenate([w, jnp.zeros((LANES - GR, CHUNK), F32)], axis=0).T
    return (cum, tot, cmax, wmax), wt


def _inproj_kernel(*refs, n_side, has_pos):
    (x_ref, sh_ref, sc_ref, wqk_ref, wv_ref, wo_ref, wg_ref, wf_ref,
     bqk_ref, bv_ref, bo_ref, bg_ref, bf_ref) = refs[:13]
    pos_ref = refs[13] if has_pos else None
    refs = refs[:13] + refs[13 + int(has_pos):]
    n_in = 13 + (n_side + N_LOC + 1 if n_side else 0)
    qk_ref, vt_ref, o_ref, f_ref, tab_ref, wt_ref = refs[n_in:n_in + 6]
    if n_side:
        side_in, wins = refs[13:13 + n_side], refs[13 + n_side:n_in]
        side_out, wloc_ref = refs[n_in + 6:n_in + 6 + n_side], refs[n_in + 6 + n_side]
        for src_ref, dst_ref in zip(side_in, side_out):
            dst_ref[...] = src_ref[...].astype(BF16)
        for k in range(N_LOC):
            wloc_ref[k] = jnp.concatenate([wins[k][:, WIN_PHASE:], wins[k + 1][:, :WIN_PHASE]], axis=1)

    x = x_ref[...]
    if has_pos:
        x = x + pos_ref[...]
    u = (x * (1.0 + sc_ref[...]) + sh_ref[...]).astype(BF16)
    d = D_MODEL
    g = _dot(u, wg_ref[...]) + bg_ref[...]
    for c in range(tab_ref.shape[0]):
        rows = slice(c * CHUNK, (c + 1) * CHUNK)
        tabs, wt = _gate_tables(g[rows, :])
        wt_ref[rows, :] = wt
        for k, tab in enumerate(tabs):
            tab_ref[c, k] = tab
    qk_ref[:, :d] = (_dot(u, wqk_ref[:, :d]) + bqk_ref[:, :d]).astype(BF16)
    qk_ref[:, d:] = ((_dot(u, wqk_ref[:, d:]) + bqk_ref[:, d:]) * (MLSTM_DH ** -0.5)).astype(BF16)
    v_t = (_dot(u, wv_ref[...]) + bv_ref[...]).T
    for c in range(vt_ref.shape[0]):
        vt_ref[c] = v_t[:, c * CHUNK:(c + 1) * CHUNK].astype(BF16)
    o_ref[...] = _dot(u, wo_ref[...]) + bo_ref[...]
    f_ref[...] = _dot(u, wf_ref[...]) + bf_ref[...]


def _pos_tile(pos, tm):
    if pos is None:
        return [], []
    per_seq = pos.shape[0] // tm
    return [pl.BlockSpec((tm, pos.shape[1]), lambda i: (i % per_seq, 0))], [pos]


def _layer_window(layer, rows, width, k):
    return pl.BlockSpec((None, rows, width), lambda *_: (layer, 0, k), pipeline_mode=pl.Buffered(1))


def _inproj(x2d, mods, seq_len, layer, wts, side=(), pos=None):
    n = x2d.shape[0]
    tm = 512
    steps = n // tm
    side_rows = [a.shape[1] // steps for a in side]
    assert all(r * steps == a.shape[1] and r % 16 == 0 for r, a in zip(side_rows, side))
    d = D_MODEL
    side_args, side_in, side_shapes, side_out = list(side), [], [], []
    if side:
        side_in = [pl.BlockSpec((None, r, a.shape[2]), lambda i: (layer, i, 0))
                   for r, a in zip(side_rows, side)]
        side_shapes = [jax.ShapeDtypeStruct(a.shape[1:], BF16) for a in side]
        side_out = [pl.BlockSpec((r, a.shape[2]), lambda i: (i, 0)) for r, a in zip(side_rows, side)]
        r = d // steps
        assert r * steps == d and r % 16 == 0
        first = (OFF_CONV - WIN_PHASE) // d
        side_in += [pl.BlockSpec((None, r, d), functools.partial(lambda k, i: (layer, i, first + k), k))
                    for k in range(N_LOC)]
        side_in += [pl.BlockSpec((None, r, LANES), lambda i: (layer, i, (first + N_LOC) * (d // LANES)))]
        side_args += [wts['w_in']] * (N_LOC + 1)
        side_shapes += [jax.ShapeDtypeStruct((N_LOC, d, d), BF16)]
        side_out += [pl.BlockSpec((N_LOC, r, d), lambda i: (0, i, 0))]
    per_mod = max(seq_len // tm, 1) if mods.shape[0] > 1 else n
    mrow = (lambda i: i // per_mod) if mods.shape[0] > 1 else (lambda i: 0)
    pos_in, pos_args = _pos_tile(pos, tm)
    wspec = functools.partial(_layer_window, layer)
    resident4 = lambda rows: pl.BlockSpec((None, None, rows, d), lambda i: (layer, 0, 0, 0),
                                          pipeline_mode=pl.Buffered(1))
    specs = lambda rows: [wspec(rows, 2 * d, 0), wspec(rows, d, 2), wspec(rows, d, 3),
                          wspec(rows, LANES, OFF_GATES // LANES), resident4(rows)]
    return pl.pallas_call(
        functools.partial(_inproj_kernel, n_side=len(side), has_pos=pos is not None),
        out_shape=(jax.ShapeDtypeStruct((n, 2 * d), BF16),
                   jax.ShapeDtypeStruct((n // CHUNK, d, CHUNK), BF16),
                   jax.ShapeDtypeStruct((n, d), F32),
                   jax.ShapeDtypeStruct((n, d), F32),
                   jax.ShapeDtypeStruct((n // CHUNK, 4, GATE_ROWS, CHUNK), F32),
                   jax.ShapeDtypeStruct((n, LANES), F32)) + tuple(side_shapes),
        grid=(steps,),
        in_specs=[pl.BlockSpec((tm, d), lambda i: (i, 0)),
                  pl.BlockSpec((None, 1, d), lambda i: (mrow(i), 0, 0)),
                  pl.BlockSpec((None, 1, d), lambda i: (mrow(i), 0, 1))]
                 + specs(d) + specs(1) + pos_in + side_in,
        out_specs=(pl.BlockSpec((tm, 2 * d), lambda i: (i, 0)),
                   pl.BlockSpec((tm // CHUNK, d, CHUNK), lambda i: (i, 0, 0)),
                   pl.BlockSpec((tm, d), lambda i: (i, 0)),
                   pl.BlockSpec((tm, d), lambda i: (i, 0)),
                   pl.BlockSpec((tm // CHUNK, 4, GATE_ROWS, CHUNK), lambda i: (i, 0, 0, 0)),
                   pl.BlockSpec((tm, LANES), lambda i: (i, 0))) + tuple(side_out),
        compiler_params=_cparams(1),
        name="inproj",
    )(x2d, mods, mods, *([wts['w_in']] * 4), wts['w_fou'], *([wts['b_in']] * 4), wts['b_fou'],
      *pos_args, *side_args)


def _mlstm_kernel(*refs, seq_len, seqs, layer, has_init, emit_state, n_prev):
    it = iter(refs)
    qk_ref, vt_ref, tab_ref, wt_ref = (next(it) for _ in range(4))
    if has_init:
        c0_ref, n0_ref, m0_ref = (next(it) for _ in range(3))
    for _ in range(n_prev):
        next(it)
    hm_ref = next(it)
    if emit_state:
        cn_ref, nn_ref, mn_ref = (next(it) for _ in range(3))
    hf_s, hb_s, ct_s, n_s, m_s = it

    nc = seq_len // CHUNK
    ncb = nc * seqs
    H, DH, W = MLSTM_H, MLSTM_DH, MLSTM_W
    HALF = CHUNK // 2
    assert CHUNK == DH
    ri = lax.broadcasted_iota(jnp.int32, (HALF, HALF), 0)
    ci = lax.broadcasted_iota(jnp.int32, (HALF, HALF), 1)
    tri_mask = {False: ri <= ci, True: ri >= ci}

    if has_init:
        for d in range(2):
            for h in range(H):
                ct_s[d, h] = c0_ref[d, h].T
                n_s[d, h] = n0_ref[d, h:h + 1, :]
                m_s[d, h] = jnp.full((1, CHUNK), m0_ref[pl.program_id(0), layer, d, h], F32)
    else:
        m_s[...] = jnp.zeros_like(m_s)
    ones_rows = jnp.ones((16, CHUNK), BF16)

    def chain(c, h, backward, use_state, update_state, q=0):
        static = isinstance(c, int)
        r0 = c * CHUNK if static else pl.multiple_of(c * CHUNK, CHUNK)
        rows = pl.ds(r0, CHUNK)
        d = 2 * q + (1 if backward else 0)
        hc = slice(h * DH, (h + 1) * DH)
        gf = (3 if backward else 1) * H + h
        qc = qk_ref[rows, hc]
        kc = qk_ref[rows, W + h * DH:W + (h + 1) * DH]
        vt1 = jnp.concatenate([vt_ref[c, hc, :], ones_rows], axis=0)
        grow = pl.ds(gf, 1)
        brow = tab_ref[c, 0, grow, :]
        g_tot = tab_ref[c, 1, grow, :]
        cmax = tab_ref[c, 2, grow, :]
        wmax = tab_ref[c, 3, grow, :]
        wb = jnp.broadcast_to(wt_ref[rows, gf:gf + 1], (CHUNK, CHUNK))
        m_prev = m_s[d, h]

        inter = brow + m_prev
        m_t = jnp.maximum(inter, brow + cmax)
        e_row = brow - m_t
        nt_dims = (((1,), (1,)), ((), ()))
        if use_state:
            n_prev = n_s[d, h]
            lhs = jnp.concatenate([kc, jnp.broadcast_to(n_prev.astype(BF16), (16, DH)),
                                   ct_s[d, h].astype(BF16)], axis=0)
            r = lax.dot_general(lhs, qc, nt_dims, preferred_element_type=F32)
            qk_t, nq, num_c = r[:CHUNK], r[CHUNK:CHUNK + 1], r[CHUNK + 16:]
        else:
            qk_t = lax.dot_general(kc, qc, nt_dims, preferred_element_type=F32)

        def quad(si, ti, masked):
            rs, cs = slice(si * HALF, (si + 1) * HALF), slice(ti * HALF, (ti + 1) * HALF)
            arg = wb[rs, cs] + e_row[:, cs]
            if masked:
                arg = jnp.where(tri_mask[backward], arg, -jnp.inf)
            return qk_t[rs, cs] * jnp.exp(arg)

        zero = jnp.zeros((HALF, HALF), F32)
        if backward:
            st = jnp.concatenate([jnp.concatenate([quad(0, 0, True), zero], axis=1),
                                  jnp.concatenate([quad(1, 0, False), quad(1, 1, True)], axis=1)], axis=0)
        else:
            st = jnp.concatenate([jnp.concatenate([quad(0, 0, True), quad(0, 1, False)], axis=1),
                                  jnp.concatenate([zero, quad(1, 1, True)], axis=1)], axis=0)
        r1 = _dot(vt1, st.astype(BF16))
        num, den = r1[:DH], r1[DH:DH + 1]
        if use_state:
            a = jnp.exp(inter - m_t)
            num = a * num_c + num
            den = a * nq + den
        hval = num * (1.0 / jnp.maximum(jnp.abs(den), jnp.exp(-m_t)))
        (hb_s if backward else hf_s)[c, h] = hval

        if update_state:
            m_new = jnp.maximum(g_tot + m_prev, wmax + g_tot)
            wk = jnp.exp(wb + (g_tot - m_new))
            kw = kc * wk.astype(BF16)
            r2 = _dot(vt1, kw)
            kv, nsum = r2[:DH], r2[DH:DH + 1]
            if use_state:
                a_c = jnp.exp(g_tot + m_prev - m_new)
                ct_s[d, h] = a_c * ct_s[d, h] + kv
                n_s[d, h] = a_c * n_prev + nsum
            else:
                ct_s[d, h] = kv
                n_s[d, h] = nsum
            m_s[d, h] = m_new

    def step(j, use_state, update_state):
        for q in range(seqs):
            for h in range(H):
                chain(q * nc + j, h, False, use_state, update_state, q)
                chain(q * nc + nc - 1 - j, h, True, use_state, update_state, q)

    first = 0
    if not has_init:
        step(0, False, True)
        first = 1
    last = nc if emit_state else nc - 1
    if last - first <= 2:
        for j in range(first, last):
            step(j, True, True)
    else:
        def body(j, carry):
            step(j, True, True)
            return carry
        lax.fori_loop(first, last, body, 0)
    if not emit_state:
        step(nc - 1, True, False)

    def finalize(c):
        static = isinstance(c, int)
        rows = pl.ds(c * CHUNK if static else pl.multiple_of(c * CHUNK, CHUNK), CHUNK)
        for h in range(H):
            hc = slice(h * DH, (h + 1) * DH)
            ht = hf_s[c, h] + hb_s[c, h]
            mu = jnp.mean(ht, axis=0, keepdims=True)
            cen = ht - mu
            var = jnp.mean(cen * cen, axis=0, keepdims=True)
            hm_ref[rows, hc] = (cen * lax.rsqrt(var + LN_EPS)).T

    if ncb <= 2:
        for c in range(ncb):
            finalize(c)
    else:
        def fbody(c, carry):
            finalize(c)
            return carry
        lax.fori_loop(0, ncb, fbody, 0)

    if emit_state:
        for q in range(seqs):
            for d in range(2):
                for h in range(H):
                    cn_ref[q, d, h] = ct_s[2 * q + d, h].T
                    nn_ref[q, d, h:h + 1, :] = n_s[2 * q + d, h]
                    mn_ref[q, d, h:h + 1, :] = m_s[2 * q + d, h][:, :LANES]


def _mlstm(qk, vt, tabs, wt, batch, seq_len, layer, init_state, prev_state):
    n = batch * seq_len
    w = MLSTM_W
    H, DH = MLSTM_H, MLSTM_DH
    has_init = init_state is not None
    emit_state = not has_init
    seqs = 2 if (seq_len == CHUNK and not has_init and batch % 2 == 0) else 1
    rows = seq_len * seqs
    ncb = rows // CHUNK
    in_specs = [pl.BlockSpec((rows, 2 * w), lambda b: (b, 0)),
                pl.BlockSpec((ncb, w, CHUNK), lambda b: (b, 0, 0)),
                pl.BlockSpec((ncb, 4, GATE_ROWS, CHUNK), lambda b: (b, 0, 0, 0)),
                pl.BlockSpec((rows, LANES), lambda b: (b, 0))]
    args = [qk, vt, tabs, wt]
    init_specs = [pl.BlockSpec((None, None, 2, H, DH, DH), lambda b: (b, layer, 0, 0, 0, 0)),
                  pl.BlockSpec((None, None, 2, H, DH), lambda b: (b, layer, 0, 0, 0))]
    new_specs = [pl.BlockSpec((seqs, None, 2, H, DH, DH), lambda b: (b, layer, 0, 0, 0, 0)),
                 pl.BlockSpec((seqs, None, 2, H, DH), lambda b: (b, layer, 0, 0, 0)),
                 pl.BlockSpec((seqs, None, 2, H, LANES), lambda b: (b, layer, 0, 0, 0))]
    aliases = {}
    if has_init:
        in_specs += init_specs + [pl.BlockSpec(memory_space=pltpu.SMEM)]
        args += list(init_state)
    out_shape = [jax.ShapeDtypeStruct((n, w), F32)]
    out_specs = [pl.BlockSpec((rows, w), lambda b: (b, 0))]
    if emit_state:
        out_shape += [jax.ShapeDtypeStruct((batch, DEPTH, 2, H, DH, DH), F32),
                      jax.ShapeDtypeStruct((batch, DEPTH, 2, H, DH), F32),
                      jax.ShapeDtypeStruct((batch, DEPTH, 2, H, LANES), F32)]
        out_specs += new_specs
        if prev_state is not None:
            aliases = {len(args) + k: 1 + k for k in range(3)}
            in_specs += [pl.BlockSpec(memory_space=pl.ANY)] * 3
            args += list(prev_state)
    scratch = [pltpu.VMEM((ncb, H, DH, CHUNK), F32),
               pltpu.VMEM((ncb, H, DH, CHUNK), F32),
               pltpu.VMEM((2 * seqs, H, DH, DH), F32),
               pltpu.VMEM((2 * seqs, H, 1, DH), F32),
               pltpu.VMEM((2 * seqs, H, 1, CHUNK), F32)]
    return pl.pallas_call(
        functools.partial(_mlstm_kernel, seq_len=seq_len, seqs=seqs, layer=layer, has_init=has_init,
                          emit_state=emit_state, n_prev=len(aliases)),
        out_shape=tuple(out_shape),
        grid=(batch // seqs,),
        in_specs=in_specs,
        out_specs=tuple(out_specs),
        scratch_shapes=scratch,
        input_output_aliases=aliases,
        compiler_params=_cparams(1),
        name="mlstm_init" if has_init else "mlstm_zero",
    )(*args)


def _dft_tables(seq_len):
    gw = FOURIER_GW
    kc = np.arange(gw)
    ang_c = 2.0 * np.pi * ((kc[:, None] * kc[None, :]) % gw) / gw
    w_chan = np.concatenate([np.cos(ang_c), np.sin(ang_c)], axis=1)
    kt = np.arange(seq_len)
    ang_t = 2.0 * np.pi * ((kt[:, None] * kt[None, :]) % seq_len) / seq_len
    scale = 1.0 / np.sqrt(seq_len * gw)
    return (jnp.asarray(w_chan, dtype=F32).astype(BF16),
            jnp.asarray(np.cos(ang_t) * scale, dtype=F32).astype(BF16),
            jnp.asarray(-np.sin(ang_t) * scale, dtype=F32).astype(BF16))


def _fourier_kernel(x_ref, wc_ref, ct_ref, st_ref, out_ref, *, seq_len):
    gw = FOURIER_GW
    for g in range(FOURIER_G):
        cols = slice(g * gw, (g + 1) * gw)
        a = _dot(x_ref[:, cols].astype(BF16), wc_ref[...])
        for r in range(0, x_ref.shape[0], seq_len):
            a_c = a[r:r + seq_len, :gw].astype(BF16)
            a_s = a[r:r + seq_len, gw:].astype(BF16)
            y = _dot(ct_ref[...], a_c) + _dot(st_ref[...], a_s)
            out_ref[r:r + seq_len, cols] = y.astype(BF16)


def _fourier(xf, batch, seq_len):
    w_chan, c_t, s_t = _dft_tables(seq_len)
    n = batch * seq_len
    tm = max(seq_len, 1024)
    return pl.pallas_call(
        functools.partial(_fourier_kernel, seq_len=seq_len),
        out_shape=jax.ShapeDtypeStruct((n, FOURIER_W), BF16),
        grid=(n // tm,),
        in_specs=[pl.BlockSpec((tm, FOURIER_W), lambda b: (b, 0)),
                  _resident(w_chan.shape), _resident(c_t.shape), _resident(s_t.shape)],
        out_specs=pl.BlockSpec((tm, FOURIER_W), lambda b: (b, 0)),
        compiler_params=_cparams(1),
        name="fourier",
    )(xf, w_chan, c_t, s_t)


def _merge_kernel(*refs, row_len, has_pos):
    (x_ref, sh_ref, sc_ref, gt_ref, hn_ref, o_ref, hf_ref, wl_ref, bl_ref, ng_ref, cw_ref, cb_ref,
     wm_ref, wf_ref, wc_ref, wo_ref, lg_ref, lb_ref) = refs[:18]
    pos_ref = refs[18] if has_pos else None
    out_ref = refs[-1]
    half = x_ref.shape[0] // 2
    assert half % row_len == 0
    for r in range(2):
        rows = slice(r * half, (r + 1) * half)
        x = x_ref[rows, :]
        if has_pos:
            x = x + pos_ref[rows, :]
        u = (x * (1.0 + sc_ref[...]) + sh_ref[...]).astype(BF16)

        def seg(k):
            return _dot(u, wl_ref[k]) + bl_ref[k]

        uc = seg(1) * seg(2)
        pos = lax.broadcasted_iota(jnp.int32, (half, 1), 0) % row_len
        prev = jnp.where(pos == 0, 0.0, pltpu.roll(uc, 1, 0))
        nxt = jnp.where(pos == row_len - 1, 0.0, pltpu.roll(uc, half - 1, 0))
        yc = prev * cw_ref[0:1, :] + uc * cw_ref[1:2, :] + nxt * cw_ref[2:3, :] + cb_ref[...]
        h_c = (seg(0) * yc).astype(BF16)
        h_m = (_sigmoid(o_ref[rows, :]) * hn_ref[rows, :] * ng_ref[...]).astype(BF16)
        merged = (_sigmoid(seg(3)) * _dot(h_m, wm_ref[...])
                  + _sigmoid(seg(4)) * _dot(hf_ref[rows, :], wf_ref[...])
                  + _sigmoid(seg(5)) * _dot(h_c, wc_ref[...]))
        y = _dot(merged.astype(BF16), wo_ref[...])
        out_ref[rows, :] = _layer_norm(ALPHA * x + gt_ref[...] * y, lg_ref[...], lb_ref[...])


def _merge(x2d, mods, seq_len, row_len, layer, h_n, o, h_f, wts, branch_w, pos=None):
    n = x2d.shape[0]
    d = D_MODEL
    tm = 512
    per_mod = seq_len // tm
    mrow = (lambda i: i // per_mod) if mods.shape[0] > 1 else (lambda i: 0)
    tile = lambda i: (i, 0)
    wspec = functools.partial(_layer_window, layer)
    w_loc = branch_w[4]
    pos_in, pos_args = _pos_tile(pos, tm)
    return pl.pallas_call(
        functools.partial(_merge_kernel, row_len=row_len, has_pos=pos is not None),
        out_shape=jax.ShapeDtypeStruct((n, d), F32),
        grid=(n // tm,),
        in_specs=[pl.BlockSpec((tm, d), tile),
                  pl.BlockSpec((None, 1, d), lambda i: (mrow(i), 0, 0)),
                  pl.BlockSpec((None, 1, d), lambda i: (mrow(i), 0, 1)),
                  pl.BlockSpec((None, 1, d), lambda i: (mrow(i), 0, 2)),
                  pl.BlockSpec((tm, d), tile), pl.BlockSpec((tm, d), tile), pl.BlockSpec((tm, d), tile),
                  _resident(w_loc.shape),
                  pl.BlockSpec((None, N_LOC, 1, d), lambda i: (layer, 0, 0, 0), pipeline_mode=pl.Buffered(1))]
                 + [wspec(1, d, 0), wspec(3, d, 0), wspec(1, d, 0)]
                 + [_resident(w.shape) for w in branch_w[:4]]
                 + [pl.BlockSpec((None, 1, d), lambda i: (2 * layer, 0, 0)),
                    pl.BlockSpec((None, 1, d), lambda i: (2 * layer, 0, 0))] + pos_in,
        out_specs=pl.BlockSpec((tm, d), tile),
        compiler_params=_cparams(1),
        name="merge",
    )(x2d, mods, mods, mods, h_n, o, h_f, w_loc, wts['b_loc'], wts['norm_g'], wts['conv_w'],
      wts['conv_b'], *branch_w[:4], wts['ln_g'], wts['ln_b'], *pos_args)


def _ffn_kernel(x_ref, sh_ref, sc_ref, gt_ref, wgu_ref, wd_ref, lg_ref, lb_ref, out_ref):
    half = x_ref.shape[0] // 2
    for r in range(2):
        rows = slice(r * half, (r + 1) * half)
        x = x_ref[rows, :]
        u = (x * (1.0 + sc_ref[...]) + sh_ref[...]).astype(BF16)
        a = _dot(u, wgu_ref[:, :D_FF])
        b = _dot(u, wgu_ref[:, D_FF:])
        hidden = (a * _sigmoid(a) * b).astype(BF16)
        y = _dot(hidden, wd_ref[...])
        out_ref[rows, :] = _layer_norm(ALPHA * x + gt_ref[...] * y, lg_ref[...], lb_ref[...])


def _ffn(x2d, mods, seq_len, layer, w_gu_bf, w_d_bf, ln_g, ln_b):
    n = x2d.shape[0]
    d = D_MODEL
    tm = 512
    per_mod = seq_len // tm
    mrow = (lambda i: i // per_mod) if mods.shape[0] > 1 else (lambda i: 0)
    tile = lambda i: (i, 0)
    return pl.pallas_call(
        _ffn_kernel,
        out_shape=jax.ShapeDtypeStruct((n, d), F32),
        grid=(n // tm,),
        in_specs=[pl.BlockSpec((tm, d), tile),
                  pl.BlockSpec((None, 1, d), lambda i: (mrow(i), 0, 3)),
                  pl.BlockSpec((None, 1, d), lambda i: (mrow(i), 0, 4)),
                  pl.BlockSpec((None, 1, d), lambda i: (mrow(i), 0, 5)),
                  _resident(w_gu_bf.shape), _resident(w_d_bf.shape),
                  pl.BlockSpec((None, 1, d), lambda i: (2 * layer + 1, 0, 0)),
                  pl.BlockSpec((None, 1, d), lambda i: (2 * layer + 1, 0, 0))],
        out_specs=pl.BlockSpec((tm, d), tile),
        compiler_params=_cparams(1),
        name="ffn",
    )(x2d, mods, mods, mods, w_gu_bf, w_d_bf, ln_g, ln_b)


def _prepared_weights(w_in, b_in, mlstm_norm_g, conv_w, conv_b, w_br_mlstm, w_br_fourier, w_br_conv,
                      w_out, ln_g, ln_b, w_gate_up, w_down):
    d = D_MODEL
    b_in3 = b_in[:, None, :]
    first_f = OFF_GATES // d
    w_in_bf = w_in.astype(BF16)
    return dict(
        w_in=w_in_bf, b_in=b_in3,
        w_fou=_realigned(w_in_bf, first_f, 1), b_fou=b_in3[:, :, OFF_FOURIER:OFF_CONV][:, None],
        b_loc=b_in[:, OFF_CONV:].reshape(DEPTH, N_LOC, 1, d),
        norm_g=mlstm_norm_g[:, None, :], conv_w=conv_w, conv_b=conv_b[:, None, :],
        ln_g=ln_g.reshape(2 * DEPTH, 1, D_MODEL), ln_b=ln_b.reshape(2 * DEPTH, 1, D_MODEL),
        late=(w_br_mlstm, w_br_fourier, w_br_conv, w_out, w_gate_up, w_down))


def _trunk_layer(x2d, mods, batch, seq_len, row_len, layer, wts, init_state, prev_state, late_bf,
                 pos=None):
    res_in = _inproj(x2d, mods, seq_len, layer, wts, () if late_bf is not None else wts['late'], pos)
    qk, vt, o, xf, tabs, wt = res_in[:6]
    if late_bf is None:
        late_bf = res_in[6:]
    res = _mlstm(qk, vt, tabs, wt, batch, seq_len, layer, init_state, prev_state)
    h_f = _fourier(xf, batch, seq_len)
    x1 = _merge(x2d, mods, seq_len, row_len, layer, res[0], o, h_f, wts, late_bf[:4] + late_bf[6:], pos)
    x2 = _ffn(x1, mods, seq_len, layer, late_bf[4], late_bf[5], wts['ln_g'], wts['ln_b'])
    return x2, res[1:], late_bf


def kernel(x_prompt, x_sample, state_C, state_n, state_m, c, c_ctx, w_ada, b_ada, w_in, b_in,
           mlstm_norm_g, conv_w, conv_b, w_br_mlstm, w_br_fourier, w_br_conv, w_out, ln_g, ln_b,
           w_gate_up, w_down):
    bp, tp, d = x_prompt.shape
    bs, ts, _ = x_sample.shape
    cond = jnp.concatenate([c_ctx[None, :], c, jnp.zeros((8 - 1 - bs, d), F32)], axis=0)
    mods = _ada_mods(cond, w_ada, b_ada)
    wts = _prepared_weights(w_in, b_in, mlstm_norm_g, conv_w, conv_b, w_br_mlstm, w_br_fourier,
                            w_br_conv, w_out, ln_g, ln_b, w_gate_up, w_down)

    xp = x_prompt.reshape(bp * tp, d)
    xs = x_sample.reshape(bs * ts, d)
    pos = _grid_pos_embed(ts // GRID_W, x_sample.dtype)
    new_state = None
    for l in range(DEPTH):
        mods_ctx = mods[l, 0:1].reshape(1, 1, 6 * d)
        mods_smp = mods[l, 1:1 + bs].reshape(bs, 1, 6 * d)
        xp, new_state, late_bf = _trunk_layer(xp, mods_ctx, bp, tp, tp, l, wts, None, new_state, None)
        xs, _, _ = _trunk_layer(xs, mods_smp, bs, ts, GRID_W, l, wts, (state_C, state_n, state_m), None,
                                late_bf, pos if l == 0 else None)
    c_new, n_new, m_new = new_state
    return (xp.reshape(bp, tp, d), xs.reshape(bs, ts, d), c_new, n_new, m_new[..., 0])
```
